```python
import jax, jax.numpy as jnp
from jax import lax
import numpy as np

D_MODEL = 1024
BATCH = 8
SEQ = 4096
DEPTH = 4
DEC_BATCH = 16
DEC_SEQ = 2048
PAST_LEN = 128

N_MIXERS = 4
N_A = (DEPTH + N_MIXERS - 1) // N_MIXERS
N_B = (DEPTH + N_MIXERS - 2) // N_MIXERS
N_C = (DEPTH + N_MIXERS - 3) // N_MIXERS
N_D = (DEPTH + N_MIXERS - 4) // N_MIXERS

HEAD_DIM = 64
ROPE_THETA = 10000.0
A_HEADS = D_MODEL // HEAD_DIM
A_KV_HEADS = 4
A_GROUP = A_HEADS // A_KV_HEADS
A_RADIUS = 128
RG_WIDTH = 1408
RG_BLOCKS = 16
RG_BW = RG_WIDTH // RG_BLOCKS
RG_CONV = 4
RG_C = 8.0
GLA_HEADS = 4
GLA_DK = D_MODEL // 2
GLA_DV = D_MODEL
GLA_DKH = GLA_DK // GLA_HEADS
GLA_DVH = GLA_DV // GLA_HEADS
GLA_RANK = 16
GLA_TAU = 16.0
GLA_CHUNK = 64
DIL_GROUPS = ((128, 1), (512, 4), (2048, 16))
N_DIL = len(DIL_GROUPS)
DIL_HEADS = 8
DIL_WIDTH = DIL_HEADS * HEAD_DIM
MOE_GROUPS = 4
MOE_EPG = 8
MOE_TOP_K = 2
MOE_FF = 256
ALPHA = (2 * DEPTH) ** 0.25
BETA = (8 * DEPTH) ** -0.25
LN_EPS = 1e-5
NEG_INF = -1e30

kernel_name = 'hybrid_bidir_encoder_two_batches'

F32 = jnp.float32


def _layer_norm(x, g, b):
    xf = x.astype(F32)
    mu = jnp.mean(xf, -1, keepdims=True)
    xc = xf - mu
    var = jnp.mean(xc * xc, -1, keepdims=True)
    return (xc * lax.rsqrt(var + LN_EPS) * g.astype(F32) + b.astype(F32)).astype(x.dtype)


def _rope(t):
    S, half = t.shape[1], t.shape[-1] // 2
    inv = ROPE_THETA ** (-jnp.arange(half, dtype=F32) / half)
    ang = jnp.arange(S, dtype=F32)[:, None] * inv[None, :]
    bshape = (1, S) + (1,) * (t.ndim - 3) + (half,)
    cos = jnp.cos(ang).reshape(bshape)
    sin = jnp.sin(ang).reshape(bshape)
    tf = t.astype(F32)
    t1, t2 = tf[..., :half], tf[..., half:]
    return jnp.concatenate([t1 * cos - t2 * sin, t2 * cos + t1 * sin], -1).astype(t.dtype)


def _banded_attention(q, k, v, radius, sink=None):
    N, L, Hk, G, dh = q.shape
    blk = radius
    nb = -(-L // blk)
    pad = nb * blk - L
    qp = jnp.pad(q, ((0, 0), (0, pad), (0, 0), (0, 0), (0, 0)))
    kp = jnp.pad(k, ((0, 0), (blk, blk + pad), (0, 0), (0, 0)))
    vp = jnp.pad(v, ((0, 0), (blk, blk + pad), (0, 0), (0, 0)))
    scale = dh ** -0.5
    qi = jnp.arange(blk)
    kj = jnp.arange(3 * blk)

    def one_block(i):
        qb = lax.dynamic_slice_in_dim(qp, i * blk, blk, axis=1).astype(F32)
        kb = lax.dynamic_slice_in_dim(kp, i * blk, 3 * blk, axis=1).astype(F32)
        vb = lax.dynamic_slice_in_dim(vp, i * blk, 3 * blk, axis=1).astype(F32)
        qpos = i * blk + qi
        kpos = i * blk - blk + kj
        valid = (jnp.abs(qpos[:, None] - kpos[None, :]) <= radius) & (kpos >= 0)[None, :] & (kpos < L)[None, :]
        s = jnp.einsum('nqhgd,nkhd->nhgqk', qb, kb) * scale
        s = jnp.where(valid, s, NEG_INF)
        m = jnp.max(s, -1)
        if sink is not None:
            sk = sink.astype(F32)[None, :, :, None]
            m = jnp.maximum(m, sk)
        p = jnp.exp(s - m[..., None])
        l = jnp.sum(p, -1)
        if sink is not None:
            l = l + jnp.exp(sk - m)
        o = jnp.einsum('nhgqk,nkhd->nqhgd', p, vb) / jnp.transpose(l, (0, 3, 1, 2))[..., None]
        lse = jnp.transpose(m + jnp.log(l), (0, 3, 1, 2))
        return o, lse

    o, lse = lax.map(one_block, jnp.arange(nb))
    o = jnp.moveaxis(o, 0, 1).reshape(N, nb * blk, Hk, G, dh)[:, :L]
    lse = jnp.moveaxis(lse, 0, 1).reshape(N, nb * blk, Hk, G)[:, :L]
    return o, lse


def windowed_gqa(x, wqkv, sink, wo):
    B, S, _ = x.shape
    q, k, v = jnp.split(x @ wqkv, [A_HEADS * HEAD_DIM, (A_HEADS + A_KV_HEADS) * HEAD_DIM], axis=-1)
    q = _rope(q.reshape(B, S, A_KV_HEADS, A_GROUP, HEAD_DIM))
    k = _rope(k.reshape(B, S, A_KV_HEADS, HEAD_DIM))
    v = v.reshape(B, S, A_KV_HEADS, HEAD_DIM)
    o, _ = _banded_attention(q, k, v, A_RADIUS, sink.reshape(A_KV_HEADS, A_GROUP))
    return o.reshape(B, S, A_HEADS * HEAD_DIM).astype(x.dtype) @ wo


def _combine_linear(left, right):
    a_l, b_l = left
    a_r, b_r = right
    return a_l * a_r, a_r * b_l + b_r


def _linear_scan(a, b, reverse):
    return lax.associative_scan(_combine_linear, (a, b), reverse=reverse, axis=1)[1]


def rglru_block(x, win, conv_w, conv_b, wa, ba, wi, bi, lam, wo):
    B, S, _ = x.shape
    gate_in, u = jnp.split(x @ win, 2, axis=-1)
    left = RG_CONV // 2
    u = lax.conv_general_dilated(u.astype(F32), conv_w.astype(F32)[:, None, :], (1,),
                                 [(left, RG_CONV - 1 - left)], dimension_numbers=('NWC', 'WIO', 'NWC'),
                                 feature_group_count=RG_WIDTH) + conv_b.astype(F32)
    ub = u.reshape(B, S, RG_BLOCKS, RG_BW)
    r = jax.nn.sigmoid(jnp.einsum('bsnc,zncd->zbsnd', ub, wa.astype(F32)).reshape(2, B, S, RG_WIDTH)
                       + ba.astype(F32)[:, None, None])
    ig = jax.nn.sigmoid(jnp.einsum('bsnc,zncd->zbsnd', ub, wi.astype(F32)).reshape(2, B, S, RG_WIDTH)
                        + bi.astype(F32)[:, None, None])
    log_a = -RG_C * r * jax.nn.softplus(-lam.astype(F32))[:, None, None]
    a = jnp.exp(log_a)
    b_in = jnp.sqrt(-jnp.expm1(2.0 * log_a)) * ig * u[None]
    h = _linear_scan(a[0], b_in[0], False) + _linear_scan(a[1], b_in[1], True)
    y = jax.nn.gelu(gate_in.astype(F32)) * h
    return y.astype(x.dtype) @ wo


def _gla_forward(q, k, v, log_a):
    B, S, H, dk = q.shape
    dv = v.shape[-1]
    C = GLA_CHUNK
    nc = S // C
    q = q.reshape(B, nc, C, H, dk)
    k = k.reshape(B, nc, C, H, dk)
    v = v.reshape(B, nc, C, H, dv)
    b = jnp.cumsum(log_a.reshape(B, nc, C, H, dk), axis=2)
    b_last = b[:, :, -1]
    b_mid = b[:, :, C // 2:C // 2 + 1]
    att = jnp.einsum('bnchd,bnehd->bnhce', q * jnp.exp(b - b_mid), k * jnp.exp(b_mid - b))
    att = jnp.where(jnp.tril(jnp.ones((C, C), bool)), att, 0.0)
    o = jnp.einsum('bnhce,bnehv->bnchv', att, v)
    d_state = jnp.einsum('bnchd,bnchv->bnhdv', k * jnp.exp(b_last[:, :, None] - b), v)

    def step(state, inp):
        ds_n, decay_n = inp
        return decay_n[..., None] * state + ds_n, state

    _, s_prev = lax.scan(step, jnp.zeros((B, H, dk, dv), F32),
                         (jnp.moveaxis(d_state, 1, 0), jnp.moveaxis(jnp.exp(b_last), 1, 0)))
    o = o + jnp.einsum('bnchd,nbhdv->bnchv', q * jnp.exp(b), s_prev)
    return o.reshape(B, S, H, dv)


def gla_block(x, wqkvg, wa1, wa2, ba, norm_g, wo):
    B, S, _ = x.shape
    q, k, v, g = jnp.split(x @ wqkvg, [GLA_DK, 2 * GLA_DK, 2 * GLA_DK + GLA_DV], axis=-1)
    q = q.astype(F32).reshape(B, S, GLA_HEADS, GLA_DKH) * GLA_DKH ** -0.5
    k = k.astype(F32).reshape(B, S, GLA_HEADS, GLA_DKH)
    v = v.astype(F32).reshape(B, S, GLA_HEADS, GLA_DVH)
    z = jnp.einsum('zbsr,zrk->zbsk', jnp.einsum('bsd,zdr->zbsr', x, wa1), wa2) + ba[:, None, None]
    log_a = (jax.nn.log_sigmoid(z.astype(F32)) / GLA_TAU).reshape(2, B, S, GLA_HEADS, GLA_DKH)
    flip = lambda t: jnp.flip(t, axis=1)
    o = _gla_forward(q, k, v, log_a[0]) + flip(_gla_forward(flip(q), flip(k), flip(v), flip(log_a[1])))
    o = o * lax.rsqrt(jnp.mean(o * o, -1, keepdims=True) + LN_EPS) * norm_g.astype(F32)
    o = o.reshape(B, S, GLA_DV) * jax.nn.silu(g.astype(F32))
    return o.astype(x.dtype) @ wo


def _to_strided(t, dil):
    B, S = t.shape[:2]
    rest = t.shape[2:]
    t = jnp.swapaxes(t.reshape((B, S // dil, dil) + rest), 1, 2)
    return t.reshape((B * dil, S // dil) + rest)


def _from_strided(t, B, dil):
    L = t.shape[1]
    rest = t.shape[2:]
    t = jnp.swapaxes(t.reshape((B, dil, L) + rest), 1, 2)
    return t.reshape((B, L * dil) + rest)


def dilated_attention(x, wqkv, wo):
    B, S, _ = x.shape
    qkv = (x @ wqkv).reshape(B, S, N_DIL, 3, DIL_HEADS, HEAD_DIM)
    q = _rope(qkv[:, :, :, 0])
    k = _rope(qkv[:, :, :, 1])
    v = qkv[:, :, :, 2]
    outs, lses = [], []
    for gi, (window, dil) in enumerate(DIL_GROUPS):
        radius = window // (2 * dil)
        qs = _to_strided(q[:, :, gi], dil)[:, :, :, None]
        ks = _to_strided(k[:, :, gi], dil)
        vs = _to_strided(v[:, :, gi], dil)
        o, lse = _banded_attention(qs, ks, vs, radius)
        outs.append(_from_strided(o[:, :, :, 0], B, dil))
        lses.append(_from_strided(lse[:, :, :, 0], B, dil))
    w = jax.nn.softmax(jnp.stack(lses, 0), axis=0)
    o = jnp.einsum('gbsh,gbshd->bshd', w, jnp.stack(outs, 0))
    return o.reshape(B, S, DIL_WIDTH).astype(x.dtype) @ wo


def hier_moe(x, wgr, bgr, wer, ber, wg, wu, wd):
    B, S, D = x.shape
    xt = x.reshape(B * S, D)
    g_logits = (xt @ wgr + bgr).astype(F32)
    g_prob = jax.nn.softmax(g_logits, -1)
    _, g_idx = lax.top_k(g_logits, 1)
    g_w = jnp.take_along_axis(g_prob, g_idx, axis=1)
    e_all = jnp.einsum('td,gde->tge', xt, wer) + ber
    e_logits = jnp.take_along_axis(e_all, g_idx[:, :, None], axis=1)[:, 0].astype(F32)
    top_v, top_i = lax.top_k(e_logits, MOE_TOP_K)
    top_w = jax.nn.softmax(top_v, -1)
    inner = jnp.sum(jax.nn.one_hot(top_i, MOE_EPG, dtype=F32) * top_w[..., None], axis=1)
    combine = (jax.nn.one_hot(g_idx[:, 0], MOE_GROUPS, dtype=F32)[:, :, None]
               * (g_w * inner)[:, None, :]).astype(xt.dtype)
    out = jnp.zeros_like(xt)
    for gi in range(MOE_GROUPS):
        hg = jax.nn.silu(jnp.einsum('td,edf->tef', xt, wg[gi])) * jnp.einsum('td,edf->tef', xt, wu[gi])
        out = out + jnp.einsum('tef,efd->td', hg * combine[:, gi, :, None], wd[gi])
    return out.reshape(B, S, D)


def _trunk(x, p):
    for i in range(DEPTH):
        kind, j = i % N_MIXERS, i // N_MIXERS
        if kind == 0:
            h = windowed_gqa(x, p['a_wqkv'][j], p['a_sink'][j], p['a_wo'][j])
        elif kind == 1:
            h = rglru_block(x, p['b_win'][j], p['b_conv_w'][j], p['b_conv_b'][j], p['b_wa'][j], p['b_ba'][j],
                            p['b_wi'][j], p['b_bi'][j], p['b_lam'][j], p['b_wo'][j])
        elif kind == 2:
            h = gla_block(x, p['c_wqkvg'][j], p['c_wa1'][j], p['c_wa2'][j], p['c_ba'][j], p['c_norm_g'][j],
                          p['c_wo'][j])
        else:
            h = dilated_attention(x, p['d_wqkv'][j], p['d_wo'][j])
        x = _layer_norm(ALPHA * x + h, p['ln_g'][i, 0], p['ln_b'][i, 0])
        f = hier_moe(x, p['m_wgr'][i], p['m_bgr'][i], p['m_wer'][i], p['m_ber'][i], p['m_wg'][i], p['m_wu'][i],
                     p['m_wd'][i])
        x = _layer_norm(ALPHA * x + f, p['ln_g'][i, 1], p['ln_b'][i, 1])
    return x


def setup_inputs(seed: int = 0) -> dict:
    key = jax.random.key(seed)
    ks = jax.random.split(key, 31)
    nrm = lambda k, shape, scale: jax.random.normal(k, shape, F32) * scale
    D = D_MODEL
    u_lam = jax.random.uniform(ks[14], (N_B, 2, RG_WIDTH), F32, minval=0.9, maxval=0.999)
    return {
        'x_prompt': nrm(ks[0], (BATCH, SEQ, D), 1.0),
        'x_sample': nrm(ks[1], (DEC_BATCH, DEC_SEQ, D), 1.0),
        'ln_g': 1.0 + nrm(ks[2], (DEPTH, 2, D), 0.02),
        'ln_b': nrm(ks[3], (DEPTH, 2, D), 0.02),
        'a_wqkv': nrm(ks[4], (N_A, D, (A_HEADS + 2 * A_KV_HEADS) * HEAD_DIM), D ** -0.5),
        'a_sink': nrm(ks[5], (N_A, A_HEADS), 1.0),
        'a_wo': nrm(ks[6], (N_A, A_HEADS * HEAD_DIM, D), BETA * (A_HEADS * HEAD_DIM) ** -0.5),
        'b_win': nrm(ks[7], (N_B, D, 2 * RG_WIDTH), D ** -0.5),
        'b_conv_w': nrm(ks[8], (N_B, RG_CONV, RG_WIDTH), RG_CONV ** -0.5),
        'b_conv_b': nrm(ks[9], (N_B, RG_WIDTH), 0.02),
        'b_wa': nrm(ks[10], (N_B, 2, RG_BLOCKS, RG_BW, RG_BW), RG_BW ** -0.5),
        'b_ba': nrm(ks[11], (N_B, 2, RG_WIDTH), 0.02),
        'b_wi': nrm(ks[12], (N_B, 2, RG_BLOCKS, RG_BW, RG_BW), RG_BW ** -0.5),
        'b_bi': nrm(ks[13], (N_B, 2, RG_WIDTH), 0.02),
        'b_lam': jnp.log(u_lam) - jnp.log1p(-u_lam),
        'b_wo': nrm(ks[15], (N_B, RG_WIDTH, D), BETA * RG_WIDTH ** -0.5),
        'c_wqkvg': nrm(ks[16], (N_C, D, 2 * GLA_DK + 2 * GLA_DV), D ** -0.5),
        'c_wa1': nrm(ks[17], (N_C, 2, D, GLA_RANK), D ** -0.5),
        'c_wa2': nrm(ks[18], (N_C, 2, GLA_RANK, GLA_DK), GLA_RANK ** -0.5),
        'c_ba': nrm(ks[19], (N_C, 2, GLA_DK), 0.02),
        'c_norm_g': 1.0 + nrm(ks[20], (N_C, GLA_DVH), 0.02),
        'c_wo': nrm(ks[21], (N_C, GLA_DV, D), BETA * GLA_DV ** -0.5),
        'd_wqkv': nrm(ks[22], (N_D, D, N_DIL * 3 * DIL_WIDTH), D ** -0.5),
        'd_wo': nrm(ks[23], (N_D, DIL_WIDTH, D), BETA * DIL_WIDTH ** -0.5),
        'm_wgr': nrm(ks[24], (DEPTH, D, MOE_GROUPS), D ** -0.5),
        'm_bgr': nrm(ks[25], (DEPTH, MOE_GROUPS), 0.01),
        'm_wer': nrm(ks[26], (DEPTH, MOE_GROUPS, D, MOE_EPG), D ** -0.5),
        'm_ber': nrm(ks[27], (DEPTH, MOE_GROUPS, MOE_EPG), 0.01),
        'm_wg': nrm(ks[28], (DEPTH, MOE_GROUPS, MOE_EPG, D, MOE_FF), D ** -0.5),
        'm_wu': nrm(ks[29], (DEPTH, MOE_GROUPS, MOE_EPG, D, MOE_FF), D ** -0.5),
        'm_wd': nrm(ks[30], (DEPTH, MOE_GROUPS, MOE_EPG, MOE_FF, D), BETA * MOE_FF ** -0.5),
    }


def reference(x_prompt, x_sample, ln_g, ln_b, a_wqkv, a_sink, a_wo, b_win, b_conv_w, b_conv_b, b_wa, b_ba,
              b_wi, b_bi, b_lam, b_wo, c_wqkvg, c_wa1, c_wa2, c_ba, c_norm_g, c_wo, d_wqkv, d_wo,
              m_wgr, m_bgr, m_wer, m_ber, m_wg, m_wu, m_wd):
    p = dict(ln_g=ln_g, ln_b=ln_b, a_wqkv=a_wqkv, a_sink=a_sink, a_wo=a_wo, b_win=b_win, b_conv_w=b_conv_w,
             b_conv_b=b_conv_b, b_wa=b_wa, b_ba=b_ba, b_wi=b_wi, b_bi=b_bi, b_lam=b_lam, b_wo=b_wo,
             c_wqkvg=c_wqkvg, c_wa1=c_wa1, c_wa2=c_wa2, c_ba=c_ba, c_norm_g=c_norm_g, c_wo=c_wo,
             d_wqkv=d_wqkv, d_wo=d_wo, m_wgr=m_wgr, m_bgr=m_bgr, m_wer=m_wer, m_ber=m_ber,
             m_wg=m_wg, m_wu=m_wu, m_wd=m_wd)
    y_prompt = _trunk(x_prompt, p)
    y_sample = _trunk(x_sample, p)
    return (y_prompt, y_sample)
```

```python
import functools

import jax
import jax.numpy as jnp
from jax import lax
from jax.experimental import pallas as pl
from jax.experimental.pallas import tpu as pltpu

F32 = jnp.float32
BF16 = jnp.bfloat16

D_MODEL = 1024
HEAD_DIM = 64
ROPE_THETA = 10000.0
A_HEADS = 16
A_KV_HEADS = 4
A_RADIUS = 128
RG_WIDTH = 1408
RG_BLOCKS = 16
RG_BW = RG_WIDTH // RG_BLOCKS
RG_CONV = 4
RG_C = 8.0
GLA_HEADS = 4
GLA_DK = 512
GLA_DV = 1024
GLA_DKH = 128
GLA_DVH = 256
GLA_RANK = 16
GLA_TAU = 16.0
GLA_CHUNK = 64
DIL_GROUPS = ((128, 1), (512, 4), (2048, 16))
N_DIL = 3
DIL_HEADS = 8
DIL_WIDTH = 512
MOE_GROUPS = 4
MOE_EPG = 8
MOE_FF = 256
DEPTH = 4
ALPHA = (2 * DEPTH) ** 0.25
LN_EPS = 1e-5
NEG_INF = -1e30

LANES = 128
SUBLANES = 8
VMEM_LIMIT = 52 * 1024 * 1024
ATT_Q = 128
TOK_TILE = 512
SCAN_TILE = 256
RG_NT = RG_WIDTH // LANES
RG_BAND = 3 * LANES
ROUTE_W = LANES


def _cparams(sem):
    return pltpu.CompilerParams(dimension_semantics=sem, vmem_limit_bytes=VMEM_LIMIT)


def _seq_pos(i, rows, segs, dil=1):
    (n_p, s_p), (_, s_s) = segs
    p_tiles = (n_p * s_p) // rows
    in_p = i < p_tiles
    tps = jnp.where(in_p, (s_p // dil) // rows, (s_s // dil) // rows)
    j = lax.rem(i, tps)
    return in_p, tps, j


def _layer_norm(y, g, b):
    mu = jnp.mean(y, axis=-1, keepdims=True)
    yc = y - mu
    var = jnp.mean(yc * yc, axis=-1, keepdims=True)
    return yc * lax.rsqrt(var + LN_EPS) * g + b


def _rope_slab(t, cos, sin_signed):
    lane = lax.broadcasted_iota(jnp.int32, t.shape, 1)
    first_half = (lane % HEAD_DIM) < (HEAD_DIM // 2)
    partner = jnp.where(first_half, pltpu.roll(t, LANES - HEAD_DIM // 2, 1), pltpu.roll(t, HEAD_DIM // 2, 1))
    return t * cos + partner * sin_signed


def _proj_rope_kernel(x_ref, w_ref, cos_ref, sin_ref, *out_refs, n_rope, scale_cols, scale, out_cols):
    xb = x_ref[...].astype(BF16)
    cos = cos_ref[...]
    sin = sin_ref[...]
    col = 0
    for o_ref, width in zip(out_refs, out_cols):
        for c in range(width // LANES):
            slab = (col + c * LANES) // LANES
            y = jnp.dot(xb, w_ref[:, slab * LANES:(slab + 1) * LANES], preferred_element_type=F32)
            if slab < n_rope:
                y = _rope_slab(y, cos, sin)
            if slab * LANES < scale_cols:
                y = y * scale
            o_ref[:, c * LANES:(c + 1) * LANES] = y.astype(o_ref.dtype)
        col += width


def _proj_rope(x, w, cos_tab, sin_tab, segs, *, dil, n_rope, scale_cols, scale, out_cols, out_dtypes):
    T, D = x.shape
    (n_p, s_p), (n_s, s_s) = segs
    l_p, l_s = s_p // dil, s_s // dil
    tj = min(TOK_TILE, l_s, l_p)
    assert l_p % tj == 0 and l_s % tj == 0 and (n_p * s_p) % (dil * tj) == 0
    xv = x.reshape(T // dil, dil * D)
    cosv = cos_tab.reshape(cos_tab.shape[0] // dil, dil * LANES)
    sinv = sin_tab.reshape(sin_tab.shape[0] // dil, dil * LANES)
    p_tiles = (n_p * l_p) // tj

    def decode(jt):
        in_p = jt < p_tiles
        lt = jnp.where(in_p, l_p // tj, l_s // tj)
        jt_loc = jnp.where(in_p, jt, jt - p_tiles)
        b = jt_loc // lt
        j0 = lax.rem(jt_loc, lt)
        return in_p, lt, b, j0

    def out_map(jt, r):
        in_p, lt, b, j0 = decode(jt)
        base = jnp.where(in_p, 0, (n_p * s_p) // tj)
        return (base + b * (lt * dil) + r * lt + j0, 0)

    def tab_map(jt, r):
        _, _, _, j0 = decode(jt)
        return (j0, r)

    kern = functools.partial(_proj_rope_kernel, n_rope=n_rope, scale_cols=scale_cols, scale=scale,
                             out_cols=out_cols)
    n_out = w.shape[1]
    return pl.pallas_call(
        kern,
        grid=(T // dil // tj, dil),
        in_specs=[
            pl.BlockSpec((tj, D), lambda jt, r: (jt, r)),
            pl.BlockSpec((D, n_out), lambda jt, r: (0, 0)),
            pl.BlockSpec((tj, LANES), tab_map),
            pl.BlockSpec((tj, LANES), tab_map),
        ],
        out_specs=[pl.BlockSpec((tj, c), out_map) for c in out_cols],
        out_shape=[jax.ShapeDtypeStruct((T, c), dt) for c, dt in zip(out_cols, out_dtypes)],
        compiler_params=_cparams(("parallel", "parallel")),
    )(xv, w, cosv, sinv)


def _proj_kernel(x_ref, w_ref, *out_refs, out_cols):
    xb = x_ref[...].astype(BF16)
    col = 0
    for o_ref, width in zip(out_refs, out_cols):
        for c in range(width // LANES):
            y = jnp.dot(xb, w_ref[:, col:col + LANES], preferred_element_type=F32)
            o_ref[:, c * LANES:(c + 1) * LANES] = y.astype(o_ref.dtype)
            col += LANES


def _proj(x, w, out_cols, out_dtypes):
    T, D = x.shape
    tm = TOK_TILE
    n_out = w.shape[1]
    assert sum(out_cols) == n_out and T % tm == 0
    return pl.pallas_call(
        functools.partial(_proj_kernel, out_cols=out_cols),
        grid=(T // tm,),
        in_specs=[pl.BlockSpec((tm, D), lambda i: (i, 0)), pl.BlockSpec((D, n_out), lambda i: (0, 0))],
        out_specs=[pl.BlockSpec((tm, c), lambda i: (i, 0)) for c in out_cols],
        out_shape=[jax.ShapeDtypeStruct((T, c), dt) for c, dt in zip(out_cols, out_dtypes)],
        compiler_params=_cparams(("parallel",)),
    )(x, w)


def _band_attn_kernel(*refs, segs, dil, radius, q_per_k, has_sink, want_lse):
    it = iter(refs)
    sink_ref = next(it) if has_sink else None
    q_ref = next(it)
    kp_ref, kc_ref, kn_ref = next(it), next(it), next(it)
    vp_ref, vc_ref, vn_ref = next(it), next(it), next(it)
    o_ref = next(it)
    lse_ref = next(it) if want_lse else None

    i = pl.program_id(0)
    _, tps, j = _seq_pos(i, ATT_Q, segs, dil)
    prev_ok = j > 0
    next_ok = j < tps - 1
    W = 2 * radius + ATT_Q
    row = lax.broadcasted_iota(jnp.int32, (ATT_Q, W), 0)
    col = lax.broadcasted_iota(jnp.int32, (ATT_Q, W), 1)
    rel = col - radius - row
    ok = (jnp.abs(rel) <= radius) & ((col >= radius) | prev_ok) & ((col < radius + ATT_Q) | next_ok)
    bias = jnp.where(ok, 0.0, NEG_INF).astype(F32)

    k_all = jnp.concatenate([kp_ref[...], kc_ref[...], kn_ref[...]], axis=0)
    v_all = jnp.concatenate([vp_ref[...], vc_ref[...], vn_ref[...]], axis=0)
    nk = k_all.shape[1] // LANES
    lane_lo = lax.broadcasted_iota(jnp.int32, (1, LANES), 1) < HEAD_DIM
    bias_m = jnp.concatenate([bias] * q_per_k, axis=0) if q_per_k > 1 else bias
    M = ATT_Q * q_per_k
    zero = jnp.zeros((), BF16)

    for kc in range(nk):
        ksl = k_all[:, kc * LANES:(kc + 1) * LANES]
        vsl = v_all[:, kc * LANES:(kc + 1) * LANES]
        slabs = [kc * q_per_k + t for t in range(q_per_k)]
        qm = jnp.concatenate([q_ref[:, m * LANES:(m + 1) * LANES] for m in slabs], axis=0) \
            if q_per_k > 1 else q_ref[:, kc * LANES:(kc + 1) * LANES]
        o_acc = jnp.zeros((M, LANES), F32)
        lse_acc = jnp.zeros((M, LANES), F32)
        for half in range(2):
            keep = lane_lo if half == 0 else jnp.logical_not(lane_lo)
            kx = jnp.where(keep, ksl, zero)
            vx = jnp.where(keep, vsl, zero)
            s = lax.dot_general(qm, kx, (((1,), (1,)), ((), ())), preferred_element_type=F32) + bias_m
            mx = jnp.max(s, axis=1, keepdims=True)
            if has_sink:
                sk = jnp.concatenate(
                    [jnp.full((ATT_Q, 1), sink_ref[2 * m + half], F32) for m in slabs], axis=0)
                mx = jnp.maximum(mx, sk)
            p = jnp.exp(s - mx)
            l = jnp.sum(p, axis=1, keepdims=True)
            if has_sink:
                l = l + jnp.exp(sk - mx)
            pv = jnp.dot(p.astype(BF16), vx, preferred_element_type=F32)
            o_acc = o_acc + pv / l
            if want_lse:
                lse_acc = jnp.where(keep, mx + jnp.log(l), lse_acc)
        for t, m in enumerate(slabs):
            o_ref[:, m * LANES:(m + 1) * LANES] = o_acc[t * ATT_Q:(t + 1) * ATT_Q].astype(o_ref.dtype)
            if want_lse:
                lse_ref[:, m * LANES:(m + 1) * LANES] = lse_acc[t * ATT_Q:(t + 1) * ATT_Q]


def _band_attn(q, k, v, segs, *, dil, radius, q_per_k, sink=None, want_lse=False):
    T, wq = q.shape
    wk = k.shape[1]
    (n_p, s_p), (n_s, s_s) = segs
    assert ATT_Q % radius == 0
    hb = ATT_Q // radius
    n_halo = T // radius
    nt = T // ATT_Q
    l_p, l_s = s_p // dil, s_s // dil
    assert l_p % ATT_Q == 0 and l_s % ATT_Q == 0
    p_tiles = (n_p * s_p) // ATT_Q

    def out_map(i):
        in_p = i < p_tiles
        nb = jnp.where(in_p, l_p // ATT_Q, l_s // ATT_Q)
        i_loc = jnp.where(in_p, i, i - p_tiles)
        n = i_loc // nb
        jb = lax.rem(i_loc, nb)
        b = n // dil
        r = lax.rem(n, dil)
        base = jnp.where(in_p, 0, p_tiles // dil)
        return (base + b * nb + jb, r)

    in_specs = []
    args = []
    if sink is not None:
        in_specs.append(pl.BlockSpec(memory_space=pltpu.SMEM))
        args.append(sink)
    in_specs.append(pl.BlockSpec((ATT_Q, wq), lambda i: (i, 0)))
    args.append(q)
    for arr in (k, v):
        in_specs += [
            pl.BlockSpec((radius, wk), lambda i: (jnp.maximum(i * hb - 1, 0), 0)),
            pl.BlockSpec((ATT_Q, wk), lambda i: (i, 0)),
            pl.BlockSpec((radius, wk), lambda i: (jnp.minimum((i + 1) * hb, n_halo - 1), 0)),
        ]
        args += [arr, arr, arr]
    out_shape = [jax.ShapeDtypeStruct((T // dil, dil * wq), BF16)]
    out_specs = [pl.BlockSpec((ATT_Q, wq), out_map)]
    if want_lse:
        out_shape.append(jax.ShapeDtypeStruct((T // dil, dil * wq), F32))
        out_specs.append(pl.BlockSpec((ATT_Q, wq), out_map))
    kern = functools.partial(_band_attn_kernel, segs=segs, dil=dil, radius=radius, q_per_k=q_per_k,
                             has_sink=sink is not None, want_lse=want_lse)
    outs = pl.pallas_call(
        kern, grid=(nt,), in_specs=in_specs, out_specs=out_specs, out_shape=out_shape,
        compiler_params=_cparams(("parallel",)),
    )(*args)
    return [o.reshape(T, wq) for o in outs]


def _route(x1, wr_ref, br_ref, comb_ref):
    logits = jnp.dot(x1, wr_ref[...], preferred_element_type=F32, precision=lax.Precision.HIGHEST) + br_ref[...]
    n_e = MOE_GROUPS * MOE_EPG
    gl = [logits[:, n_e + g:n_e + g + 1] for g in range(MOE_GROUPS)]
    gmax = functools.reduce(jnp.maximum, gl)
    gidx = jnp.full(gmax.shape, MOE_GROUPS - 1, jnp.int32)
    for g in range(MOE_GROUPS - 2, -1, -1):
        gidx = jnp.where(gl[g] == gmax, g, gidx)
    gsum = functools.reduce(jnp.add, [jnp.exp(v - gmax) for v in gl])
    g_w = 1.0 / gsum
    el = []
    for e in range(MOE_EPG):
        v = logits[:, (MOE_GROUPS - 1) * MOE_EPG + e:(MOE_GROUPS - 1) * MOE_EPG + e + 1]
        for g in range(MOE_GROUPS - 2, -1, -1):
            v = jnp.where(gidx == g, logits[:, g * MOE_EPG + e:g * MOE_EPG + e + 1], v)
        el.append(v)
    v1 = functools.reduce(jnp.maximum, el)
    i1 = jnp.full(v1.shape, MOE_EPG - 1, jnp.int32)
    for e in range(MOE_EPG - 2, -1, -1):
        i1 = jnp.where(el[e] == v1, e, i1)
    el2 = [jnp.where(i1 == e, -jnp.inf, el[e]) for e in range(MOE_EPG)]
    v2 = functools.reduce(jnp.maximum, el2)
    i2 = jnp.full(v2.shape, MOE_EPG - 1, jnp.int32)
    for e in range(MOE_EPG - 2, -1, -1):
        i2 = jnp.where(el2[e] == v2, e, i2)
    ex = jnp.exp(v2 - v1)
    w1 = 1.0 / (1.0 + ex)
    w2 = ex * w1
    lane = lax.broadcasted_iota(jnp.int32, (x1.shape[0], MOE_EPG), 1)
    inner = (jnp.where(lane == i1, w1, 0.0) + jnp.where(lane == i2, w2, 0.0)) * g_w
    for g in range(MOE_GROUPS):
        comb_ref[g] = jnp.where(gidx == g, inner, 0.0)


def _gelu_tanh(x):
    return 0.5 * x * (1.0 + jnp.tanh(0.7978845608028654 * (x + 0.044715 * x * x * x)))


def _silu(x):
    return x / (1.0 + jnp.exp(-x))


def _out_ln_kernel(*refs, mode, n_h):
    h_refs = refs[:n_h]
    extra_ref = refs[n_h] if mode == "gla" else None
    base = n_h + (1 if mode == "gla" else 0)
    x_ref, wo_ref, g_ref, b_ref, wr_ref, br_ref, x1_ref, comb_ref = refs[base:base + 8]

    if mode == "plain":
        hb = h_refs[0][...]
    elif mode == "rglru":
        gate, hf, hbw = h_refs
        hb = (_gelu_tanh(gate[...].astype(F32)) * (hf[...].astype(F32) + hbw[...].astype(F32))).astype(BF16)
    elif mode == "gla":
        of, ob, gg = h_refs
        o = of[...] + ob[...]
        parts = []
        for h in range(GLA_HEADS):
            oh = o[:, h * GLA_DVH:(h + 1) * GLA_DVH]
            ms = jnp.mean(oh * oh, axis=-1, keepdims=True)
            parts.append(oh * lax.rsqrt(ms + LN_EPS) * extra_ref[...])
        o = jnp.concatenate(parts, axis=1)
        hb = (o * _silu(gg[...].astype(F32))).astype(BF16)
    else:
        os_ = h_refs[:N_DIL]
        ls_ = [r[...] for r in h_refs[N_DIL:]]
        mx = functools.reduce(jnp.maximum, ls_)
        es = [jnp.exp(l - mx) for l in ls_]
        den = functools.reduce(jnp.add, es)
        o = functools.reduce(jnp.add, [(e / den) * r[...].astype(F32) for e, r in zip(es, os_)])
        hb = o.astype(BF16)

    acc = jnp.dot(hb, wo_ref[...], preferred_element_type=F32)
    x1 = _layer_norm(ALPHA * x_ref[...] + acc, g_ref[...], b_ref[...])
    x1_ref[...] = x1
    _route(x1, wr_ref, br_ref, comb_ref)


def _out_ln(hs, x, wo, ln_g, ln_b, wr, br, *, mode, extra=None):
    T, D = x.shape
    tm = TOK_TILE
    in_specs = [pl.BlockSpec((tm, h.shape[1]), lambda i: (i, 0)) for h in hs]
    args = list(hs)
    if mode == "gla":
        in_specs.append(pl.BlockSpec((1, extra.shape[1]), lambda i: (0, 0)))
        args.append(extra)
    in_specs += [
        pl.BlockSpec((tm, D), lambda i: (i, 0)),
        pl.BlockSpec(wo.shape, lambda i: (0, 0)),
        pl.BlockSpec((1, D), lambda i: (0, 0)),
        pl.BlockSpec((1, D), lambda i: (0, 0)),
        pl.BlockSpec((D, ROUTE_W), lambda i: (0, 0)),
        pl.BlockSpec((1, ROUTE_W), lambda i: (0, 0)),
    ]
    args += [x, wo, ln_g, ln_b, wr, br]
    return pl.pallas_call(
        functools.partial(_out_ln_kernel, mode=mode, n_h=len(hs)),
        grid=(T // tm,),
        in_specs=in_specs,
        out_specs=[pl.BlockSpec((tm, D), lambda i: (i, 0)),
                   pl.BlockSpec((MOE_GROUPS, tm, MOE_EPG), lambda i: (0, i, 0))],
        out_shape=[jax.ShapeDtypeStruct((T, D), F32),
                   jax.ShapeDtypeStruct((MOE_GROUPS, T, MOE_EPG), F32)],
        compiler_params=_cparams(("parallel",)),
    )(*args)


def _moe_kernel(x_ref, comb_ref, wg_ref, wu_ref, wd_ref, g_ref, b_ref, o_ref, acc_ref):
    g = pl.program_id(1)

    @pl.when(g == 0)
    def _():
        acc_ref[...] = jnp.zeros_like(acc_ref)

    xb = x_ref[...].astype(BF16)
    comb = comb_ref[0]
    for e in range(MOE_EPG):
        hg = jnp.dot(xb, wg_ref[0, e], preferred_element_type=F32)
        hu = jnp.dot(xb, wu_ref[0, e], preferred_element_type=F32)
        h = _silu(hg) * hu * comb[:, e:e + 1]
        acc_ref[...] += jnp.dot(h.astype(BF16), wd_ref[0, e], preferred_element_type=F32)

    @pl.when(g == MOE_GROUPS - 1)
    def _():
        o_ref[...] = _layer_norm(ALPHA * x_ref[...] + acc_ref[...], g_ref[...], b_ref[...])


def _moe(x1, comb, wg, wu, wd, ln_g, ln_b):
    T, D = x1.shape
    tm = TOK_TILE
    return pl.pallas_call(
        _moe_kernel,
        grid=(T // tm, MOE_GROUPS),
        in_specs=[
            pl.BlockSpec((tm, D), lambda i, g: (i, 0)),
            pl.BlockSpec((1, tm, MOE_EPG), lambda i, g: (g, i, 0)),
            pl.BlockSpec((1, MOE_EPG, D, MOE_FF), lambda i, g: (g, 0, 0, 0)),
            pl.BlockSpec((1, MOE_EPG, D, MOE_FF), lambda i, g: (g, 0, 0, 0)),
            pl.BlockSpec((1, MOE_EPG, MOE_FF, D), lambda i, g: (g, 0, 0, 0)),
            pl.BlockSpec((1, D), lambda i, g: (0, 0)),
            pl.BlockSpec((1, D), lambda i, g: (0, 0)),
        ],
        out_specs=pl.BlockSpec((tm, D), lambda i, g: (i, 0)),
        out_shape=jax.ShapeDtypeStruct((T, D), F32),
        scratch_shapes=[pltpu.VMEM((tm, D), F32)],
        compiler_params=_cparams(("parallel", "arbitrary")),
    )(x1, comb, wg, wu, wd, ln_g, ln_b)


def _rglru_stream(z, u_ref, up_ref, un_ref, w_ref, cw_ref, cb_ref, ba_ref, bi_ref, lam_ref,
                  h_out_ref, a_s, b_s, h_s, carry_ref, prev_ok, next_ok, reset):
    tt = u_ref.shape[0]
    u_mid = u_ref[...]
    up = jnp.where(prev_ok, up_ref[...], 0.0)
    un = jnp.where(next_ok, un_ref[...], 0.0)
    ext = jnp.concatenate([up, u_mid, un], axis=0)
    left = RG_CONV // 2
    u = cb_ref[...]
    for kk in range(RG_CONV):
        off = SUBLANES - left + kk
        u = u + cw_ref[kk:kk + 1, :] * ext[off:off + tt]
    ub = u.astype(BF16)
    sp = jnp.maximum(-lam_ref[...], 0.0) + jnp.log(1.0 + jnp.exp(-jnp.abs(lam_ref[...])))
    for n in range(RG_NT):
        s0 = min(max(n - 1, 0), RG_NT - 3) * LANES
        zz = jnp.dot(ub[:, s0:s0 + RG_BAND], w_ref[n], preferred_element_type=F32)
        sl = slice(n * LANES, (n + 1) * LANES)
        r = jax.nn.sigmoid(zz[:, :LANES] + ba_ref[:, sl])
        ig = jax.nn.sigmoid(zz[:, LANES:] + bi_ref[:, sl])
        log_a = -RG_C * r * sp[:, sl]
        a = jnp.exp(log_a)
        a_s[:, sl] = a
        b_s[:, sl] = jnp.sqrt(-jnp.tanh(log_a) * (a * a + 1.0)) * ig * u[:, sl]

    @pl.when(reset)
    def _():
        carry_ref[...] = jnp.zeros_like(carry_ref)

    n_grp = tt // SUBLANES

    def body(gi, h):
        g = gi if z == 0 else n_grp - 1 - gi
        base = pl.multiple_of(g * SUBLANES, SUBLANES)
        a8 = a_s[pl.ds(base, SUBLANES), :]
        b8 = b_s[pl.ds(base, SUBLANES), :]
        rows = [None] * SUBLANES
        order = range(SUBLANES) if z == 0 else range(SUBLANES - 1, -1, -1)
        for r_ in order:
            h = a8[r_:r_ + 1] * h + b8[r_:r_ + 1]
            rows[r_] = h
        h_s[pl.ds(base, SUBLANES), :] = jnp.concatenate(rows, axis=0)
        return h

    h_last = lax.fori_loop(0, n_grp, body, carry_ref[...])
    carry_ref[...] = h_last
    h_out_ref[...] = h_s[...].astype(h_out_ref.dtype)


def _rglru_scan_kernel(uf_ref, ufp_ref, ufn_ref, ubk_ref, ubp_ref, ubn_ref, wf_ref, wb_ref, cw_ref, cb_ref,
                       ba_ref, bi_ref, lam_ref, hf_ref, hb_ref, a_s, b_s, h_s, cf_ref, cbk_ref, *, segs, nt):
    i = pl.program_id(0)
    tt = uf_ref.shape[0]
    _, tps, j = _seq_pos(i, tt, segs)
    _rglru_stream(0, uf_ref, ufp_ref, ufn_ref, wf_ref, cw_ref, cb_ref, ba_ref.at[0:1], bi_ref.at[0:1],
                  lam_ref.at[0:1], hf_ref, a_s, b_s, h_s, cf_ref, j > 0, j < tps - 1, j == 0)
    ib = nt - 1 - i
    _, tps_b, jb = _seq_pos(ib, tt, segs)
    _rglru_stream(1, ubk_ref, ubp_ref, ubn_ref, wb_ref, cw_ref, cb_ref, ba_ref.at[1:2], bi_ref.at[1:2],
                  lam_ref.at[1:2], hb_ref, a_s, b_s, h_s, cbk_ref, jb > 0, jb < tps_b - 1, jb == tps_b - 1)


def _rglru_scan(u_pre, w_band, conv_w, conv_b, ba, bi, lam, segs):
    T, W = u_pre.shape
    tt = SCAN_TILE
    nt = T // tt
    hpt = tt // SUBLANES
    n_h = T // SUBLANES
    fwd = lambda i: (i, 0)
    bwd = lambda i: (nt - 1 - i, 0)

    def halo_specs(idx):
        return [
            pl.BlockSpec((tt, W), lambda i: (idx(i), 0)),
            pl.BlockSpec((SUBLANES, W), lambda i: (jnp.maximum(idx(i) * hpt - 1, 0), 0)),
            pl.BlockSpec((SUBLANES, W), lambda i: (jnp.minimum((idx(i) + 1) * hpt, n_h - 1), 0)),
        ]

    const2 = lambda i: (0, 0)
    const3 = lambda i: (0, 0, 0)
    in_specs = halo_specs(lambda i: i) + halo_specs(lambda i: nt - 1 - i) + [
        pl.BlockSpec(w_band.shape[1:], const3),
        pl.BlockSpec(w_band.shape[1:], const3),
        pl.BlockSpec(conv_w.shape, const2),
        pl.BlockSpec(conv_b.shape, const2),
        pl.BlockSpec(ba.shape, const2),
        pl.BlockSpec(bi.shape, const2),
        pl.BlockSpec(lam.shape, const2),
    ]
    return pl.pallas_call(
        functools.partial(_rglru_scan_kernel, segs=segs, nt=nt),
        grid=(nt,),
        in_specs=in_specs,
        out_specs=[pl.BlockSpec((tt, W), fwd), pl.BlockSpec((tt, W), bwd)],
        out_shape=[jax.ShapeDtypeStruct((T, W), BF16), jax.ShapeDtypeStruct((T, W), BF16)],
        scratch_shapes=[pltpu.VMEM((tt, W), F32), pltpu.VMEM((tt, W), F32), pltpu.VMEM((tt, W), F32),
                        pltpu.VMEM((1, W), F32), pltpu.VMEM((1, W), F32)],
        compiler_params=_cparams(("arbitrary",)),
    )(u_pre, u_pre, u_pre, u_pre, u_pre, u_pre, w_band[0], w_band[1], conv_w, conv_b, ba, bi, lam)


def _rglru_band_weights(wa, wi):
    def dense(w):
        eye = jnp.eye(RG_BLOCKS, dtype=w.dtype)
        return jnp.einsum("ncd,nm->ncmd", w, eye).reshape(RG_WIDTH, RG_WIDTH)

    out = []
    for z in range(2):
        da, di = dense(wa[z]), dense(wi[z])
        tiles = []
        for n in range(RG_NT):
            s0 = min(max(n - 1, 0), RG_NT - 3) * LANES
            sl = slice(n * LANES, (n + 1) * LANES)
            tiles.append(jnp.concatenate([da[s0:s0 + RG_BAND, sl], di[s0:s0 + RG_BAND, sl]], axis=1))
        out.append(jnp.stack(tiles))
    return jnp.stack(out).astype(BF16)


def _gla_stream(z, q_ref, k_ref, v_ref, lr_ref, wa2_ref, ba_ref, o_ref, st_ref, reset):
    tt = q_ref.shape[0]
    C = GLA_CHUNK

    @pl.when(reset)
    def _():
        st_ref[z] = jnp.zeros(st_ref.shape[1:], F32)

    zz = jnp.dot(lr_ref[...].astype(BF16), wa2_ref[:, z * GLA_DK:(z + 1) * GLA_DK],
                 preferred_element_type=F32) + ba_ref[:, z * GLA_DK:(z + 1) * GLA_DK]
    log_a = -(jnp.maximum(-zz, 0.0) + jnp.log(1.0 + jnp.exp(-jnp.abs(zz)))) / GLA_TAU
    ri = lax.broadcasted_iota(jnp.int32, (C, C), 0)
    ci = lax.broadcasted_iota(jnp.int32, (C, C), 1)
    causal = (ri >= ci) if z == 0 else (ri <= ci)
    tri = causal.astype(F32)
    mid = C // 2 if z == 0 else C - 1 - C // 2
    last = C - 1 if z == 0 else 0
    scale = GLA_DKH ** -0.5
    chunks = range(tt // C) if z == 0 else range(tt // C - 1, -1, -1)
    for c in chunks:
        rs = slice(c * C, (c + 1) * C)
        b = jnp.dot(tri, log_a[rs], preferred_element_type=F32, precision=lax.Precision.HIGHEST)
        b_mid = b[mid:mid + 1]
        b_last = b[last:last + 1]
        qc = q_ref[rs, :] * scale
        kc = k_ref[rs, :]
        qd = (qc * jnp.exp(b - b_mid)).astype(BF16)
        kd = (kc * jnp.exp(b_mid - b)).astype(BF16)
        ks = (kc * jnp.exp(b_last - b)).astype(BF16)
        qb = (qc * jnp.exp(b)).astype(BF16)
        dec = jnp.exp(b_last)
        for h in range(GLA_HEADS):
            ksl = slice(h * GLA_DKH, (h + 1) * GLA_DKH)
            vsl = slice(h * GLA_DVH, (h + 1) * GLA_DVH)
            vc = v_ref[rs, vsl]
            att = lax.dot_general(qd[:, ksl], kd[:, ksl], (((1,), (1,)), ((), ())), preferred_element_type=F32)
            att = jnp.where(causal, att, 0.0)
            o = jnp.dot(att.astype(BF16), vc, preferred_element_type=F32)
            st = st_ref[z, h]
            o = o + lax.dot_general(qb[:, ksl], st.astype(BF16), (((1,), (1,)), ((), ())),
                                    preferred_element_type=F32)
            o_ref[rs, vsl] = o
            upd = lax.dot_general(vc, ks[:, ksl], (((0,), (0,)), ((), ())), preferred_element_type=F32)
            st_ref[z, h] = st * dec[:, ksl] + upd


def _gla_kernel(qf, kf, vf, lf, qb, kb, vb, lb, wa2_ref, ba_ref, of_ref, ob_ref, st_ref, *, segs, nt):
    i = pl.program_id(0)
    tt = qf.shape[0]
    _, tps, j = _seq_pos(i, tt, segs)
    _gla_stream(0, qf, kf, vf, lf, wa2_ref, ba_ref, of_ref, st_ref, j == 0)
    ib = nt - 1 - i
    _, tps_b, jb = _seq_pos(ib, tt, segs)
    _gla_stream(1, qb, kb, vb, lb, wa2_ref, ba_ref, ob_ref, st_ref, jb == tps_b - 1)


def _gla(q, k, v, lr, wa2, ba, segs):
    T = q.shape[0]
    tt = SCAN_TILE
    nt = T // tt
    fwd = lambda i: (i, 0)
    bwd = lambda i: (nt - 1 - i, 0)
    in_specs = []
    for idx in (fwd, bwd):
        in_specs += [pl.BlockSpec((tt, GLA_DK), idx), pl.BlockSpec((tt, GLA_DK), idx),
                     pl.BlockSpec((tt, GLA_DV), idx), pl.BlockSpec((tt, LANES), idx)]
    in_specs += [pl.BlockSpec(wa2.shape, lambda i: (0, 0)), pl.BlockSpec(ba.shape, lambda i: (0, 0))]
    return pl.pallas_call(
        functools.partial(_gla_kernel, segs=segs, nt=nt),
        grid=(nt,),
        in_specs=in_specs,
        out_specs=[pl.BlockSpec((tt, GLA_DV), fwd), pl.BlockSpec((tt, GLA_DV), bwd)],
        out_shape=[jax.ShapeDtypeStruct((T, GLA_DV), F32), jax.ShapeDtypeStruct((T, GLA_DV), F32)],
        scratch_shapes=[pltpu.VMEM((2, GLA_HEADS, GLA_DVH, GLA_DKH), F32)],
        compiler_params=_cparams(("arbitrary",)),
    )(q, k, v, lr, q, k, v, lr, wa2, ba)


def _rope_tables(s_max):
    half = HEAD_DIM // 2
    inv = ROPE_THETA ** (-jnp.arange(half, dtype=F32) / half)
    ang = jnp.arange(s_max, dtype=F32)[:, None] * inv[None, :]
    cos, sin = jnp.cos(ang), jnp.sin(ang)
    cos_t = jnp.tile(jnp.concatenate([cos, cos], axis=1), (1, LANES // HEAD_DIM))
    sin_t = jnp.tile(jnp.concatenate([-sin, sin], axis=1), (1, LANES // HEAD_DIM))
    return cos_t, sin_t


def _router_weights(wgr, bgr, wer, ber):
    n_e = MOE_GROUPS * MOE_EPG
    w = jnp.concatenate([jnp.transpose(wer, (1, 0, 2)).reshape(D_MODEL, n_e), wgr], axis=1)
    b = jnp.concatenate([ber.reshape(n_e), bgr])
    w = jnp.pad(w, ((0, 0), (0, ROUTE_W - w.shape[1])))
    b = jnp.pad(b, (0, ROUTE_W - b.shape[0]))
    return w, b[None, :]


def _mixer(kind, jl, x, segs, p, cos_t, sin_t, lg1, lb1, wr, br):
    D = x.shape[1]
    if kind == 0:
        nq = A_HEADS * HEAD_DIM
        nkv = A_KV_HEADS * HEAD_DIM
        w = p["a_wqkv"][jl]
        wq, wk, wv = w[:, :nq], w[:, nq:nq + nkv], w[:, nq + nkv:]
        dup = lambda t: jnp.repeat(t.reshape(D, A_KV_HEADS, 1, HEAD_DIM), 2, axis=2).reshape(D, 2 * nkv)
        w_ext = jnp.concatenate([wq, dup(wk), dup(wv)], axis=1).astype(BF16)
        q, kd, vd = _proj_rope(x, w_ext, cos_t, sin_t, segs, dil=1, n_rope=(nq + 2 * nkv) // LANES,
                               scale_cols=nq, scale=HEAD_DIM ** -0.5, out_cols=(nq, 2 * nkv, 2 * nkv),
                               out_dtypes=(BF16, BF16, BF16))
        (o,) = _band_attn(q, kd, vd, segs, dil=1, radius=A_RADIUS, q_per_k=2, sink=p["a_sink"][jl])
        return _out_ln([o], x, p["a_wo"][jl].astype(BF16), lg1, lb1, wr, br, mode="plain")
    if kind == 1:
        gate, u_pre = _proj(x, p["b_win"][jl].astype(BF16), (RG_WIDTH, RG_WIDTH), (BF16, F32))
        w_band = _rglru_band_weights(p["b_wa"][jl], p["b_wi"][jl])
        hf, hb = _rglru_scan(u_pre, w_band, p["b_conv_w"][jl], p["b_conv_b"][jl][None, :], p["b_ba"][jl],
                             p["b_bi"][jl], p["b_lam"][jl], segs)
        return _out_ln([gate, hf, hb], x, p["b_wo"][jl].astype(BF16), lg1, lb1, wr, br, mode="rglru")
    if kind == 2:
        wa1 = jnp.concatenate([p["c_wa1"][jl][0], p["c_wa1"][jl][1]], axis=1)
        wa1 = jnp.pad(wa1, ((0, 0), (0, LANES - wa1.shape[1])))
        w_all = jnp.concatenate([p["c_wqkvg"][jl], wa1], axis=1).astype(BF16)
        q, k, v, g, lr = _proj(x, w_all, (GLA_DK, GLA_DK, GLA_DV, GLA_DV, LANES), (F32, F32, BF16, BF16, F32))
        wa2 = jnp.zeros((LANES, 2 * GLA_DK), F32)
        wa2 = wa2.at[:GLA_RANK, :GLA_DK].set(p["c_wa2"][jl][0])
        wa2 = wa2.at[GLA_RANK:2 * GLA_RANK, GLA_DK:].set(p["c_wa2"][jl][1])
        of, ob = _gla(q, k, v, lr, wa2.astype(BF16), p["c_ba"][jl].reshape(1, 2 * GLA_DK), segs)
        return _out_ln([of, ob, g], x, p["c_wo"][jl].astype(BF16), lg1, lb1, wr, br, mode="gla",
                       extra=p["c_norm_g"][jl][None, :])
    os_, ls_ = [], []
    for gi, (window, dil) in enumerate(DIL_GROUPS):
        w_g = p["d_wqkv"][jl][:, gi * 3 * DIL_WIDTH:(gi + 1) * 3 * DIL_WIDTH].astype(BF16)
        q, k, v = _proj_rope(x, w_g, cos_t, sin_t, segs, dil=dil, n_rope=2 * DIL_WIDTH // LANES,
                             scale_cols=DIL_WIDTH, scale=HEAD_DIM ** -0.5,
                             out_cols=(DIL_WIDTH,) * 3, out_dtypes=(BF16,) * 3)
        o, lse = _band_attn(q, k, v, segs, dil=dil, radius=window // (2 * dil), q_per_k=1, want_lse=True)
        os_.append(o)
        ls_.append(lse)
    return _out_ln(os_ + ls_, x, p["d_wo"][jl].astype(BF16), lg1, lb1, wr, br, mode="dil")


def kernel(x_prompt, x_sample, ln_g, ln_b, a_wqkv, a_sink, a_wo, b_win, b_conv_w, b_conv_b, b_wa, b_ba, b_wi,
           b_bi, b_lam, b_wo, c_wqkvg, c_wa1, c_wa2, c_ba, c_norm_g, c_wo, d_wqkv, d_wo, m_wgr, m_bgr, m_wer,
           m_ber, m_wg, m_wu, m_wd):
    p = dict(a_wqkv=a_wqkv, a_sink=a_sink, a_wo=a_wo, b_win=b_win, b_conv_w=b_conv_w, b_conv_b=b_conv_b,
             b_wa=b_wa, b_ba=b_ba, b_wi=b_wi, b_bi=b_bi, b_lam=b_lam, b_wo=b_wo, c_wqkvg=c_wqkvg, c_wa1=c_wa1,
             c_wa2=c_wa2, c_ba=c_ba, c_norm_g=c_norm_g, c_wo=c_wo, d_wqkv=d_wqkv, d_wo=d_wo)
    n_p, s_p, D = x_prompt.shape
    n_s, s_s, _ = x_sample.shape
    segs = ((n_p, s_p), (n_s, s_s))
    assert (n_p * s_p) % s_s == 0
    x = jnp.concatenate([x_prompt.reshape(n_p * s_p, D), x_sample.reshape(n_s * s_s, D)], axis=0)
    cos_t, sin_t = _rope_tables(max(s_p, s_s))
    wg_b, wu_b, wd_b = m_wg.astype(BF16), m_wu.astype(BF16), m_wd.astype(BF16)

    for layer in range(DEPTH):
        lg1, lb1 = ln_g[layer, 0][None, :], ln_b[layer, 0][None, :]
        lg2, lb2 = ln_g[layer, 1][None, :], ln_b[layer, 1][None, :]
        wr, br = _router_weights(m_wgr[layer], m_bgr[layer], m_wer[layer], m_ber[layer])
        x1, comb = _mixer(layer % 4, layer // 4, x, segs, p, cos_t, sin_t, lg1, lb1, wr, br)
        x = _moe(x1, comb, wg_b[layer], wu_b[layer], wd_b[layer], lg2, lb2)

    t_p = n_p * s_p
    return x[:t_p].reshape(n_p, s_p, D), x[t_p:].reshape(n_s, s_s, D)
```

```python
import functools

import jax
import jax.numpy as jnp
from jax import lax
from jax.experimental import pallas as pl
from jax.experimental.pallas import tpu as pltpu

F32 = jnp.float32
BF16 = jnp.bfloat16

D_MODEL = 1024
HEAD_DIM = 64
ROPE_THETA = 10000.0
A_HEADS = 16
A_KV_HEADS = 4
A_RADIUS = 128
RG_WIDTH = 1408
RG_BLOCKS = 16
RG_BW = RG_WIDTH // RG_BLOCKS
RG_CONV = 4
RG_C = 8.0
GLA_HEADS = 4
GLA_DK = 512
GLA_DV = 1024
GLA_DKH = 128
GLA_DVH = 256
GLA_RANK = 16
GLA_TAU = 16.0
GLA_CHUNK = 64
DIL_GROUPS = ((128, 1), (512, 4), (2048, 16))
N_DIL = 3
DIL_HEADS = 8
DIL_WIDTH = 512
MOE_GROUPS = 4
MOE_EPG = 8
MOE_FF = 256
DEPTH = 4
ALPHA = (2 * DEPTH) ** 0.25
LN_EPS = 1e-5
NEG_INF = -1e30

LANES = 128
SUBLANES = 8
VMEM_LIMIT = 52 * 1024 * 1024
ATT_Q = 128
TOK_TILE = 512
SCAN_TILE = 256
RG_NT = RG_WIDTH // LANES
RG_BAND = 3 * LANES
ROUTER_ROWS = 40
ROUTE_ROWS = 16
MOE_TILE = 256


def _cparams(sem):
    return pltpu.CompilerParams(dimension_semantics=sem, vmem_limit_bytes=VMEM_LIMIT)


def _seq_pos(i, rows, segs, dil=1):
    (n_p, s_p), (_, s_s) = segs
    p_tiles = (n_p * s_p) // rows
    in_p = i < p_tiles
    tps = jnp.where(in_p, (s_p // dil) // rows, (s_s // dil) // rows)
    j = lax.rem(i, tps)
    return in_p, tps, j


def _layer_norm(y, g, b):
    mu = jnp.mean(y, axis=-1, keepdims=True)
    yc = y - mu
    var = jnp.mean(yc * yc, axis=-1, keepdims=True)
    return yc * lax.rsqrt(var + LN_EPS) * g + b


def _rope_slab(t, cos, sin_signed):
    lane = lax.broadcasted_iota(jnp.int32, t.shape, 1)
    first_half = (lane % HEAD_DIM) < (HEAD_DIM // 2)
    partner = jnp.where(first_half, pltpu.roll(t, LANES - HEAD_DIM // 2, 1), pltpu.roll(t, HEAD_DIM // 2, 1))
    return t * cos + partner * sin_signed


def _proj_rope_kernel(x_ref, w_ref, cos_ref, sin_ref, *out_refs, n_rope, scale_cols, scale, out_cols):
    xb = x_ref[...].astype(BF16)
    cos = cos_ref[...]
    sin = sin_ref[...]
    col = 0
    for o_ref, width in zip(out_refs, out_cols):
        for c in range(width // LANES):
            slab = (col + c * LANES) // LANES
            y = jnp.dot(xb, w_ref[:, slab * LANES:(slab + 1) * LANES], preferred_element_type=F32)
            if slab < n_rope:
                y = _rope_slab(y, cos, sin)
            if slab * LANES < scale_cols:
                y = y * scale
            o_ref[:, c * LANES:(c + 1) * LANES] = y.astype(o_ref.dtype)
        col += width


def _proj_rope(x, w, cos_tab, sin_tab, segs, *, dil, n_rope, scale_cols, scale, out_cols, out_dtypes):
    T, D = x.shape
    (n_p, s_p), (n_s, s_s) = segs
    l_p, l_s = s_p // dil, s_s // dil
    tj = min(TOK_TILE, l_s, l_p)
    assert l_p % tj == 0 and l_s % tj == 0 and (n_p * s_p) % (dil * tj) == 0
    xv = x.reshape(T // dil, dil * D)
    cosv = cos_tab.reshape(cos_tab.shape[0] // dil, dil * LANES)
    sinv = sin_tab.reshape(sin_tab.shape[0] // dil, dil * LANES)
    p_tiles = (n_p * l_p) // tj

    def decode(jt):
        in_p = jt < p_tiles
        lt = jnp.where(in_p, l_p // tj, l_s // tj)
        jt_loc = jnp.where(in_p, jt, jt - p_tiles)
        b = jt_loc // lt
        j0 = lax.rem(jt_loc, lt)
        return in_p, lt, b, j0

    def out_map(jt, r):
        in_p, lt, b, j0 = decode(jt)
        base = jnp.where(in_p, 0, (n_p * s_p) // tj)
        return (base + b * (lt * dil) + r * lt + j0, 0)

    def tab_map(jt, r):
        _, _, _, j0 = decode(jt)
        return (j0, r)

    kern = functools.partial(_proj_rope_kernel, n_rope=n_rope, scale_cols=scale_cols, scale=scale,
                             out_cols=out_cols)
    n_out = w.shape[1]
    return pl.pallas_call(
        kern,
        grid=(T // dil // tj, dil),
        in_specs=[
            pl.BlockSpec((tj, D), lambda jt, r: (jt, r)),
            pl.BlockSpec((D, n_out), lambda jt, r: (0, 0)),
            pl.BlockSpec((tj, LANES), tab_map),
            pl.BlockSpec((tj, LANES), tab_map),
        ],
        out_specs=[pl.BlockSpec((tj, c), out_map) for c in out_cols],
        out_shape=[jax.ShapeDtypeStruct((T, c), dt) for c, dt in zip(out_cols, out_dtypes)],
        compiler_params=_cparams(("parallel", "parallel")),
    )(xv, w, cosv, sinv)


def _proj_kernel(x_ref, w_ref, *out_refs, out_cols):
    xb = x_ref[...].astype(BF16)
    col = 0
    for o_ref, width in zip(out_refs, out_cols):
        for c in range(width // LANES):
            y = jnp.dot(xb, w_ref[:, col:col + LANES], preferred_element_type=F32)
            o_ref[:, c * LANES:(c + 1) * LANES] = y.astype(o_ref.dtype)
            col += LANES


def _proj(x, w, out_cols, out_dtypes):
    T, D = x.shape
    tm = TOK_TILE
    n_out = w.shape[1]
    assert sum(out_cols) == n_out and T % tm == 0
    return pl.pallas_call(
        functools.partial(_proj_kernel, out_cols=out_cols),
        grid=(T // tm,),
        in_specs=[pl.BlockSpec((tm, D), lambda i: (i, 0)), pl.BlockSpec((D, n_out), lambda i: (0, 0))],
        out_specs=[pl.BlockSpec((tm, c), lambda i: (i, 0)) for c in out_cols],
        out_shape=[jax.ShapeDtypeStruct((T, c), dt) for c, dt in zip(out_cols, out_dtypes)],
        compiler_params=_cparams(("parallel",)),
    )(x, w)


def _band_attn_kernel(*refs, segs, dil, radius, q_per_k, has_sink, want_lse):
    it = iter(refs)
    sink_ref = next(it) if has_sink else None
    q_ref = next(it)
    kp_ref, kc_ref, kn_ref = next(it), next(it), next(it)
    vp_ref, vc_ref, vn_ref = next(it), next(it), next(it)
    o_ref = next(it)
    lse_ref = next(it) if want_lse else None

    i = pl.program_id(0)
    _, tps, j = _seq_pos(i, ATT_Q, segs, dil)
    prev_ok = j > 0
    next_ok = j < tps - 1
    W = 2 * radius + ATT_Q
    row = lax.broadcasted_iota(jnp.int32, (ATT_Q, W), 0)
    col = lax.broadcasted_iota(jnp.int32, (ATT_Q, W), 1)
    rel = col - radius - row
    ok = (jnp.abs(rel) <= radius) & ((col >= radius) | prev_ok) & ((col < radius + ATT_Q) | next_ok)
    bias = jnp.where(ok, 0.0, NEG_INF).astype(F32)

    k_all = jnp.concatenate([kp_ref[...], kc_ref[...], kn_ref[...]], axis=0)
    v_all = jnp.concatenate([vp_ref[...], vc_ref[...], vn_ref[...]], axis=0)
    nk = k_all.shape[1] // LANES
    lane_lo = lax.broadcasted_iota(jnp.int32, (1, LANES), 1) < HEAD_DIM
    bias_m = jnp.concatenate([bias] * q_per_k, axis=0) if q_per_k > 1 else bias
    M = ATT_Q * q_per_k
    zero = jnp.zeros((), BF16)

    for kc in range(nk):
        ksl = k_all[:, kc * LANES:(kc + 1) * LANES]
        vsl = v_all[:, kc * LANES:(kc + 1) * LANES]
        slabs = [kc * q_per_k + t for t in range(q_per_k)]
        qm = jnp.concatenate([q_ref[:, m * LANES:(m + 1) * LANES] for m in slabs], axis=0) \
            if q_per_k > 1 else q_ref[:, kc * LANES:(kc + 1) * LANES]
        o_acc = jnp.zeros((M, LANES), F32)
        lse_acc = jnp.zeros((M, LANES), F32)
        for half in range(2):
            keep = lane_lo if half == 0 else jnp.logical_not(lane_lo)
            kx = jnp.where(keep, ksl, zero)
            vx = jnp.where(keep, vsl, zero)
            s = lax.dot_general(qm, kx, (((1,), (1,)), ((), ())), preferred_element_type=F32) + bias_m
            mx = jnp.max(s, axis=1, keepdims=True)
            if has_sink:
                sk = jnp.concatenate(
                    [jnp.full((ATT_Q, 1), sink_ref[2 * m + half], F32) for m in slabs], axis=0)
                mx = jnp.maximum(mx, sk)
            p = jnp.exp(s - mx)
            l = jnp.sum(p, axis=1, keepdims=True)
            if has_sink:
                l = l + jnp.exp(sk - mx)
            pv = jnp.dot(p.astype(BF16), vx, preferred_element_type=F32)
            o_acc = o_acc + pv / l
            if want_lse:
                lse_acc = jnp.where(keep, mx + jnp.log(l), lse_acc)
        for t, m in enumerate(slabs):
            o_ref[:, m * LANES:(m + 1) * LANES] = o_acc[t * ATT_Q:(t + 1) * ATT_Q].astype(o_ref.dtype)
            if want_lse:
                lse_ref[:, m * LANES:(m + 1) * LANES] = lse_acc[t * ATT_Q:(t + 1) * ATT_Q]


def _band_attn(q, k, v, segs, *, dil, radius, q_per_k, sink=None, want_lse=False):
    T, wq = q.shape
    wk = k.shape[1]
    (n_p, s_p), (n_s, s_s) = segs
    assert ATT_Q % radius == 0
    hb = ATT_Q // radius
    n_halo = T // radius
    nt = T // ATT_Q
    l_p, l_s = s_p // dil, s_s // dil
    assert l_p % ATT_Q == 0 and l_s % ATT_Q == 0
    p_tiles = (n_p * s_p) // ATT_Q

    def out_map(i):
        in_p = i < p_tiles
        nb = jnp.where(in_p, l_p // ATT_Q, l_s // ATT_Q)
        i_loc = jnp.where(in_p, i, i - p_tiles)
        n = i_loc // nb
        jb = lax.rem(i_loc, nb)
        b = n // dil
        r = lax.rem(n, dil)
        base = jnp.where(in_p, 0, p_tiles // dil)
        return (base + b * nb + jb, r)

    in_specs = []
    args = []
    if sink is not None:
        in_specs.append(pl.BlockSpec(memory_space=pltpu.SMEM))
        args.append(sink)
    in_specs.append(pl.BlockSpec((ATT_Q, wq), lambda i: (i, 0)))
    args.append(q)
    for arr in (k, v):
        in_specs += [
            pl.BlockSpec((radius, wk), lambda i: (jnp.maximum(i * hb - 1, 0), 0)),
            pl.BlockSpec((ATT_Q, wk), lambda i: (i, 0)),
            pl.BlockSpec((radius, wk), lambda i: (jnp.minimum((i + 1) * hb, n_halo - 1), 0)),
        ]
        args += [arr, arr, arr]
    out_shape = [jax.ShapeDtypeStruct((T // dil, dil * wq), BF16)]
    out_specs = [pl.BlockSpec((ATT_Q, wq), out_map)]
    if want_lse:
        out_shape.append(jax.ShapeDtypeStruct((T // dil, dil * wq), F32))
        out_specs.append(pl.BlockSpec((ATT_Q, wq), out_map))
    kern = functools.partial(_band_attn_kernel, segs=segs, dil=dil, radius=radius, q_per_k=q_per_k,
                             has_sink=sink is not None, want_lse=want_lse)
    outs = pl.pallas_call(
        kern, grid=(nt,), in_specs=in_specs, out_specs=out_specs, out_shape=out_shape,
        compiler_params=_cparams(("parallel",)),
    )(*args)
    return [o.reshape(T, wq) for o in outs]


def _split3(v):
    hi = v.astype(BF16)
    r1 = v - hi.astype(F32)
    mid = r1.astype(BF16)
    lo = (r1 - mid.astype(F32)).astype(BF16)
    return hi, mid, lo


def _route(x1, wr_ref, br_ref, sel_ref):
    hi, mid, _ = _split3(x1)
    xs = jnp.concatenate([hi, hi, mid], axis=1)
    lt = lax.dot_general(wr_ref[...], xs, (((1,), (1,)), ((), ())), preferred_element_type=F32) + br_ref[...]
    n_tok = x1.shape[0]
    row = lax.broadcasted_iota(jnp.int32, (MOE_EPG, n_tok), 0)
    big = jnp.int32(MOE_EPG)

    def first_argmax(v):
        m = jnp.max(v, axis=0, keepdims=True)
        return m, jnp.min(jnp.where(v == m, row, big), axis=0, keepdims=True)

    n_e = MOE_GROUPS * MOE_EPG
    gl = lt[n_e:n_e + MOE_EPG]
    gmax, gidx = first_argmax(gl)
    g_w = 1.0 / jnp.sum(jnp.exp(gl - gmax), axis=0, keepdims=True)
    el = lt[(MOE_GROUPS - 1) * MOE_EPG:n_e]
    for g in range(MOE_GROUPS - 2, -1, -1):
        el = jnp.where(gidx == g, lt[g * MOE_EPG:(g + 1) * MOE_EPG], el)
    v1, i1 = first_argmax(el)
    el2 = jnp.where(row == i1, -jnp.inf, el)
    v2, i2 = first_argmax(el2)
    ex = jnp.exp(v2 - v1)
    w1 = 1.0 / (1.0 + ex)
    w2 = ex * w1
    first_lo = i1 < i2
    e_lo = jnp.minimum(i1, i2).astype(F32)
    e_hi = jnp.maximum(i1, i2).astype(F32)
    c_lo = g_w * jnp.where(first_lo, w1, w2)
    c_hi = g_w * jnp.where(first_lo, w2, w1)
    pad = jnp.zeros((ROUTE_ROWS - 5, n_tok), F32)
    rt = jnp.concatenate([gidx.astype(F32), e_lo, e_hi, c_lo, c_hi, pad], axis=0)
    pieces = jnp.concatenate(_split3(rt), axis=0)
    slab = lax.dot_general(pieces, sel_ref[...], (((0,), (0,)), ((), ())), preferred_element_type=F32)
    return rt, slab


def _gelu_tanh(x):
    return 0.5 * x * (1.0 + jnp.tanh(0.7978845608028654 * (x + 0.044715 * x * x * x)))


def _silu(x):
    return x / (1.0 + jnp.exp(-x))


def _out_ln_kernel(*refs, mode, n_h):
    h_refs = refs[:n_h]
    extra_ref = refs[n_h] if mode == "gla" else None
    base = n_h + (1 if mode == "gla" else 0)
    x_ref, wo_ref, g_ref, b_ref, wr_ref, br_ref, sel_ref, x1_ref, rt_ref = refs[base:base + 9]

    if mode == "plain":
        hb = h_refs[0][...]
    elif mode == "rglru":
        gate, hf, hbw = h_refs
        hb = (_gelu_tanh(gate[...].astype(F32)) * (hf[...].astype(F32) + hbw[...].astype(F32))).astype(BF16)
    elif mode == "gla":
        of, ob, gg = h_refs
        o = of[...] + ob[...]
        parts = []
        for h in range(GLA_HEADS):
            oh = o[:, h * GLA_DVH:(h + 1) * GLA_DVH]
            ms = jnp.mean(oh * oh, axis=-1, keepdims=True)
            parts.append(oh * lax.rsqrt(ms + LN_EPS) * extra_ref[...])
        o = jnp.concatenate(parts, axis=1)
        hb = (o * _silu(gg[...].astype(F32))).astype(BF16)
    else:
        os_ = h_refs[:N_DIL]
        ls_ = [r[...] for r in h_refs[N_DIL:]]
        mx = functools.reduce(jnp.maximum, ls_)
        es = [jnp.exp(l - mx) for l in ls_]
        den = functools.reduce(jnp.add, es)
        o = functools.reduce(jnp.add, [(e / den) * r[...].astype(F32) for e, r in zip(es, os_)])
        hb = o.astype(BF16)

    acc = jnp.dot(hb, wo_ref[...], preferred_element_type=F32)
    x1 = _layer_norm(ALPHA * x_ref[...] + acc, g_ref[...], b_ref[...])
    rt, slab = _route(x1, wr_ref, br_ref, sel_ref)
    x1_ref[:, :D_MODEL] = x1
    x1_ref[:, D_MODEL:] = slab
    rt_ref[...] = rt


def _out_ln(hs, x, wo, ln_g, ln_b, wr, br, *, mode, extra=None):
    T, D = x.shape
    tm = TOK_TILE
    sel = jnp.tile(jnp.eye(ROUTE_ROWS, LANES, dtype=BF16), (3, 1))
    in_specs = [pl.BlockSpec((tm, h.shape[1]), lambda i: (i, 0)) for h in hs]
    args = list(hs)
    if mode == "gla":
        in_specs.append(pl.BlockSpec((1, extra.shape[1]), lambda i: (0, 0)))
        args.append(extra)
    in_specs += [
        pl.BlockSpec((tm, D), lambda i: (i, 0)),
        pl.BlockSpec(wo.shape, lambda i: (0, 0)),
        pl.BlockSpec((1, D), lambda i: (0, 0)),
        pl.BlockSpec((1, D), lambda i: (0, 0)),
        pl.BlockSpec(wr.shape, lambda i: (0, 0)),
        pl.BlockSpec(br.shape, lambda i: (0, 0)),
        pl.BlockSpec(sel.shape, lambda i: (0, 0)),
    ]
    args += [x, wo, ln_g, ln_b, wr, br, sel]
    return pl.pallas_call(
        functools.partial(_out_ln_kernel, mode=mode, n_h=len(hs)),
        grid=(T // tm,),
        in_specs=in_specs,
        out_specs=[pl.BlockSpec((tm, D + LANES), lambda i: (i, 0)),
                   pl.BlockSpec((ROUTE_ROWS, tm), lambda i: (0, i))],
        out_shape=[jax.ShapeDtypeStruct((T, D + LANES), F32),
                   jax.ShapeDtypeStruct((ROUTE_ROWS, T), F32)],
        compiler_params=_cparams(("parallel",)),
    )(*args)


def _moe_kernel(tg_ref, tlo_ref, thi_ref, nv_ref, idx_ref, x_hbm, wgl_ref, wul_ref, wdl_ref, wgh_ref, wuh_ref,
                wdh_ref, g_ref, b_ref, out_hbm, xbuf, obuf, gsem, ssem, *, nt):
    i = pl.program_id(0)
    tm = xbuf.shape[1]
    slot = lax.rem(i, 2)
    used = nv_ref[i] > 0
    nxt = jnp.minimum(i + 1, nt - 1)
    next_used = jnp.logical_and(i + 1 < nt, nv_ref[nxt] > 0)

    def gather(tile, sl, start):
        def body(r, c):
            tok = idx_ref[tile * tm + r]
            cp = pltpu.make_async_copy(x_hbm.at[pl.ds(tok, 1)], xbuf.at[sl, pl.ds(r, 1)], gsem.at[sl])
            if start:
                cp.start()
            else:
                cp.wait()
            return c
        lax.fori_loop(0, tm, body, 0, unroll=8)

    def scatter(tile, sl, start):
        n = nv_ref[tile]

        def body(r, c):
            @pl.when(r < n)
            def _():
                tok = idx_ref[tile * tm + r]
                cp = pltpu.make_async_copy(obuf.at[sl, pl.ds(r, 1)], out_hbm.at[pl.ds(tok, 1)], ssem.at[sl])
                if start:
                    cp.start()
                else:
                    cp.wait()
            return c
        lax.fori_loop(0, tm, body, 0, unroll=8)

    @pl.when(jnp.logical_and(i == 0, used))
    def _():
        gather(0, 0, True)

    @pl.when(used)
    def _():
        gather(i, slot, False)

        @pl.when(next_used)
        def _():
            gather(i + 1, 1 - slot, True)

        @pl.when(i >= 2)
        def _():
            scatter(i - 2, slot, False)

        xe = xbuf[slot]
        x = xe[:, :D_MODEL]
        xb = x.astype(BF16)
        acc = jnp.zeros((tm, D_MODEL), F32)
        for wg_ref, wu_ref, wd_ref, lane in ((wgl_ref, wul_ref, wdl_ref, 3), (wgh_ref, wuh_ref, wdh_ref, 4)):
            hg = jnp.dot(xb, wg_ref[0, 0], preferred_element_type=F32)
            hu = jnp.dot(xb, wu_ref[0, 0], preferred_element_type=F32)
            h = _silu(hg) * hu * xe[:, D_MODEL + lane:D_MODEL + lane + 1]
            acc = acc + jnp.dot(h.astype(BF16), wd_ref[0, 0], preferred_element_type=F32)
        obuf[slot] = _layer_norm(ALPHA * x + acc, g_ref[...], b_ref[...])
        scatter(i, slot, True)

        @pl.when(jnp.logical_not(next_used))
        def _():
            @pl.when(i >= 1)
            def _():
                scatter(i - 1, 1 - slot, False)
            scatter(i, slot, False)


_PAIR_LO = tuple(a for a in range(MOE_EPG) for b in range(a + 1, MOE_EPG))
_PAIR_HI = tuple(b for a in range(MOE_EPG) for b in range(a + 1, MOE_EPG))
N_PAIRS = len(_PAIR_LO)
N_CLASSES = MOE_GROUPS * N_PAIRS


def _moe_schedule(rt, tm, nt):
    T = rt.shape[1]
    i32 = jnp.int32
    g, lo, hi = rt[0].astype(i32), rt[1].astype(i32), rt[2].astype(i32)
    cls = g * N_PAIRS + lo * (2 * MOE_EPG - 1 - lo) // 2 + (hi - lo - 1)
    cls_sorted, order = lax.sort((cls, jnp.arange(T, dtype=i32)), num_keys=1, is_stable=True)
    starts = jnp.searchsorted(cls_sorted, jnp.arange(N_CLASSES + 1, dtype=i32), side="left").astype(i32)
    counts = starts[1:] - starts[:-1]
    tiles_per = (counts + tm - 1) // tm
    cum = jnp.cumsum(tiles_per)
    n_used = cum[-1]
    t = jnp.arange(nt, dtype=i32)
    tc = jnp.minimum(t, n_used - 1)
    tcls = jnp.searchsorted(cum, tc, side="right").astype(i32)
    k = tc - (cum[tcls] - tiles_per[tcls])
    nvalid = jnp.where(t < n_used, jnp.clip(counts[tcls] - k * tm, 0, tm), 0).astype(i32)
    r = jnp.arange(tm, dtype=i32)
    pos = starts[tcls][:, None] + k[:, None] * tm + jnp.minimum(r[None, :], jnp.maximum(nvalid, 1)[:, None] - 1)
    idx = order[jnp.clip(pos, 0, T - 1)].reshape(-1)
    pr = tcls % N_PAIRS
    tile_lo = jnp.asarray(_PAIR_LO, i32)[pr]
    tile_hi = jnp.asarray(_PAIR_HI, i32)[pr]
    return (tcls // N_PAIRS).astype(i32), tile_lo, tile_hi, nvalid, idx.astype(i32)


def _moe(x1e, rt, wg, wu, wd, ln_g, ln_b):
    T, XE = x1e.shape
    D = D_MODEL
    tm = MOE_TILE
    nt = T // tm + N_CLASSES
    sched = _moe_schedule(rt, tm, nt)
    lo_map = lambda i, tg, tlo, thi, nv, idx: (tg[i], tlo[i], 0, 0)
    hi_map = lambda i, tg, tlo, thi, nv, idx: (tg[i], thi[i], 0, 0)
    const = lambda i, tg, tlo, thi, nv, idx: (0, 0)
    grid_spec = pltpu.PrefetchScalarGridSpec(
        num_scalar_prefetch=5,
        grid=(nt,),
        in_specs=[
            pl.BlockSpec(memory_space=pl.ANY),
            pl.BlockSpec((1, 1, D, MOE_FF), lo_map),
            pl.BlockSpec((1, 1, D, MOE_FF), lo_map),
            pl.BlockSpec((1, 1, MOE_FF, D), lo_map),
            pl.BlockSpec((1, 1, D, MOE_FF), hi_map),
            pl.BlockSpec((1, 1, D, MOE_FF), hi_map),
            pl.BlockSpec((1, 1, MOE_FF, D), hi_map),
            pl.BlockSpec((1, D), const),
            pl.BlockSpec((1, D), const),
        ],
        out_specs=pl.BlockSpec(memory_space=pl.ANY),
        scratch_shapes=[pltpu.VMEM((2, tm, XE), F32), pltpu.VMEM((2, tm, D), F32),
                        pltpu.SemaphoreType.DMA((2,)), pltpu.SemaphoreType.DMA((2,))],
    )
    return pl.pallas_call(
        functools.partial(_moe_kernel, nt=nt),
        grid_spec=grid_spec,
        out_shape=jax.ShapeDtypeStruct((T, D), F32),
        compiler_params=_cparams(("arbitrary",)),
    )(*sched, x1e, wg, wu, wd, wg, wu, wd, ln_g, ln_b)


def _rglru_stream(z, u_ref, up_ref, un_ref, w_ref, cw_ref, cb_ref, ba_ref, bi_ref, lam_ref,
                  h_out_ref, a_s, b_s, h_s, carry_ref, prev_ok, next_ok, reset):
    tt = u_ref.shape[0]
    u_mid = u_ref[...]
    up = jnp.where(prev_ok, up_ref[...], 0.0)
    un = jnp.where(next_ok, un_ref[...], 0.0)
    ext = jnp.concatenate([up, u_mid, un], axis=0)
    left = RG_CONV // 2
    u = cb_ref[...]
    for kk in range(RG_CONV):
        off = SUBLANES - left + kk
        u = u + cw_ref[kk:kk + 1, :] * ext[off:off + tt]
    ub = u.astype(BF16)
    sp = jnp.maximum(-lam_ref[...], 0.0) + jnp.log(1.0 + jnp.exp(-jnp.abs(lam_ref[...])))
    for n in range(RG_NT):
        s0 = min(max(n - 1, 0), RG_NT - 3) * LANES
        zz = jnp.dot(ub[:, s0:s0 + RG_BAND], w_ref[n], preferred_element_type=F32)
        sl = slice(n * LANES, (n + 1) * LANES)
        r = jax.nn.sigmoid(zz[:, :LANES] + ba_ref[:, sl])
        ig = jax.nn.sigmoid(zz[:, LANES:] + bi_ref[:, sl])
        log_a = -RG_C * r * sp[:, sl]
        a = jnp.exp(log_a)
        a_s[:, sl] = a
        b_s[:, sl] = jnp.sqrt(-jnp.tanh(log_a) * (a * a + 1.0)) * ig * u[:, sl]

    @pl.when(reset)
    def _():
        carry_ref[...] = jnp.zeros_like(carry_ref)

    n_grp = tt // SUBLANES

    def body(gi, h):
        g = gi if z == 0 else n_grp - 1 - gi
        base = pl.multiple_of(g * SUBLANES, SUBLANES)
        a8 = a_s[pl.ds(base, SUBLANES), :]
        b8 = b_s[pl.ds(base, SUBLANES), :]
        rows = [None] * SUBLANES
        order = range(SUBLANES) if z == 0 else range(SUBLANES - 1, -1, -1)
        for r_ in order:
            h = a8[r_:r_ + 1] * h + b8[r_:r_ + 1]
            rows[r_] = h
        h_s[pl.ds(base, SUBLANES), :] = jnp.concatenate(rows, axis=0)
        return h

    h_last = lax.fori_loop(0, n_grp, body, carry_ref[...])
    carry_ref[...] = h_last
    h_out_ref[...] = h_s[...].astype(h_out_ref.dtype)


def _rglru_scan_kernel(uf_ref, ufp_ref, ufn_ref, ubk_ref, ubp_ref, ubn_ref, wf_ref, wb_ref, cw_ref, cb_ref,
                       ba_ref, bi_ref, lam_ref, hf_ref, hb_ref, a_s, b_s, h_s, cf_ref, cbk_ref, *, segs, nt):
    i = pl.program_id(0)
    tt = uf_ref.shape[0]
    _, tps, j = _seq_pos(i, tt, segs)
    _rglru_stream(0, uf_ref, ufp_ref, ufn_ref, wf_ref, cw_ref, cb_ref, ba_ref.at[0:1], bi_ref.at[0:1],
                  lam_ref.at[0:1], hf_ref, a_s, b_s, h_s, cf_ref, j > 0, j < tps - 1, j == 0)
    ib = nt - 1 - i
    _, tps_b, jb = _seq_pos(ib, tt, segs)
    _rglru_stream(1, ubk_ref, ubp_ref, ubn_ref, wb_ref, cw_ref, cb_ref, ba_ref.at[1:2], bi_ref.at[1:2],
                  lam_ref.at[1:2], hb_ref, a_s, b_s, h_s, cbk_ref, jb > 0, jb < tps_b - 1, jb == tps_b - 1)


def _rglru_scan(u_pre, w_band, conv_w, conv_b, ba, bi, lam, segs):
    T, W = u_pre.shape
    tt = SCAN_TILE
    nt = T // tt
    hpt = tt // SUBLANES
    n_h = T // SUBLANES
    fwd = lambda i: (i, 0)
    bwd = lambda i: (nt - 1 - i, 0)

    def halo_specs(idx):
        return [
            pl.BlockSpec((tt, W), lambda i: (idx(i), 0)),
            pl.BlockSpec((SUBLANES, W), lambda i: (jnp.maximum(idx(i) * hpt - 1, 0), 0)),
            pl.BlockSpec((SUBLANES, W), lambda i: (jnp.minimum((idx(i) + 1) * hpt, n_h - 1), 0)),
        ]

    const2 = lambda i: (0, 0)
    const3 = lambda i: (0, 0, 0)
    in_specs = halo_specs(lambda i: i) + halo_specs(lambda i: nt - 1 - i) + [
        pl.BlockSpec(w_band.shape[1:], const3),
        pl.BlockSpec(w_band.shape[1:], const3),
        pl.BlockSpec(conv_w.shape, const2),
        pl.BlockSpec(conv_b.shape, const2),
        pl.BlockSpec(ba.shape, const2),
        pl.BlockSpec(bi.shape, const2),
        pl.BlockSpec(lam.shape, const2),
    ]
    return pl.pallas_call(
        functools.partial(_rglru_scan_kernel, segs=segs, nt=nt),
        grid=(nt,),
        in_specs=in_specs,
        out_specs=[pl.BlockSpec((tt, W), fwd), pl.BlockSpec((tt, W), bwd)],
        out_shape=[jax.ShapeDtypeStruct((T, W), BF16), jax.ShapeDtypeStruct((T, W), BF16)],
        scratch_shapes=[pltpu.VMEM((tt, W), F32), pltpu.VMEM((tt, W), F32), pltpu.VMEM((tt, W), F32),
                        pltpu.VMEM((1, W), F32), pltpu.VMEM((1, W), F32)],
        compiler_params=_cparams(("arbitrary",)),
    )(u_pre, u_pre, u_pre, u_pre, u_pre, u_pre, w_band[0], w_band[1], conv_w, conv_b, ba, bi, lam)


def _rglru_band_weights(wa, wi):
    def dense(w):
        eye = jnp.eye(RG_BLOCKS, dtype=w.dtype)
        return jnp.einsum("ncd,nm->ncmd", w, eye).reshape(RG_WIDTH, RG_WIDTH)

    out = []
    for z in range(2):
        da, di = dense(wa[z]), dense(wi[z])
        tiles = []
        for n in range(RG_NT):
            s0 = min(max(n - 1, 0), RG_NT - 3) * LANES
            sl = slice(n * LANES, (n + 1) * LANES)
            tiles.append(jnp.concatenate([da[s0:s0 + RG_BAND, sl], di[s0:s0 + RG_BAND, sl]], axis=1))
        out.append(jnp.stack(tiles))
    return jnp.stack(out).astype(BF16)


def _gla_stream(z, q_ref, k_ref, v_ref, lr_ref, wa2_ref, ba_ref, o_ref, st_ref, reset):
    tt = q_ref.shape[0]
    C = GLA_CHUNK

    @pl.when(reset)
    def _():
        st_ref[z] = jnp.zeros(st_ref.shape[1:], F32)

    zz = jnp.dot(lr_ref[...].astype(BF16), wa2_ref[:, z * GLA_DK:(z + 1) * GLA_DK],
                 preferred_element_type=F32) + ba_ref[:, z * GLA_DK:(z + 1) * GLA_DK]
    log_a = -(jnp.maximum(-zz, 0.0) + jnp.log(1.0 + jnp.exp(-jnp.abs(zz)))) / GLA_TAU
    ri = lax.broadcasted_iota(jnp.int32, (C, C), 0)
    ci = lax.broadcasted_iota(jnp.int32, (C, C), 1)
    causal = (ri >= ci) if z == 0 else (ri <= ci)
    tri = causal.astype(F32)
    mid = C // 2 if z == 0 else C - 1 - C // 2
    last = C - 1 if z == 0 else 0
    scale = GLA_DKH ** -0.5
    chunks = range(tt // C) if z == 0 else range(tt // C - 1, -1, -1)
    for c in chunks:
        rs = slice(c * C, (c + 1) * C)
        b = jnp.dot(tri, log_a[rs], preferred_element_type=F32, precision=lax.Precision.HIGHEST)
        b_mid = b[mid:mid + 1]
        b_last = b[last:last + 1]
        qc = q_ref[rs, :] * scale
        kc = k_ref[rs, :]
        qd = (qc * jnp.exp(b - b_mid)).astype(BF16)
        kd = (kc * jnp.exp(b_mid - b)).astype(BF16)
        ks = (kc * jnp.exp(b_last - b)).astype(BF16)
        qb = (qc * jnp.exp(b)).astype(BF16)
        dec = jnp.exp(b_last)
        for h in range(GLA_HEADS):
            ksl = slice(h * GLA_DKH, (h + 1) * GLA_DKH)
            vsl = slice(h * GLA_DVH, (h + 1) * GLA_DVH)
            vc = v_ref[rs, vsl]
            att = lax.dot_general(qd[:, ksl], kd[:, ksl], (((1,), (1,)), ((), ())), preferred_element_type=F32)
            att = jnp.where(causal, att, 0.0)
            o = jnp.dot(att.astype(BF16), vc, preferred_element_type=F32)
            st = st_ref[z, h]
            o = o + lax.dot_general(qb[:, ksl], st.astype(BF16), (((1,), (1,)), ((), ())),
                                    preferred_element_type=F32)
            o_ref[rs, vsl] = o
            upd = lax.dot_general(vc, ks[:, ksl], (((0,), (0,)), ((), ())), preferred_element_type=F32)
            st_ref[z, h] = st * dec[:, ksl] + upd


def _gla_kernel(qf, kf, vf, lf, qb, kb, vb, lb, wa2_ref, ba_ref, of_ref, ob_ref, st_ref, *, segs, nt):
    i = pl.program_id(0)
    tt = qf.shape[0]
    _, tps, j = _seq_pos(i, tt, segs)
    _gla_stream(0, qf, kf, vf, lf, wa2_ref, ba_ref, of_ref, st_ref, j == 0)
    ib = nt - 1 - i
    _, tps_b, jb = _seq_pos(ib, tt, segs)
    _gla_stream(1, qb, kb, vb, lb, wa2_ref, ba_ref, ob_ref, st_ref, jb == tps_b - 1)


def _gla(q, k, v, lr, wa2, ba, segs):
    T = q.shape[0]
    tt = SCAN_TILE
    nt = T // tt
    fwd = lambda i: (i, 0)
    bwd = lambda i: (nt - 1 - i, 0)
    in_specs = []
    for idx in (fwd, bwd):
        in_specs += [pl.BlockSpec((tt, GLA_DK), idx), pl.BlockSpec((tt, GLA_DK), idx),
                     pl.BlockSpec((tt, GLA_DV), idx), pl.BlockSpec((tt, LANES), idx)]
    in_specs += [pl.BlockSpec(wa2.shape, lambda i: (0, 0)), pl.BlockSpec(ba.shape, lambda i: (0, 0))]
    return pl.pallas_call(
        functools.partial(_gla_kernel, segs=segs, nt=nt),
        grid=(nt,),
        in_specs=in_specs,
        out_specs=[pl.BlockSpec((tt, GLA_DV), fwd), pl.BlockSpec((tt, GLA_DV), bwd)],
        out_shape=[jax.ShapeDtypeStruct((T, GLA_DV), F32), jax.ShapeDtypeStruct((T, GLA_DV), F32)],
        scratch_shapes=[pltpu.VMEM((2, GLA_HEADS, GLA_DVH, GLA_DKH), F32)],
        compiler_params=_cparams(("arbitrary",)),
    )(q, k, v, lr, q, k, v, lr, wa2, ba)


def _rope_tables(s_max):
    half = HEAD_DIM // 2
    inv = ROPE_THETA ** (-jnp.arange(half, dtype=F32) / half)
    ang = jnp.arange(s_max, dtype=F32)[:, None] * inv[None, :]
    cos, sin = jnp.cos(ang), jnp.sin(ang)
    cos_t = jnp.tile(jnp.concatenate([cos, cos], axis=1), (1, LANES // HEAD_DIM))
    sin_t = jnp.tile(jnp.concatenate([-sin, sin], axis=1), (1, LANES // HEAD_DIM))
    return cos_t, sin_t


def _router_weights(wgr, bgr, wer, ber):
    n_e = MOE_GROUPS * MOE_EPG
    w = jnp.concatenate([jnp.transpose(wer, (0, 2, 1)).reshape(n_e, D_MODEL), wgr.T], axis=0)
    b = jnp.concatenate([ber.reshape(n_e), bgr])
    n_pad = ROUTER_ROWS - w.shape[0]
    w = jnp.pad(w, ((0, n_pad), (0, 0)))
    b = jnp.concatenate([b, jnp.full((n_pad,), NEG_INF, F32)])
    w_hi = w.astype(BF16)
    w_mid = (w - w_hi.astype(F32)).astype(BF16)
    return jnp.concatenate([w_hi, w_mid, w_hi], axis=1), b[:, None]


def _mixer(kind, jl, x, segs, p, cos_t, sin_t, lg1, lb1, wr, br):
    D = x.shape[1]
    if kind == 0:
        nq = A_HEADS * HEAD_DIM
        nkv = A_KV_HEADS * HEAD_DIM
        w = p["a_wqkv"][jl]
        wq, wk, wv = w[:, :nq], w[:, nq:nq + nkv], w[:, nq + nkv:]
        dup = lambda t: jnp.repeat(t.reshape(D, A_KV_HEADS, 1, HEAD_DIM), 2, axis=2).reshape(D, 2 * nkv)
        w_ext = jnp.concatenate([wq, dup(wk), dup(wv)], axis=1).astype(BF16)
        q, kd, vd = _proj_rope(x, w_ext, cos_t, sin_t, segs, dil=1, n_rope=(nq + 2 * nkv) // LANES,
                               scale_cols=nq, scale=HEAD_DIM ** -0.5, out_cols=(nq, 2 * nkv, 2 * nkv),
                               out_dtypes=(BF16, BF16, BF16))
        (o,) = _band_attn(q, kd, vd, segs, dil=1, radius=A_RADIUS, q_per_k=2, sink=p["a_sink"][jl])
        return _out_ln([o], x, p["a_wo"][jl].astype(BF16), lg1, lb1, wr, br, mode="plain")
    if kind == 1:
        gate, u_pre = _proj(x, p["b_win"][jl].astype(BF16), (RG_WIDTH, RG_WIDTH), (BF16, F32))
        w_band = _rglru_band_weights(p["b_wa"][jl], p["b_wi"][jl])
        hf, hb = _rglru_scan(u_pre, w_band, p["b_conv_w"][jl], p["b_conv_b"][jl][None, :], p["b_ba"][jl],
                             p["b_bi"][jl], p["b_lam"][jl], segs)
        return _out_ln([gate, hf, hb], x, p["b_wo"][jl].astype(BF16), lg1, lb1, wr, br, mode="rglru")
    if kind == 2:
        wa1 = jnp.concatenate([p["c_wa1"][jl][0], p["c_wa1"][jl][1]], axis=1)
        wa1 = jnp.pad(wa1, ((0, 0), (0, LANES - wa1.shape[1])))
        w_all = jnp.concatenate([p["c_wqkvg"][jl], wa1], axis=1).astype(BF16)
        q, k, v, g, lr = _proj(x, w_all, (GLA_DK, GLA_DK, GLA_DV, GLA_DV, LANES), (F32, F32, BF16, BF16, F32))
        wa2 = jnp.zeros((LANES, 2 * GLA_DK), F32)
        wa2 = wa2.at[:GLA_RANK, :GLA_DK].set(p["c_wa2"][jl][0])
        wa2 = wa2.at[GLA_RANK:2 * GLA_RANK, GLA_DK:].set(p["c_wa2"][jl][1])
        of, ob = _gla(q, k, v, lr, wa2.astype(BF16), p["c_ba"][jl].reshape(1, 2 * GLA_DK), segs)
        return _out_ln([of, ob, g], x, p["c_wo"][jl].astype(BF16), lg1, lb1, wr, br, mode="gla",
                       extra=p["c_norm_g"][jl][None, :])
    os_, ls_ = [], []
    for gi, (window, dil) in enumerate(DIL_GROUPS):
        w_g = p["d_wqkv"][jl][:, gi * 3 * DIL_WIDTH:(gi + 1) * 3 * DIL_WIDTH].astype(BF16)
        q, k, v = _proj_rope(x, w_g, cos_t, sin_t, segs, dil=dil, n_rope=2 * DIL_WIDTH // LANES,
                             scale_cols=DIL_WIDTH, scale=HEAD_DIM ** -0.5,
                             out_cols=(DIL_WIDTH,) * 3, out_dtypes=(BF16,) * 3)
        o, lse = _band_attn(q, k, v, segs, dil=dil, radius=window // (2 * dil), q_per_k=1, want_lse=True)
        os_.append(o)
        ls_.append(lse)
    return _out_ln(os_ + ls_, x, p["d_wo"][jl].astype(BF16), lg1, lb1, wr, br, mode="dil")


def kernel(x_prompt, x_sample, ln_g, ln_b, a_wqkv, a_sink, a_wo, b_win, b_conv_w, b_conv_b, b_wa, b_ba, b_wi,
           b_bi, b_lam, b_wo, c_wqkvg, c_wa1, c_wa2, c_ba, c_norm_g, c_wo, d_wqkv, d_wo, m_wgr, m_bgr, m_wer,
           m_ber, m_wg, m_wu, m_wd):
    p = dict(a_wqkv=a_wqkv, a_sink=a_sink, a_wo=a_wo, b_win=b_win, b_conv_w=b_conv_w, b_conv_b=b_conv_b,
             b_wa=b_wa, b_ba=b_ba, b_wi=b_wi, b_bi=b_bi, b_lam=b_lam, b_wo=b_wo, c_wqkvg=c_wqkvg, c_wa1=c_wa1,
             c_wa2=c_wa2, c_ba=c_ba, c_norm_g=c_norm_g, c_wo=c_wo, d_wqkv=d_wqkv, d_wo=d_wo)
    n_p, s_p, D = x_prompt.shape
    n_s, s_s, _ = x_sample.shape
    segs = ((n_p, s_p), (n_s, s_s))
    assert (n_p * s_p) % s_s == 0
    x = jnp.concatenate([x_prompt.reshape(n_p * s_p, D), x_sample.reshape(n_s * s_s, D)], axis=0)
    cos_t, sin_t = _rope_tables(max(s_p, s_s))
    wg_b, wu_b, wd_b = m_wg.astype(BF16), m_wu.astype(BF16), m_wd.astype(BF16)

    for layer in range(DEPTH):
        lg1, lb1 = ln_g[layer, 0][None, :], ln_b[layer, 0][None, :]
        lg2, lb2 = ln_g[layer, 1][None, :], ln_b[layer, 1][None, :]
        wr, br = _router_weights(m_wgr[layer], m_bgr[layer], m_wer[layer], m_ber[layer])
        x1e, rt = _mixer(layer % 4, layer // 4, x, segs, p, cos_t, sin_t, lg1, lb1, wr, br)
        x = _moe(x1e, rt, wg_b[layer], wu_b[layer], wd_b[layer], lg2, lb2)

    t_p = n_p * s_p
    return x[:t_p].reshape(n_p, s_p, D), x[t_p:].reshape(n_s, s_s, D)
```

```python
import functools

import jax
import jax.numpy as jnp
from jax import lax
from jax.experimental import pallas as pl
from jax.experimental.pallas import tpu as pltpu

F32 = jnp.float32
BF16 = jnp.bfloat16

D_MODEL = 1024
HEAD_DIM = 64
ROPE_THETA = 10000.0
A_HEADS = 16
A_KV_HEADS = 4
A_RADIUS = 128
RG_WIDTH = 1408
RG_BLOCKS = 16
RG_BW = RG_WIDTH // RG_BLOCKS
RG_CONV = 4
RG_C = 8.0
GLA_HEADS = 4
GLA_DK = 512
GLA_DV = 1024
GLA_DKH = 128
GLA_DVH = 256
GLA_RANK = 16
GLA_TAU = 16.0
GLA_CHUNK = 64
DIL_GROUPS = ((128, 1), (512, 4), (2048, 16))
N_DIL = 3
DIL_HEADS = 8
DIL_WIDTH = 512
MOE_GROUPS = 4
MOE_EPG = 8
MOE_FF = 256
DEPTH = 4
ALPHA = (2 * DEPTH) ** 0.25
LN_EPS = 1e-5
NEG_INF = -1e30

LANES = 128
SUBLANES = 8
VMEM_LIMIT = 52 * 1024 * 1024
ATT_Q = 128
TOK_TILE = 512
SCAN_TILE = 256
RG_NT = RG_WIDTH // LANES
RG_BAND = 3 * LANES
ROUTER_ROWS = 40
ROUTE_ROWS = 16
MOE_TILE = 256
SLAB = D_MODEL // LANES
PROJ_CHUNK = 512


def _cparams(sem):
    return pltpu.CompilerParams(dimension_semantics=sem, vmem_limit_bytes=VMEM_LIMIT)


def _seq_pos(i, rows, segs, dil=1):
    (n_p, s_p), (_, s_s) = segs
    p_tiles = (n_p * s_p) // rows
    in_p = i < p_tiles
    tps = jnp.where(in_p, (s_p // dil) // rows, (s_s // dil) // rows)
    j = lax.rem(i, tps)
    return in_p, tps, j


def _layer_norm(y, g, b):
    mu = jnp.mean(y, axis=-1, keepdims=True)
    yc = y - mu
    var = jnp.mean(yc * yc, axis=-1, keepdims=True)
    return yc * lax.rsqrt(var + LN_EPS) * g + b


def _rope_slab(t, cos, sin_signed):
    lane = lax.broadcasted_iota(jnp.int32, t.shape, 1)
    first_half = (lane % HEAD_DIM) < (HEAD_DIM // 2)
    partner = jnp.where(first_half, pltpu.roll(t, LANES - HEAD_DIM // 2, 1), pltpu.roll(t, HEAD_DIM // 2, 1))
    return t * cos + partner * sin_signed


def _slab_read(ref):
    return jnp.concatenate([ref[:, c, :] for c in range(ref.shape[1])], axis=1)


def _slab_write(ref, val):
    for c in range(ref.shape[1]):
        ref[:, c, :] = val[:, c * LANES:(c + 1) * LANES]


def _col_chunks(width):
    out, c = [], 0
    while c < width:
        w = min(PROJ_CHUNK, width - c)
        out.append((c, w))
        c += w
    return out


def _proj_rope_kernel(x_ref, w_ref, cos_ref, sin_ref, *out_refs, n_rope, scale_cols, scale, out_cols):
    xb = _slab_read(x_ref).astype(BF16)
    cos = cos_ref[...]
    sin = sin_ref[...]
    col = 0
    for o_ref, width in zip(out_refs, out_cols):
        for c0, cw in _col_chunks(width):
            yc = jnp.dot(xb, w_ref[:, col + c0:col + c0 + cw], preferred_element_type=F32)
            for c in range(cw // LANES):
                slab = (col + c0) // LANES + c
                y = yc[:, c * LANES:(c + 1) * LANES]
                if slab < n_rope:
                    y = _rope_slab(y, cos, sin)
                if slab * LANES < scale_cols:
                    y = y * scale
                o_ref[:, c0 + c * LANES:c0 + (c + 1) * LANES] = y.astype(o_ref.dtype)
        col += width


def _proj_rope(x, T, w, cos_tab, sin_tab, segs, *, dil, n_rope, scale_cols, scale, out_cols, out_dtypes):
    D = D_MODEL
    (n_p, s_p), (n_s, s_s) = segs
    l_p, l_s = s_p // dil, s_s // dil
    tj = min(TOK_TILE, l_s, l_p)
    assert l_p % tj == 0 and l_s % tj == 0 and (n_p * s_p) % (dil * tj) == 0 and x.shape[0] % dil == 0
    xv = x.reshape(x.shape[0] // dil, dil * SLAB, LANES)
    cosv = cos_tab.reshape(cos_tab.shape[0] // dil, dil * LANES)
    sinv = sin_tab.reshape(sin_tab.shape[0] // dil, dil * LANES)
    p_tiles = (n_p * l_p) // tj

    def decode(jt):
        in_p = jt < p_tiles
        lt = jnp.where(in_p, l_p // tj, l_s // tj)
        jt_loc = jnp.where(in_p, jt, jt - p_tiles)
        b = jt_loc // lt
        j0 = lax.rem(jt_loc, lt)
        return in_p, lt, b, j0

    def out_map(jt, r):
        in_p, lt, b, j0 = decode(jt)
        base = jnp.where(in_p, 0, (n_p * s_p) // tj)
        return (base + b * (lt * dil) + r * lt + j0, 0)

    def tab_map(jt, r):
        _, _, _, j0 = decode(jt)
        return (j0, r)

    kern = functools.partial(_proj_rope_kernel, n_rope=n_rope, scale_cols=scale_cols, scale=scale,
                             out_cols=out_cols)
    n_out = w.shape[1]
    return pl.pallas_call(
        kern,
        grid=(T // dil // tj, dil),
        in_specs=[
            pl.BlockSpec((tj, SLAB, LANES), lambda jt, r: (jt, r, 0)),
            pl.BlockSpec((D, n_out), lambda jt, r: (0, 0)),
            pl.BlockSpec((tj, LANES), tab_map),
            pl.BlockSpec((tj, LANES), tab_map),
        ],
        out_specs=[pl.BlockSpec((tj, c), out_map) for c in out_cols],
        out_shape=[jax.ShapeDtypeStruct((T, c), dt) for c, dt in zip(out_cols, out_dtypes)],
        compiler_params=_cparams(("parallel", "parallel")),
    )(xv, w, cosv, sinv)


def _proj_kernel(x_ref, w_ref, *out_refs, out_cols):
    xb = _slab_read(x_ref).astype(BF16)
    col = 0
    for o_ref, width in zip(out_refs, out_cols):
        for c0, cw in _col_chunks(width):
            y = jnp.dot(xb, w_ref[:, col + c0:col + c0 + cw], preferred_element_type=F32)
            o_ref[:, c0:c0 + cw] = y.astype(o_ref.dtype)
        col += width


def _proj(x, T, w, out_cols, out_dtypes):
    D = D_MODEL
    tm = TOK_TILE
    n_out = w.shape[1]
    assert sum(out_cols) == n_out and T % tm == 0
    return pl.pallas_call(
        functools.partial(_proj_kernel, out_cols=out_cols),
        grid=(T // tm,),
        in_specs=[pl.BlockSpec((tm, SLAB, LANES), lambda i: (i, 0, 0)), pl.BlockSpec((D, n_out), lambda i: (0, 0))],
        out_specs=[pl.BlockSpec((tm, c), lambda i: (i, 0)) for c in out_cols],
        out_shape=[jax.ShapeDtypeStruct((T, c), dt) for c, dt in zip(out_cols, out_dtypes)],
        compiler_params=_cparams(("parallel",)),
    )(x, w)


def _band_attn_kernel(*refs, segs, dil, radius, q_per_k, has_sink, want_lse):
    it = iter(refs)
    sink_ref = next(it) if has_sink else None
    q_ref = next(it)
    kp_ref, kc_ref, kn_ref = next(it), next(it), next(it)
    vp_ref, vc_ref, vn_ref = next(it), next(it), next(it)
    o_ref = next(it)
    n_q_slabs = q_ref.shape[1] // LANES

    i = pl.program_id(0)
    _, tps, j = _seq_pos(i, ATT_Q, segs, dil)
    prev_ok = j > 0
    next_ok = j < tps - 1
    W = 2 * radius + ATT_Q
    row = lax.broadcasted_iota(jnp.int32, (ATT_Q, W), 0)
    col = lax.broadcasted_iota(jnp.int32, (ATT_Q, W), 1)
    rel = col - radius - row
    ok = (jnp.abs(rel) <= radius) & ((col >= radius) | prev_ok) & ((col < radius + ATT_Q) | next_ok)
    bias = jnp.where(ok, 0.0, NEG_INF).astype(F32)

    k_all = jnp.concatenate([kp_ref[...], kc_ref[...], kn_ref[...]], axis=0)
    v_all = jnp.concatenate([vp_ref[...], vc_ref[...], vn_ref[...]], axis=0)
    nk = k_all.shape[1] // LANES
    lane_lo = lax.broadcasted_iota(jnp.int32, (1, LANES), 1) < HEAD_DIM
    bias_m = jnp.concatenate([bias] * q_per_k, axis=0) if q_per_k > 1 else bias
    M = ATT_Q * q_per_k
    zero = jnp.zeros((), BF16)

    for kc in range(nk):
        ksl = k_all[:, kc * LANES:(kc + 1) * LANES]
        vsl = v_all[:, kc * LANES:(kc + 1) * LANES]
        slabs = [kc * q_per_k + t for t in range(q_per_k)]
        qm = jnp.concatenate([q_ref[:, m * LANES:(m + 1) * LANES] for m in slabs], axis=0) \
            if q_per_k > 1 else q_ref[:, kc * LANES:(kc + 1) * LANES]
        o_acc = jnp.zeros((M, LANES), F32)
        lse_acc = jnp.zeros((M, LANES), F32)
        for half in range(2):
            keep = lane_lo if half == 0 else jnp.logical_not(lane_lo)
            kx = jnp.where(keep, ksl, zero)
            vx = jnp.where(keep, vsl, zero)
            s = lax.dot_general(qm, kx, (((1,), (1,)), ((), ())), preferred_element_type=F32) + bias_m
            mx = jnp.max(s, axis=1, keepdims=True)
            if has_sink:
                sk = jnp.concatenate(
                    [jnp.full((ATT_Q, 1), sink_ref[2 * m + half], F32) for m in slabs], axis=0)
                mx = jnp.maximum(mx, sk)
            p = jnp.exp(s - mx)
            l = jnp.sum(p, axis=1, keepdims=True)
            if has_sink:
                l = l + jnp.exp(sk - mx)
            pv = jnp.dot(p.astype(BF16), vx, preferred_element_type=F32)
            o_acc = o_acc + pv / l
            if want_lse:
                lse_acc = jnp.where(keep, mx + jnp.log(l), lse_acc)
        for t, m in enumerate(slabs):
            if want_lse:
                o_ref[:, m, :] = o_acc[t * ATT_Q:(t + 1) * ATT_Q]
                o_ref[:, n_q_slabs + m, :] = lse_acc[t * ATT_Q:(t + 1) * ATT_Q]
            else:
                o_ref[:, m * LANES:(m + 1) * LANES] = o_acc[t * ATT_Q:(t + 1) * ATT_Q].astype(o_ref.dtype)


def _band_attn(q, k, v, segs, *, dil, radius, q_per_k, sink=None, want_lse=False):
    T, wq = q.shape
    wk = k.shape[1]
    (n_p, s_p), (n_s, s_s) = segs
    assert ATT_Q % radius == 0
    hb = ATT_Q // radius
    n_halo = T // radius
    nt = T // ATT_Q
    l_p, l_s = s_p // dil, s_s // dil
    assert l_p % ATT_Q == 0 and l_s % ATT_Q == 0
    p_tiles = (n_p * s_p) // ATT_Q

    def out_map(i):
        in_p = i < p_tiles
        nb = jnp.where(in_p, l_p // ATT_Q, l_s // ATT_Q)
        i_loc = jnp.where(in_p, i, i - p_tiles)
        n = i_loc // nb
        jb = lax.rem(i_loc, nb)
        b = n // dil
        r = lax.rem(n, dil)
        base = jnp.where(in_p, 0, p_tiles // dil)
        return (base + b * nb + jb, r, 0)

    in_specs = []
    args = []
    if sink is not None:
        in_specs.append(pl.BlockSpec(memory_space=pltpu.SMEM))
        args.append(sink)
    in_specs.append(pl.BlockSpec((ATT_Q, wq), lambda i: (i, 0)))
    args.append(q)
    for arr in (k, v):
        in_specs += [
            pl.BlockSpec((radius, wk), lambda i: (jnp.maximum(i * hb - 1, 0), 0)),
            pl.BlockSpec((ATT_Q, wk), lambda i: (i, 0)),
            pl.BlockSpec((radius, wk), lambda i: (jnp.minimum((i + 1) * hb, n_halo - 1), 0)),
        ]
        args += [arr, arr, arr]
    if want_lse:
        n_ch = 2 * wq // LANES
        out_shape = jax.ShapeDtypeStruct((T // dil, dil * n_ch, LANES), F32)
        out_specs = pl.BlockSpec((ATT_Q, n_ch, LANES), out_map)
    else:
        assert dil == 1
        out_shape = jax.ShapeDtypeStruct((T, wq), BF16)
        out_specs = pl.BlockSpec((ATT_Q, wq), lambda i: (i, 0))
    kern = functools.partial(_band_attn_kernel, segs=segs, dil=dil, radius=radius, q_per_k=q_per_k,
                             has_sink=sink is not None, want_lse=want_lse)
    out = pl.pallas_call(
        kern, grid=(nt,), in_specs=in_specs, out_specs=out_specs, out_shape=out_shape,
        compiler_params=_cparams(("parallel",)),
    )(*args)
    return out.reshape(T, n_ch, LANES) if want_lse else out


def _split3(v):
    hi = v.astype(BF16)
    r1 = v - hi.astype(F32)
    mid = r1.astype(BF16)
    lo = (r1 - mid.astype(F32)).astype(BF16)
    return hi, mid, lo


def _route(x1, wr_ref, br_ref):
    hi, mid, _ = _split3(x1)
    xs = jnp.concatenate([hi, hi, mid], axis=1)
    lt = lax.dot_general(wr_ref[...], xs, (((1,), (1,)), ((), ())), preferred_element_type=F32) + br_ref[...]
    n_tok = x1.shape[0]
    row = lax.broadcasted_iota(jnp.int32, (MOE_EPG, n_tok), 0)
    big = jnp.int32(MOE_EPG)

    def first_argmax(v):
        m = jnp.max(v, axis=0, keepdims=True)
        return m, jnp.min(jnp.where(v == m, row, big), axis=0, keepdims=True)

    n_e = MOE_GROUPS * MOE_EPG
    gl = lt[n_e:n_e + MOE_EPG]
    gmax, gidx = first_argmax(gl)
    g_w = 1.0 / jnp.sum(jnp.exp(gl - gmax), axis=0, keepdims=True)
    el = lt[(MOE_GROUPS - 1) * MOE_EPG:n_e]
    for g in range(MOE_GROUPS - 2, -1, -1):
        el = jnp.where(gidx == g, lt[g * MOE_EPG:(g + 1) * MOE_EPG], el)
    v1, i1 = first_argmax(el)
    el2 = jnp.where(row == i1, -jnp.inf, el)
    v2, i2 = first_argmax(el2)
    ex = jnp.exp(v2 - v1)
    w1 = 1.0 / (1.0 + ex)
    w2 = ex * w1
    first_lo = i1 < i2
    e_lo = jnp.minimum(i1, i2).astype(F32)
    e_hi = jnp.maximum(i1, i2).astype(F32)
    c_lo = g_w * jnp.where(first_lo, w1, w2)
    c_hi = g_w * jnp.where(first_lo, w2, w1)
    pad = jnp.zeros((ROUTE_ROWS - 5, n_tok), F32)
    return jnp.concatenate([gidx.astype(F32), e_lo, e_hi, c_lo, c_hi, pad], axis=0)


def _rows_to_lanes(rows, sel):
    pieces = jnp.concatenate(_split3(rows), axis=0)
    return lax.dot_general(pieces, sel, (((0,), (0,)), ((), ())), preferred_element_type=F32)


def _gelu_tanh(x):
    return 0.5 * x * (1.0 + jnp.tanh(0.7978845608028654 * (x + 0.044715 * x * x * x)))


def _silu(x):
    return x / (1.0 + jnp.exp(-x))


def _out_ln_kernel(*refs, mode, n_h):
    h_refs = refs[:n_h]
    extra_ref = refs[n_h] if mode == "gla" else None
    base = n_h + (1 if mode == "gla" else 0)
    x_ref, wo_ref, g_ref, b_ref, wr_ref, br_ref, x1_ref, rt_ref = refs[base:base + 8]

    if mode == "plain":
        hb = h_refs[0][...]
    elif mode == "rglru":
        gate, hf, hbw = h_refs
        hb = (_gelu_tanh(gate[...].astype(F32)) * (hf[...].astype(F32) + hbw[...].astype(F32))).astype(BF16)
    elif mode == "gla":
        of, ob, gg = h_refs
        o = of[...] + ob[...]
        parts = []
        for h in range(GLA_HEADS):
            oh = o[:, h * GLA_DVH:(h + 1) * GLA_DVH]
            ms = jnp.mean(oh * oh, axis=-1, keepdims=True)
            parts.append(oh * lax.rsqrt(ms + LN_EPS) * extra_ref[...])
        o = jnp.concatenate(parts, axis=1)
        hb = (o * _silu(gg[...].astype(F32))).astype(BF16)
    else:
        half = h_refs[0].shape[1] // 2
        os_ = [jnp.concatenate([r[:, c, :] for c in range(half)], axis=1) for r in h_refs]
        ls_ = [jnp.concatenate([r[:, half + c, :] for c in range(half)], axis=1) for r in h_refs]
        mx = functools.reduce(jnp.maximum, ls_)
        es = [jnp.exp(l - mx) for l in ls_]
        den = functools.reduce(jnp.add, es)
        o = functools.reduce(jnp.add, [(e / den) * ov for e, ov in zip(es, os_)])
        hb = o.astype(BF16)

    acc = jnp.dot(hb, wo_ref[...], preferred_element_type=F32)
    x1 = _layer_norm(ALPHA * _slab_read(x_ref) + acc, g_ref[...], b_ref[...])
    _slab_write(x1_ref, x1)
    rt_ref[...] = _route(x1, wr_ref, br_ref)


def _out_ln(hs, x, T, wo, ln_g, ln_b, wr, br, *, mode, extra=None):
    D = D_MODEL
    tm = TOK_TILE
    in_specs = [pl.BlockSpec((tm,) + h.shape[1:], (lambda i: (i, 0)) if h.ndim == 2 else (lambda i: (i, 0, 0)))
                for h in hs]
    args = list(hs)
    if mode == "gla":
        in_specs.append(pl.BlockSpec((1, extra.shape[1]), lambda i: (0, 0)))
        args.append(extra)
    in_specs += [
        pl.BlockSpec((tm, SLAB, LANES), lambda i: (i, 0, 0)),
        pl.BlockSpec(wo.shape, lambda i: (0, 0)),
        pl.BlockSpec((1, D), lambda i: (0, 0)),
        pl.BlockSpec((1, D), lambda i: (0, 0)),
        pl.BlockSpec(wr.shape, lambda i: (0, 0)),
        pl.BlockSpec(br.shape, lambda i: (0, 0)),
    ]
    args += [x, wo, ln_g, ln_b, wr, br]
    return pl.pallas_call(
        functools.partial(_out_ln_kernel, mode=mode, n_h=len(hs)),
        grid=(T // tm,),
        in_specs=in_specs,
        out_specs=[pl.BlockSpec((tm, SLAB, LANES), lambda i: (i, 0, 0)),
                   pl.BlockSpec((ROUTE_ROWS, tm), lambda i: (0, i))],
        out_shape=[jax.ShapeDtypeStruct((T, SLAB, LANES), F32),
                   jax.ShapeDtypeStruct((ROUTE_ROWS, T), F32)],
        compiler_params=_cparams(("parallel",)),
    )(*args)


def _moe_kernel(tg_ref, tlo_ref, thi_ref, nv_ref, idx_ref, x_hbm, cw_ref, sel_ref, wgl_ref, wul_ref, wdl_ref,
                wgh_ref, wuh_ref, wdh_ref, g_ref, b_ref, out_hbm, xbuf, obuf, gsem, ssem, *, nt, n_tok):
    i = pl.program_id(0)
    tm = xbuf.shape[1]
    slot = lax.rem(i, 2)
    used = nv_ref[i] > 0
    nxt = jnp.minimum(i + 1, nt - 1)
    next_used = jnp.logical_and(i + 1 < nt, nv_ref[nxt] > 0)

    def gather_start(tile, sl):
        for r in range(tm):
            tok = idx_ref[tile * tm + r]
            pltpu.make_async_copy(x_hbm.at[tok], xbuf.at[sl, r], gsem.at[sl]).start()

    def gather_wait(sl):
        pltpu.make_async_copy(x_hbm.at[pl.ds(0, tm)], xbuf.at[sl], gsem.at[sl]).wait()

    def scatter_start(tile, sl):
        n = nv_ref[tile]
        dump = n_tok + sl * tm
        for r in range(tm):
            tok = jnp.where(r < n, idx_ref[tile * tm + r], dump + r)
            pltpu.make_async_copy(obuf.at[sl, r], out_hbm.at[tok], ssem.at[sl]).start()

    def scatter_wait(sl):
        pltpu.make_async_copy(obuf.at[sl], out_hbm.at[pl.ds(0, tm)], ssem.at[sl]).wait()

    @pl.when(i == 0)
    def _():
        obuf[...] = jnp.zeros_like(obuf)
        for sl in range(2):
            cp = pltpu.make_async_copy(obuf.at[sl], out_hbm.at[pl.ds(n_tok + sl * tm, tm)], ssem.at[sl])
            cp.start()
            cp.wait()

    @pl.when(jnp.logical_and(i == 0, used))
    def _():
        gather_start(0, 0)

    @pl.when(used)
    def _():
        gather_wait(slot)

        @pl.when(next_used)
        def _():
            gather_start(i + 1, 1 - slot)

        @pl.when(i >= 2)
        def _():
            scatter_wait(slot)

        x = _slab_read(xbuf.at[slot])
        cw = _rows_to_lanes(cw_ref[...], sel_ref[...])
        xb = x.astype(BF16)
        acc = jnp.zeros((tm, D_MODEL), F32)
        for wg_ref, wu_ref, wd_ref, lane in ((wgl_ref, wul_ref, wdl_ref, 0), (wgh_ref, wuh_ref, wdh_ref, 1)):
            hg = jnp.dot(xb, wg_ref[0, 0], preferred_element_type=F32)
            hu = jnp.dot(xb, wu_ref[0, 0], preferred_element_type=F32)
            h = _silu(hg) * hu * cw[:, lane:lane + 1]
            acc = acc + jnp.dot(h.astype(BF16), wd_ref[0, 0], preferred_element_type=F32)
        _slab_write(obuf.at[slot], _layer_norm(ALPHA * x + acc, g_ref[...], b_ref[...]))
        scatter_start(i, slot)

        @pl.when(jnp.logical_not(next_used))
        def _():
            @pl.when(i >= 1)
            def _():
                scatter_wait(1 - slot)
            scatter_wait(slot)


_PAIR_LO = tuple(a for a in range(MOE_EPG) for b in range(a + 1, MOE_EPG))
_PAIR_HI = tuple(b for a in range(MOE_EPG) for b in range(a + 1, MOE_EPG))
N_PAIRS = len(_PAIR_LO)
N_CLASSES = MOE_GROUPS * N_PAIRS


def _moe_schedule(rt, tm, nt):
    T = rt.shape[1]
    i32 = jnp.int32
    g, lo, hi = rt[0].astype(i32), rt[1].astype(i32), rt[2].astype(i32)
    cls = g * N_PAIRS + lo * (2 * MOE_EPG - 1 - lo) // 2 + (hi - lo - 1)
    cls_sorted, order = lax.sort((cls, jnp.arange(T, dtype=i32)), num_keys=1, is_stable=True)
    starts = jnp.searchsorted(cls_sorted, jnp.arange(N_CLASSES + 1, dtype=i32), side="left").astype(i32)
    counts = starts[1:] - starts[:-1]
    tiles_per = (counts + tm - 1) // tm
    cum = jnp.cumsum(tiles_per)
    n_used = cum[-1]
    t = jnp.arange(nt, dtype=i32)
    tc = jnp.minimum(t, n_used - 1)
    tcls = jnp.searchsorted(cum, tc, side="right").astype(i32)
    k = tc - (cum[tcls] - tiles_per[tcls])
    nvalid = jnp.where(t < n_used, jnp.clip(counts[tcls] - k * tm, 0, tm), 0).astype(i32)
    r = jnp.arange(tm, dtype=i32)
    pos = starts[tcls][:, None] + k[:, None] * tm + jnp.minimum(r[None, :], jnp.maximum(nvalid, 1)[:, None] - 1)
    idx = order[jnp.clip(pos, 0, T - 1)].reshape(-1)
    pr = tcls % N_PAIRS
    tile_lo = jnp.asarray(_PAIR_LO, i32)[pr]
    tile_hi = jnp.asarray(_PAIR_HI, i32)[pr]
    return (tcls // N_PAIRS).astype(i32), tile_lo, tile_hi, nvalid, idx.astype(i32)


def _moe(x1, rt, wg, wu, wd, ln_g, ln_b):
    T = x1.shape[0]
    D = D_MODEL
    tm = MOE_TILE
    nt = T // tm + N_CLASSES
    sched = _moe_schedule(rt, tm, nt)
    idx = sched[-1]
    cw = jnp.take(rt[3:5], idx, axis=1)
    cw = jnp.concatenate([cw, jnp.zeros((ROUTE_ROWS - 2, nt * tm), F32)], axis=0)
    sel = jnp.tile(jnp.eye(ROUTE_ROWS, LANES, dtype=BF16), (3, 1))
    lo_map = lambda i, tg, tlo, thi, nv, idx: (tg[i], tlo[i], 0, 0)
    hi_map = lambda i, tg, tlo, thi, nv, idx: (tg[i], thi[i], 0, 0)
    const = lambda i, tg, tlo, thi, nv, idx: (0, 0)
    grid_spec = pltpu.PrefetchScalarGridSpec(
        num_scalar_prefetch=5,
        grid=(nt,),
        in_specs=[
            pl.BlockSpec(memory_space=pl.ANY),
            pl.BlockSpec((ROUTE_ROWS, tm), lambda i, tg, tlo, thi, nv, idx: (0, i)),
            pl.BlockSpec(sel.shape, const),
            pl.BlockSpec((1, 1, D, MOE_FF), lo_map),
            pl.BlockSpec((1, 1, D, MOE_FF), lo_map),
            pl.BlockSpec((1, 1, MOE_FF, D), lo_map),
            pl.BlockSpec((1, 1, D, MOE_FF), hi_map),
            pl.BlockSpec((1, 1, D, MOE_FF), hi_map),
            pl.BlockSpec((1, 1, MOE_FF, D), hi_map),
            pl.BlockSpec((1, D), const),
            pl.BlockSpec((1, D), const),
        ],
        out_specs=pl.BlockSpec(memory_space=pl.ANY),
        scratch_shapes=[pltpu.VMEM((2, tm, SLAB, LANES), F32), pltpu.VMEM((2, tm, SLAB, LANES), F32),
                        pltpu.SemaphoreType.DMA((2,)), pltpu.SemaphoreType.DMA((2,))],
    )
    return pl.pallas_call(
        functools.partial(_moe_kernel, nt=nt, n_tok=T),
        grid_spec=grid_spec,
        out_shape=jax.ShapeDtypeStruct((T + 2 * tm, SLAB, LANES), F32),
        compiler_params=_cparams(("arbitrary",)),
    )(*sched, x1, cw, sel, wg, wu, wd, wg, wu, wd, ln_g, ln_b)


def _rglru_stream(z, u_ref, up_ref, un_ref, w_ref, cw_ref, cb_ref, ba_ref, bi_ref, lam_ref,
                  h_out_ref, a_s, b_s, h_s, carry_ref, prev_ok, next_ok, reset):
    tt = u_ref.shape[0]
    u_mid = u_ref[...]
    up = jnp.where(prev_ok, up_ref[...], 0.0)
    un = jnp.where(next_ok, un_ref[...], 0.0)
    ext = jnp.concatenate([up, u_mid, un], axis=0)
    left = RG_CONV // 2
    u = cb_ref[...]
    for kk in range(RG_CONV):
        off = SUBLANES - left + kk
        u = u + cw_ref[kk:kk + 1, :] * ext[off:off + tt]
    ub = u.astype(BF16)
    sp = jnp.maximum(-lam_ref[...], 0.0) + jnp.log(1.0 + jnp.exp(-jnp.abs(lam_ref[...])))
    for n in range(RG_NT):
        s0 = min(max(n - 1, 0), RG_NT - 3) * LANES
        zz = jnp.dot(ub[:, s0:s0 + RG_BAND], w_ref[n], preferred_element_type=F32)
        sl = slice(n * LANES, (n + 1) * LANES)
        r = jax.nn.sigmoid(zz[:, :LANES] + ba_ref[:, sl])
        ig = jax.nn.sigmoid(zz[:, LANES:] + bi_ref[:, sl])
        log_a = -RG_C * r * sp[:, sl]
        a = jnp.exp(log_a)
        a_s[:, sl] = a
        b_s[:, sl] = jnp.sqrt(-jnp.tanh(log_a) * (a * a + 1.0)) * ig * u[:, sl]

    @pl.when(reset)
    def _():
        carry_ref[...] = jnp.zeros_like(carry_ref)

    n_grp = tt // SUBLANES

    def body(gi, h):
        g = gi if z == 0 else n_grp - 1 - gi
        base = pl.multiple_of(g * SUBLANES, SUBLANES)
        a8 = a_s[pl.ds(base, SUBLANES), :]
        b8 = b_s[pl.ds(base, SUBLANES), :]
        rows = [None] * SUBLANES
        order = range(SUBLANES) if z == 0 else range(SUBLANES - 1, -1, -1)
        for r_ in order:
            h = a8[r_:r_ + 1] * h + b8[r_:r_ + 1]
            rows[r_] = h
        h_s[pl.ds(base, SUBLANES), :] = jnp.concatenate(rows, axis=0)
        return h

    h_last = lax.fori_loop(0, n_grp, body, carry_ref[...])
    carry_ref[...] = h_last
    h_out_ref[...] = h_s[...].astype(h_out_ref.dtype)


def _rglru_scan_kernel(uf_ref, ufp_ref, ufn_ref, ubk_ref, ubp_ref, ubn_ref, wf_ref, wb_ref, cw_ref, cb_ref,
                       ba_ref, bi_ref, lam_ref, hf_ref, hb_ref, a_s, b_s, h_s, cf_ref, cbk_ref, *, segs, nt):
    i = pl.program_id(0)
    tt = uf_ref.shape[0]
    _, tps, j = _seq_pos(i, tt, segs)
    _rglru_stream(0, uf_ref, ufp_ref, ufn_ref, wf_ref, cw_ref, cb_ref, ba_ref.at[0:1], bi_ref.at[0:1],
                  lam_ref.at[0:1], hf_ref, a_s, b_s, h_s, cf_ref, j > 0, j < tps - 1, j == 0)
    ib = nt - 1 - i
    _, tps_b, jb = _seq_pos(ib, tt, segs)
    _rglru_stream(1, ubk_ref, ubp_ref, ubn_ref, wb_ref, cw_ref, cb_ref, ba_ref.at[1:2], bi_ref.at[1:2],
                  lam_ref.at[1:2], hb_ref, a_s, b_s, h_s, cbk_ref, jb > 0, jb < tps_b - 1, jb == tps_b - 1)


def _rglru_scan(u_pre, w_band, conv_w, conv_b, ba, bi, lam, segs):
    T, W = u_pre.shape
    tt = SCAN_TILE
    nt = T // tt
    hpt = tt // SUBLANES
    n_h = T // SUBLANES
    fwd = lambda i: (i, 0)
    bwd = lambda i: (nt - 1 - i, 0)

    def halo_specs(idx):
        return [
            pl.BlockSpec((tt, W), lambda i: (idx(i), 0)),
            pl.BlockSpec((SUBLANES, W), lambda i: (jnp.maximum(idx(i) * hpt - 1, 0), 0)),
            pl.BlockSpec((SUBLANES, W), lambda i: (jnp.minimum((idx(i) + 1) * hpt, n_h - 1), 0)),
        ]

    const2 = lambda i: (0, 0)
    const3 = lambda i: (0, 0, 0)
    in_specs = halo_specs(lambda i: i) + halo_specs(lambda i: nt - 1 - i) + [
        pl.BlockSpec(w_band.shape[1:], const3),
        pl.BlockSpec(w_band.shape[1:], const3),
        pl.BlockSpec(conv_w.shape, const2),
        pl.BlockSpec(conv_b.shape, const2),
        pl.BlockSpec(ba.shape, const2),
        pl.BlockSpec(bi.shape, const2),
        pl.BlockSpec(lam.shape, const2),
    ]
    return pl.pallas_call(
        functools.partial(_rglru_scan_kernel, segs=segs, nt=nt),
        grid=(nt,),
        in_specs=in_specs,
        out_specs=[pl.BlockSpec((tt, W), fwd), pl.BlockSpec((tt, W), bwd)],
        out_shape=[jax.ShapeDtypeStruct((T, W), BF16), jax.ShapeDtypeStruct((T, W), BF16)],
        scratch_shapes=[pltpu.VMEM((tt, W), F32), pltpu.VMEM((tt, W), F32), pltpu.VMEM((tt, W), F32),
                        pltpu.VMEM((1, W), F32), pltpu.VMEM((1, W), F32)],
        compiler_params=_cparams(("arbitrary",)),
    )(u_pre, u_pre, u_pre, u_pre, u_pre, u_pre, w_band[0], w_band[1], conv_w, conv_b, ba, bi, lam)


def _rglru_band_weights(wa, wi):
    def dense(w):
        eye = jnp.eye(RG_BLOCKS, dtype=w.dtype)
        return jnp.einsum("ncd,nm->ncmd", w, eye).reshape(RG_WIDTH, RG_WIDTH)

    out = []
    for z in range(2):
        da, di = dense(wa[z]), dense(wi[z])
        tiles = []
        for n in range(RG_NT):
            s0 = min(max(n - 1, 0), RG_NT - 3) * LANES
            sl = slice(n * LANES, (n + 1) * LANES)
            tiles.append(jnp.concatenate([da[s0:s0 + RG_BAND, sl], di[s0:s0 + RG_BAND, sl]], axis=1))
        out.append(jnp.stack(tiles))
    return jnp.stack(out).astype(BF16)


def _gla_stream(z, q_ref, k_ref, v_ref, lr_ref, wa2_ref, ba_ref, o_ref, st_ref, reset):
    tt = q_ref.shape[0]
    C = GLA_CHUNK

    @pl.when(reset)
    def _():
        st_ref[z] = jnp.zeros(st_ref.shape[1:], F32)

    zz = jnp.dot(lr_ref[...].astype(BF16), wa2_ref[:, z * GLA_DK:(z + 1) * GLA_DK],
                 preferred_element_type=F32) + ba_ref[:, z * GLA_DK:(z + 1) * GLA_DK]
    log_a = -(jnp.maximum(-zz, 0.0) + jnp.log(1.0 + jnp.exp(-jnp.abs(zz)))) / GLA_TAU
    ri = lax.broadcasted_iota(jnp.int32, (C, C), 0)
    ci = lax.broadcasted_iota(jnp.int32, (C, C), 1)
    causal = (ri >= ci) if z == 0 else (ri <= ci)
    tri = causal.astype(F32)
    mid = C // 2 if z == 0 else C - 1 - C // 2
    last = C - 1 if z == 0 else 0
    scale = GLA_DKH ** -0.5
    chunks = range(tt // C) if z == 0 else range(tt // C - 1, -1, -1)
    for c in chunks:
        rs = slice(c * C, (c + 1) * C)
        b = jnp.dot(tri, log_a[rs], preferred_element_type=F32, precision=lax.Precision.HIGHEST)
        b_mid = b[mid:mid + 1]
        b_last = b[last:last + 1]
        qc = q_ref[rs, :] * scale
        kc = k_ref[rs, :]
        qd = (qc * jnp.exp(b - b_mid)).astype(BF16)
        kd = (kc * jnp.exp(b_mid - b)).astype(BF16)
        ks = (kc * jnp.exp(b_last - b)).astype(BF16)
        qb = (qc * jnp.exp(b)).astype(BF16)
        dec = jnp.exp(b_last)
        for h in range(GLA_HEADS):
            ksl = slice(h * GLA_DKH, (h + 1) * GLA_DKH)
            vsl = slice(h * GLA_DVH, (h + 1) * GLA_DVH)
            vc = v_ref[rs, vsl]
            att = lax.dot_general(qd[:, ksl], kd[:, ksl], (((1,), (1,)), ((), ())), preferred_element_type=F32)
            att = jnp.where(causal, att, 0.0)
            o = jnp.dot(att.astype(BF16), vc, preferred_element_type=F32)
            st = st_ref[z, h]
            o = o + lax.dot_general(qb[:, ksl], st.astype(BF16), (((1,), (1,)), ((), ())),
                                    preferred_element_type=F32)
            o_ref[rs, vsl] = o
            upd = lax.dot_general(vc, ks[:, ksl], (((0,), (0,)), ((), ())), preferred_element_type=F32)
            st_ref[z, h] = st * dec[:, ksl] + upd


def _gla_kernel(qf, kf, vf, lf, qb, kb, vb, lb, wa2_ref, ba_ref, of_ref, ob_ref, st_ref, *, segs, nt):
    i = pl.program_id(0)
    tt = qf.shape[0]
    _, tps, j = _seq_pos(i, tt, segs)
    _gla_stream(0, qf, kf, vf, lf, wa2_ref, ba_ref, of_ref, st_ref, j == 0)
    ib = nt - 1 - i
    _, tps_b, jb = _seq_pos(ib, tt, segs)
    _gla_stream(1, qb, kb, vb, lb, wa2_ref, ba_ref, ob_ref, st_ref, jb == tps_b - 1)


def _gla(q, k, v, lr, wa2, ba, segs):
    T = q.shape[0]
    tt = SCAN_TILE
    nt = T // tt
    fwd = lambda i: (i, 0)
    bwd = lambda i: (nt - 1 - i, 0)
    in_specs = []
    for idx in (fwd, bwd):
        in_specs += [pl.BlockSpec((tt, GLA_DK), idx), pl.BlockSpec((tt, GLA_DK), idx),
                     pl.BlockSpec((tt, GLA_DV), idx), pl.BlockSpec((tt, LANES), idx)]
    in_specs += [pl.BlockSpec(wa2.shape, lambda i: (0, 0)), pl.BlockSpec(ba.shape, lambda i: (0, 0))]
    return pl.pallas_call(
        functools.partial(_gla_kernel, segs=segs, nt=nt),
        grid=(nt,),
        in_specs=in_specs,
        out_specs=[pl.BlockSpec((tt, GLA_DV), fwd), pl.BlockSpec((tt, GLA_DV), bwd)],
        out_shape=[jax.ShapeDtypeStruct((T, GLA_DV), F32), jax.ShapeDtypeStruct((T, GLA_DV), F32)],
        scratch_shapes=[pltpu.VMEM((2, GLA_HEADS, GLA_DVH, GLA_DKH), F32)],
        compiler_params=_cparams(("arbitrary",)),
    )(q, k, v, lr, q, k, v, lr, wa2, ba)


def _rope_tables(s_max):
    half = HEAD_DIM // 2
    inv = ROPE_THETA ** (-jnp.arange(half, dtype=F32) / half)
    ang = jnp.arange(s_max, dtype=F32)[:, None] * inv[None, :]
    cos, sin = jnp.cos(ang), jnp.sin(ang)
    cos_t = jnp.tile(jnp.concatenate([cos, cos], axis=1), (1, LANES // HEAD_DIM))
    sin_t = jnp.tile(jnp.concatenate([-sin, sin], axis=1), (1, LANES // HEAD_DIM))
    return cos_t, sin_t


def _router_weights(wgr, bgr, wer, ber):
    n_e = MOE_GROUPS * MOE_EPG
    w = jnp.concatenate([jnp.transpose(wer, (0, 2, 1)).reshape(n_e, D_MODEL), wgr.T], axis=0)
    b = jnp.concatenate([ber.reshape(n_e), bgr])
    n_pad = ROUTER_ROWS - w.shape[0]
    w = jnp.pad(w, ((0, n_pad), (0, 0)))
    b = jnp.concatenate([b, jnp.full((n_pad,), NEG_INF, F32)])
    w_hi = w.astype(BF16)
    w_mid = (w - w_hi.astype(F32)).astype(BF16)
    return jnp.concatenate([w_hi, w_mid, w_hi], axis=1), b[:, None]


def _mixer(kind, jl, x, segs, p, cos_t, sin_t, lg1, lb1, wr, br):
    D = D_MODEL
    T = sum(n * s for n, s in segs)
    if kind == 0:
        nq = A_HEADS * HEAD_DIM
        nkv = A_KV_HEADS * HEAD_DIM
        w = p["a_wqkv"][jl]
        wq, wk, wv = w[:, :nq], w[:, nq:nq + nkv], w[:, nq + nkv:]
        dup = lambda t: jnp.repeat(t.reshape(D, A_KV_HEADS, 1, HEAD_DIM), 2, axis=2).reshape(D, 2 * nkv)
        w_ext = jnp.concatenate([wq, dup(wk), dup(wv)], axis=1).astype(BF16)
        q, kd, vd = _proj_rope(x, T, w_ext, cos_t, sin_t, segs, dil=1, n_rope=(nq + 2 * nkv) // LANES,
                               scale_cols=nq, scale=HEAD_DIM ** -0.5, out_cols=(nq, 2 * nkv, 2 * nkv),
                               out_dtypes=(BF16, BF16, BF16))
        o = _band_attn(q, kd, vd, segs, dil=1, radius=A_RADIUS, q_per_k=2, sink=p["a_sink"][jl])
        return _out_ln([o], x, T, p["a_wo"][jl].astype(BF16), lg1, lb1, wr, br, mode="plain")
    if kind == 1:
        gate, u_pre = _proj(x, T, p["b_win"][jl].astype(BF16), (RG_WIDTH, RG_WIDTH), (BF16, F32))
        w_band = _rglru_band_weights(p["b_wa"][jl], p["b_wi"][jl])
        hf, hb = _rglru_scan(u_pre, w_band, p["b_conv_w"][jl], p["b_conv_b"][jl][None, :], p["b_ba"][jl],
                             p["b_bi"][jl], p["b_lam"][jl], segs)
        return _out_ln([gate, hf, hb], x, T, p["b_wo"][jl].astype(BF16), lg1, lb1, wr, br, mode="rglru")
    if kind == 2:
        wa1 = jnp.concatenate([p["c_wa1"][jl][0], p["c_wa1"][jl][1]], axis=1)
        wa1 = jnp.pad(wa1, ((0, 0), (0, LANES - wa1.shape[1])))
        w_all = jnp.concatenate([p["c_wqkvg"][jl], wa1], axis=1).astype(BF16)
        q, k, v, g, lr = _proj(x, T, w_all, (GLA_DK, GLA_DK, GLA_DV, GLA_DV, LANES), (F32, F32, BF16, BF16, F32))
        wa2 = jnp.zeros((LANES, 2 * GLA_DK), F32)
        wa2 = wa2.at[:GLA_RANK, :GLA_DK].set(p["c_wa2"][jl][0])
        wa2 = wa2.at[GLA_RANK:2 * GLA_RANK, GLA_DK:].set(p["c_wa2"][jl][1])
        of, ob = _gla(q, k, v, lr, wa2.astype(BF16), p["c_ba"][jl].reshape(1, 2 * GLA_DK), segs)
        return _out_ln([of, ob, g], x, T, p["c_wo"][jl].astype(BF16), lg1, lb1, wr, br, mode="gla",
                       extra=p["c_norm_g"][jl][None, :])
    ols = []
    for gi, (window, dil) in enumerate(DIL_GROUPS):
        w_g = p["d_wqkv"][jl][:, gi * 3 * DIL_WIDTH:(gi + 1) * 3 * DIL_WIDTH].astype(BF16)
        q, k, v = _proj_rope(x, T, w_g, cos_t, sin_t, segs, dil=dil, n_rope=2 * DIL_WIDTH // LANES,
                             scale_cols=DIL_WIDTH, scale=HEAD_DIM ** -0.5,
                             out_cols=(DIL_WIDTH,) * 3, out_dtypes=(BF16,) * 3)
        ols.append(_band_attn(q, k, v, segs, dil=dil, radius=window // (2 * dil), q_per_k=1, want_lse=True))
    return _out_ln(ols, x, T, p["d_wo"][jl].astype(BF16), lg1, lb1, wr, br, mode="dil")


def kernel(x_prompt, x_sample, ln_g, ln_b, a_wqkv, a_sink, a_wo, b_win, b_conv_w, b_conv_b, b_wa, b_ba, b_wi,
           b_bi, b_lam, b_wo, c_wqkvg, c_wa1, c_wa2, c_ba, c_norm_g, c_wo, d_wqkv, d_wo, m_wgr, m_bgr, m_wer,
           m_ber, m_wg, m_wu, m_wd):
    p = dict(a_wqkv=a_wqkv, a_sink=a_sink, a_wo=a_wo, b_win=b_win, b_conv_w=b_conv_w, b_conv_b=b_conv_b,
             b_wa=b_wa, b_ba=b_ba, b_wi=b_wi, b_bi=b_bi, b_lam=b_lam, b_wo=b_wo, c_wqkvg=c_wqkvg, c_wa1=c_wa1,
             c_wa2=c_wa2, c_ba=c_ba, c_norm_g=c_norm_g, c_wo=c_wo, d_wqkv=d_wqkv, d_wo=d_wo)
    n_p, s_p, D = x_prompt.shape
    n_s, s_s, _ = x_sample.shape
    segs = ((n_p, s_p), (n_s, s_s))
    assert (n_p * s_p) % s_s == 0
    t_p, T = n_p * s_p, n_p * s_p + n_s * s_s
    x = jnp.concatenate([x_prompt.reshape(t_p, D), x_sample.reshape(T - t_p, D)], axis=0).reshape(T, SLAB, LANES)
    cos_t, sin_t = _rope_tables(max(s_p, s_s))
    wg_b, wu_b, wd_b = m_wg.astype(BF16), m_wu.astype(BF16), m_wd.astype(BF16)

    for layer in range(DEPTH):
        lg1, lb1 = ln_g[layer, 0][None, :], ln_b[layer, 0][None, :]
        lg2, lb2 = ln_g[layer, 1][None, :], ln_b[layer, 1][None, :]
        wr, br = _router_weights(m_wgr[layer], m_bgr[layer], m_wer[layer], m_ber[layer])
        x1, rt = _mixer(layer % 4, layer // 4, x, segs, p, cos_t, sin_t, lg1, lb1, wr, br)
        x = _moe(x1, rt, wg_b[layer], wu_b[layer], wd_b[layer], lg2, lb2)

    return x[:t_p].reshape(n_p, s_p, D), x[t_p:T].reshape(n_s, s_s, D)
```

```python
import functools

import jax
import jax.numpy as jnp
from jax import lax
from jax.experimental import pallas as pl
from jax.experimental.pallas import tpu as pltpu

F32 = jnp.float32
BF16 = jnp.bfloat16

D_MODEL = 1024
HEAD_DIM = 64
ROPE_THETA = 10000.0
A_HEADS = 16
A_KV_HEADS = 4
A_RADIUS = 128
RG_WIDTH = 1408
RG_BLOCKS = 16
RG_BW = RG_WIDTH // RG_BLOCKS
RG_CONV = 4
RG_C = 8.0
GLA_HEADS = 4
GLA_DK = 512
GLA_DV = 1024
GLA_DKH = 128
GLA_DVH = 256
GLA_RANK = 16
GLA_TAU = 16.0
GLA_CHUNK = 64
DIL_GROUPS = ((128, 1), (512, 4), (2048, 16))
N_DIL = 3
DIL_HEADS = 8
DIL_WIDTH = 512
MOE_GROUPS = 4
MOE_EPG = 8
MOE_FF = 256
DEPTH = 4
ALPHA = (2 * DEPTH) ** 0.25
LN_EPS = 1e-5
NEG_INF = -1e30

LANES = 128
SUBLANES = 8
VMEM_LIMIT = 52 * 1024 * 1024
ATT_Q = 128
TOK_TILE = 512
SCAN_TILE = 256
RG_NT = RG_WIDTH // LANES
RG_BAND = 3 * LANES
ROUTER_ROWS = 40
ROUTE_ROWS = 16
MOE_TILE = 256
SLAB = D_MODEL // LANES
PROJ_CHUNK = 512


def _cparams(sem):
    return pltpu.CompilerParams(dimension_semantics=sem, vmem_limit_bytes=VMEM_LIMIT)


def _seq_pos(i, rows, segs, dil=1):
    (n_p, s_p), (_, s_s) = segs
    p_tiles = (n_p * s_p) // rows
    in_p = i < p_tiles
    tps = jnp.where(in_p, (s_p // dil) // rows, (s_s // dil) // rows)
    j = lax.rem(i, tps)
    return in_p, tps, j


def _layer_norm(y, g, b):
    mu = jnp.mean(y, axis=-1, keepdims=True)
    yc = y - mu
    var = jnp.mean(yc * yc, axis=-1, keepdims=True)
    return yc * lax.rsqrt(var + LN_EPS) * g + b


def _rope_slab(t, cos, sin_signed):
    lane = lax.broadcasted_iota(jnp.int32, t.shape, 1)
    first_half = (lane % HEAD_DIM) < (HEAD_DIM // 2)
    partner = jnp.where(first_half, pltpu.roll(t, LANES - HEAD_DIM // 2, 1), pltpu.roll(t, HEAD_DIM // 2, 1))
    return t * cos + partner * sin_signed


def _slab_read(ref, chunks=range(SLAB)):
    if len(ref.shape) == 3:
        return jnp.concatenate([ref[:, c, :] for c in chunks], axis=1)
    rows = ref.shape[0] // SLAB
    return jnp.concatenate([ref[pl.ds(c, rows, stride=SLAB), :] for c in chunks], axis=1)


def _slab_write(ref, val, first_chunk=0):
    for c in range(val.shape[1] // LANES):
        v = val[:, c * LANES:(c + 1) * LANES]
        if len(ref.shape) == 3:
            ref[:, first_chunk + c, :] = v
        else:
            ref[pl.ds(first_chunk + c, ref.shape[0] // SLAB, stride=SLAB), :] = v


def _col_chunks(width):
    out, c = [], 0
    while c < width:
        w = min(PROJ_CHUNK, width - c)
        out.append((c, w))
        c += w
    return out


def _proj_rope_kernel(x_ref, w_ref, cos_ref, sin_ref, *out_refs, n_rope, scale_cols, scale, out_cols):
    xb = _slab_read(x_ref).astype(BF16)
    cos = cos_ref[...]
    sin = sin_ref[...]
    col = 0
    for o_ref, width in zip(out_refs, out_cols):
        for c0, cw in _col_chunks(width):
            yc = jnp.dot(xb, w_ref[:, col + c0:col + c0 + cw], preferred_element_type=F32)
            for c in range(cw // LANES):
                slab = (col + c0) // LANES + c
                y = yc[:, c * LANES:(c + 1) * LANES]
                if slab < n_rope:
                    y = _rope_slab(y, cos, sin)
                if slab * LANES < scale_cols:
                    y = y * scale
                o_ref[:, c0 + c * LANES:c0 + (c + 1) * LANES] = y.astype(o_ref.dtype)
        col += width


def _proj_rope(x, T, w, cos_tab, sin_tab, segs, *, dil, n_rope, scale_cols, scale, out_cols, out_dtypes):
    D = D_MODEL
    (n_p, s_p), (n_s, s_s) = segs
    l_p, l_s = s_p // dil, s_s // dil
    tj = min(TOK_TILE, l_s, l_p)
    assert l_p % tj == 0 and l_s % tj == 0 and (n_p * s_p) % (dil * tj) == 0 and x.shape[0] % (dil * SLAB) == 0
    if dil == 1:
        xv = x
        x_spec = pl.BlockSpec((tj * SLAB, LANES), lambda jt, r: (jt, 0))
    else:
        xv = x.reshape(x.shape[0] // (dil * SLAB), dil * SLAB, LANES)
        x_spec = pl.BlockSpec((tj, SLAB, LANES), lambda jt, r: (jt, r, 0))
    cosv = cos_tab.reshape(cos_tab.shape[0] // dil, dil * LANES)
    sinv = sin_tab.reshape(sin_tab.shape[0] // dil, dil * LANES)
    p_tiles = (n_p * l_p) // tj

    def decode(jt):
        in_p = jt < p_tiles
        lt = jnp.where(in_p, l_p // tj, l_s // tj)
        jt_loc = jnp.where(in_p, jt, jt - p_tiles)
        b = jt_loc // lt
        j0 = lax.rem(jt_loc, lt)
        return in_p, lt, b, j0

    def out_map(jt, r):
        in_p, lt, b, j0 = decode(jt)
        base = jnp.where(in_p, 0, (n_p * s_p) // tj)
        return (base + b * (lt * dil) + r * lt + j0, 0)

    def tab_map(jt, r):
        _, _, _, j0 = decode(jt)
        return (j0, r)

    kern = functools.partial(_proj_rope_kernel, n_rope=n_rope, scale_cols=scale_cols, scale=scale,
                             out_cols=out_cols)
    n_out = w.shape[1]
    return pl.pallas_call(
        kern,
        grid=(T // dil // tj, dil),
        in_specs=[
            x_spec,
            pl.BlockSpec((D, n_out), lambda jt, r: (0, 0)),
            pl.BlockSpec((tj, LANES), tab_map),
            pl.BlockSpec((tj, LANES), tab_map),
        ],
        out_specs=[pl.BlockSpec((tj, c), out_map) for c in out_cols],
        out_shape=[jax.ShapeDtypeStruct((T, c), dt) for c, dt in zip(out_cols, out_dtypes)],
        compiler_params=_cparams(("parallel", "parallel")),
    )(xv, w, cosv, sinv)


def _proj_kernel(x_ref, w_ref, *out_refs, out_cols):
    xb = _slab_read(x_ref).astype(BF16)
    col = 0
    for o_ref, width in zip(out_refs, out_cols):
        for c0, cw in _col_chunks(width):
            y = jnp.dot(xb, w_ref[:, col + c0:col + c0 + cw], preferred_element_type=F32)
            o_ref[:, c0:c0 + cw] = y.astype(o_ref.dtype)
        col += width


def _proj(x, T, w, out_cols, out_dtypes):
    D = D_MODEL
    tm = TOK_TILE
    n_out = w.shape[1]
    assert sum(out_cols) == n_out and T % tm == 0
    return pl.pallas_call(
        functools.partial(_proj_kernel, out_cols=out_cols),
        grid=(T // tm,),
        in_specs=[pl.BlockSpec((tm * SLAB, LANES), lambda i: (i, 0)), pl.BlockSpec((D, n_out), lambda i: (0, 0))],
        out_specs=[pl.BlockSpec((tm, c), lambda i: (i, 0)) for c in out_cols],
        out_shape=[jax.ShapeDtypeStruct((T, c), dt) for c, dt in zip(out_cols, out_dtypes)],
        compiler_params=_cparams(("parallel",)),
    )(x, w)


def _band_attn_kernel(*refs, segs, dil, radius, q_per_k, has_sink, want_lse):
    it = iter(refs)
    sink_ref = next(it) if has_sink else None
    q_ref = next(it)
    kp_ref, kc_ref, kn_ref = next(it), next(it), next(it)
    vp_ref, vc_ref, vn_ref = next(it), next(it), next(it)
    o_ref = next(it)
    n_q_slabs = q_ref.shape[1] // LANES

    i = pl.program_id(0)
    _, tps, j = _seq_pos(i, ATT_Q, segs, dil)
    prev_ok = j > 0
    next_ok = j < tps - 1
    W = 2 * radius + ATT_Q
    row = lax.broadcasted_iota(jnp.int32, (ATT_Q, W), 0)
    col = lax.broadcasted_iota(jnp.int32, (ATT_Q, W), 1)
    rel = col - radius - row
    ok = (jnp.abs(rel) <= radius) & ((col >= radius) | prev_ok) & ((col < radius + ATT_Q) | next_ok)
    bias = jnp.where(ok, 0.0, NEG_INF).astype(F32)

    k_all = jnp.concatenate([kp_ref[...], kc_ref[...], kn_ref[...]], axis=0)
    v_all = jnp.concatenate([vp_ref[...], vc_ref[...], vn_ref[...]], axis=0)
    nk = k_all.shape[1] // LANES
    lane_lo = lax.broadcasted_iota(jnp.int32, (1, LANES), 1) < HEAD_DIM
    bias_m = jnp.concatenate([bias] * q_per_k, axis=0) if q_per_k > 1 else bias
    M = ATT_Q * q_per_k
    zero = jnp.zeros((), BF16)

    for kc in range(nk):
        ksl = k_all[:, kc * LANES:(kc + 1) * LANES]
        vsl = v_all[:, kc * LANES:(kc + 1) * LANES]
        slabs = [kc * q_per_k + t for t in range(q_per_k)]
        qm = jnp.concatenate([q_ref[:, m * LANES:(m + 1) * LANES] for m in slabs], axis=0) \
            if q_per_k > 1 else q_ref[:, kc * LANES:(kc + 1) * LANES]
        o_acc = jnp.zeros((M, LANES), F32)
        lse_acc = jnp.zeros((M, LANES), F32)
        for half in range(2):
            keep = lane_lo if half == 0 else jnp.logical_not(lane_lo)
            kx = jnp.where(keep, ksl, zero)
            vx = jnp.where(keep, vsl, zero)
            s = lax.dot_general(qm, kx, (((1,), (1,)), ((), ())), preferred_element_type=F32) + bias_m
            mx = jnp.max(s, axis=1, keepdims=True)
            if has_sink:
                sk = jnp.concatenate(
                    [jnp.full((ATT_Q, 1), sink_ref[2 * m + half], F32) for m in slabs], axis=0)
                mx = jnp.maximum(mx, sk)
            p = jnp.exp(s - mx)
            l = jnp.sum(p, axis=1, keepdims=True)
            if has_sink:
                l = l + jnp.exp(sk - mx)
            pv = jnp.dot(p.astype(BF16), vx, preferred_element_type=F32)
            o_acc = o_acc + pv / l
            if want_lse:
                lse_acc = jnp.where(keep, mx + jnp.log(l), lse_acc)
        for t, m in enumerate(slabs):
            if want_lse:
                _slab_write(o_ref, o_acc[t * ATT_Q:(t + 1) * ATT_Q], first_chunk=m)
                _slab_write(o_ref, lse_acc[t * ATT_Q:(t + 1) * ATT_Q], first_chunk=n_q_slabs + m)
            else:
                o_ref[:, m * LANES:(m + 1) * LANES] = o_acc[t * ATT_Q:(t + 1) * ATT_Q].astype(o_ref.dtype)


def _band_attn(q, k, v, segs, *, dil, radius, q_per_k, sink=None, want_lse=False):
    T, wq = q.shape
    wk = k.shape[1]
    (n_p, s_p), (n_s, s_s) = segs
    assert ATT_Q % radius == 0
    hb = ATT_Q // radius
    n_halo = T // radius
    nt = T // ATT_Q
    l_p, l_s = s_p // dil, s_s // dil
    assert l_p % ATT_Q == 0 and l_s % ATT_Q == 0
    p_tiles = (n_p * s_p) // ATT_Q

    def out_map(i):
        in_p = i < p_tiles
        nb = jnp.where(in_p, l_p // ATT_Q, l_s // ATT_Q)
        i_loc = jnp.where(in_p, i, i - p_tiles)
        n = i_loc // nb
        jb = lax.rem(i_loc, nb)
        b = n // dil
        r = lax.rem(n, dil)
        base = jnp.where(in_p, 0, p_tiles // dil)
        return (base + b * nb + jb, r, 0)

    in_specs = []
    args = []
    if sink is not None:
        in_specs.append(pl.BlockSpec(memory_space=pltpu.SMEM))
        args.append(sink)
    in_specs.append(pl.BlockSpec((ATT_Q, wq), lambda i: (i, 0)))
    args.append(q)
    for arr in (k, v):
        in_specs += [
            pl.BlockSpec((radius, wk), lambda i: (jnp.maximum(i * hb - 1, 0), 0)),
            pl.BlockSpec((ATT_Q, wk), lambda i: (i, 0)),
            pl.BlockSpec((radius, wk), lambda i: (jnp.minimum((i + 1) * hb, n_halo - 1), 0)),
        ]
        args += [arr, arr, arr]
    if want_lse:
        assert 2 * wq // LANES == SLAB
        if dil == 1:
            out_shape = jax.ShapeDtypeStruct((T * SLAB, LANES), F32)
            out_specs = pl.BlockSpec((ATT_Q * SLAB, LANES), lambda i: (i, 0))
        else:
            out_shape = jax.ShapeDtypeStruct((T // dil, dil * SLAB, LANES), F32)
            out_specs = pl.BlockSpec((ATT_Q, SLAB, LANES), out_map)
    else:
        assert dil == 1
        out_shape = jax.ShapeDtypeStruct((T, wq), BF16)
        out_specs = pl.BlockSpec((ATT_Q, wq), lambda i: (i, 0))
    kern = functools.partial(_band_attn_kernel, segs=segs, dil=dil, radius=radius, q_per_k=q_per_k,
                             has_sink=sink is not None, want_lse=want_lse)
    out = pl.pallas_call(
        kern, grid=(nt,), in_specs=in_specs, out_specs=out_specs, out_shape=out_shape,
        compiler_params=_cparams(("parallel",)),
    )(*args)
    return out.reshape(T * SLAB, LANES) if want_lse else out


def _split3(v):
    hi = v.astype(BF16)
    r1 = v - hi.astype(F32)
    mid = r1.astype(BF16)
    lo = (r1 - mid.astype(F32)).astype(BF16)
    return hi, mid, lo


def _route(x1, wr_ref, br_ref):
    hi, mid, _ = _split3(x1)
    xs = jnp.concatenate([hi, hi, mid], axis=1)
    lt = lax.dot_general(wr_ref[...], xs, (((1,), (1,)), ((), ())), preferred_element_type=F32) + br_ref[...]
    n_tok = x1.shape[0]
    row = lax.broadcasted_iota(jnp.int32, (MOE_EPG, n_tok), 0)
    big = jnp.int32(MOE_EPG)

    def first_argmax(v):
        m = jnp.max(v, axis=0, keepdims=True)
        return m, jnp.min(jnp.where(v == m, row, big), axis=0, keepdims=True)

    n_e = MOE_GROUPS * MOE_EPG
    gl = lt[n_e:n_e + MOE_EPG]
    gmax, gidx = first_argmax(gl)
    g_w = 1.0 / jnp.sum(jnp.exp(gl - gmax), axis=0, keepdims=True)
    el = lt[(MOE_GROUPS - 1) * MOE_EPG:n_e]
    for g in range(MOE_GROUPS - 2, -1, -1):
        el = jnp.where(gidx == g, lt[g * MOE_EPG:(g + 1) * MOE_EPG], el)
    v1, i1 = first_argmax(el)
    el2 = jnp.where(row == i1, -jnp.inf, el)
    v2, i2 = first_argmax(el2)
    ex = jnp.exp(v2 - v1)
    w1 = 1.0 / (1.0 + ex)
    w2 = ex * w1
    first_lo = i1 < i2
    e_lo = jnp.minimum(i1, i2).astype(F32)
    e_hi = jnp.maximum(i1, i2).astype(F32)
    c_lo = g_w * jnp.where(first_lo, w1, w2)
    c_hi = g_w * jnp.where(first_lo, w2, w1)
    pad = jnp.zeros((ROUTE_ROWS - 5, n_tok), F32)
    return jnp.concatenate([gidx.astype(F32), e_lo, e_hi, c_lo, c_hi, pad], axis=0)


def _rows_to_lanes(rows, sel):
    pieces = jnp.concatenate(_split3(rows), axis=0)
    return lax.dot_general(pieces, sel, (((0,), (0,)), ((), ())), preferred_element_type=F32)


def _gelu_tanh(x):
    return 0.5 * x * (1.0 + jnp.tanh(0.7978845608028654 * (x + 0.044715 * x * x * x)))


def _silu(x):
    return x / (1.0 + jnp.exp(-x))


def _out_ln_kernel(*refs, mode, n_h):
    h_refs = refs[:n_h]
    extra_ref = refs[n_h] if mode == "gla" else None
    base = n_h + (1 if mode == "gla" else 0)
    x_ref, wo_ref, g_ref, b_ref, wr_ref, br_ref, x1_ref, rt_ref = refs[base:base + 8]

    if mode == "plain":
        hb = h_refs[0][...]
    elif mode == "rglru":
        gate, hf, hbw = h_refs
        hb = (_gelu_tanh(gate[...].astype(F32)) * (hf[...].astype(F32) + hbw[...].astype(F32))).astype(BF16)
    elif mode == "gla":
        of, ob, gg = h_refs
        o = of[...] + ob[...]
        parts = []
        for h in range(GLA_HEADS):
            oh = o[:, h * GLA_DVH:(h + 1) * GLA_DVH]
            ms = jnp.mean(oh * oh, axis=-1, keepdims=True)
            parts.append(oh * lax.rsqrt(ms + LN_EPS) * extra_ref[...])
        o = jnp.concatenate(parts, axis=1)
        hb = (o * _silu(gg[...].astype(F32))).astype(BF16)
    else:
        half = SLAB // 2
        os_ = [_slab_read(r, range(half)) for r in h_refs]
        ls_ = [_slab_read(r, range(half, SLAB)) for r in h_refs]
        mx = functools.reduce(jnp.maximum, ls_)
        es = [jnp.exp(l - mx) for l in ls_]
        den = functools.reduce(jnp.add, es)
        o = functools.reduce(jnp.add, [(e / den) * ov for e, ov in zip(es, os_)])
        hb = o.astype(BF16)

    acc = jnp.dot(hb, wo_ref[...], preferred_element_type=F32)
    x1 = _layer_norm(ALPHA * _slab_read(x_ref) + acc, g_ref[...], b_ref[...])
    _slab_write(x1_ref, x1)
    rt_ref[...] = _route(x1, wr_ref, br_ref)


def _out_ln(hs, x, T, wo, ln_g, ln_b, wr, br, *, mode, extra=None):
    D = D_MODEL
    tm = TOK_TILE
    rows = lambda h: tm * SLAB if mode == "dil" else tm
    in_specs = [pl.BlockSpec((rows(h), h.shape[1]), lambda i: (i, 0)) for h in hs]
    args = list(hs)
    if mode == "gla":
        in_specs.append(pl.BlockSpec((1, extra.shape[1]), lambda i: (0, 0)))
        args.append(extra)
    in_specs += [
        pl.BlockSpec((tm * SLAB, LANES), lambda i: (i, 0)),
        pl.BlockSpec(wo.shape, lambda i: (0, 0)),
        pl.BlockSpec((1, D), lambda i: (0, 0)),
        pl.BlockSpec((1, D), lambda i: (0, 0)),
        pl.BlockSpec(wr.shape, lambda i: (0, 0)),
        pl.BlockSpec(br.shape, lambda i: (0, 0)),
    ]
    args += [x, wo, ln_g, ln_b, wr, br]
    return pl.pallas_call(
        functools.partial(_out_ln_kernel, mode=mode, n_h=len(hs)),
        grid=(T // tm,),
        in_specs=in_specs,
        out_specs=[pl.BlockSpec((tm * SLAB, LANES), lambda i: (i, 0)),
                   pl.BlockSpec((ROUTE_ROWS, tm), lambda i: (0, i))],
        out_shape=[jax.ShapeDtypeStruct((T * SLAB, LANES), F32),
                   jax.ShapeDtypeStruct((ROUTE_ROWS, T), F32)],
        compiler_params=_cparams(("parallel",)),
    )(*args)


def _moe_kernel(tg_ref, tlo_ref, thi_ref, nv_ref, idx_ref, x_hbm, cw_ref, sel_ref, wgl_ref, wul_ref, wdl_ref,
                wgh_ref, wuh_ref, wdh_ref, g_ref, b_ref, out_hbm, xbuf, obuf, gsem, ssem, *, nt, n_tok):
    i = pl.program_id(0)
    tm = xbuf.shape[1] // SLAB
    slot = lax.rem(i, 2)
    used = nv_ref[i] > 0
    nxt = jnp.minimum(i + 1, nt - 1)
    next_used = jnp.logical_and(i + 1 < nt, nv_ref[nxt] > 0)

    def token(ref, tok):
        return ref.at[pl.ds(pl.multiple_of(tok * SLAB, SLAB), SLAB)]

    def gather_start(tile, sl):
        for r in range(tm):
            tok = idx_ref[tile * tm + r]
            pltpu.make_async_copy(token(x_hbm, tok), xbuf.at[sl, pl.ds(r * SLAB, SLAB)], gsem.at[sl]).start()

    def gather_wait(sl):
        pltpu.make_async_copy(x_hbm.at[pl.ds(0, tm * SLAB)], xbuf.at[sl], gsem.at[sl]).wait()

    def scatter_start(tile, sl):
        n = nv_ref[tile]
        dump = n_tok + sl * tm
        for r in range(tm):
            tok = jnp.where(r < n, idx_ref[tile * tm + r], dump + r)
            pltpu.make_async_copy(obuf.at[sl, pl.ds(r * SLAB, SLAB)], token(out_hbm, tok), ssem.at[sl]).start()

    def scatter_wait(sl):
        pltpu.make_async_copy(obuf.at[sl], out_hbm.at[pl.ds(0, tm * SLAB)], ssem.at[sl]).wait()

    @pl.when(i == 0)
    def _():
        obuf[...] = jnp.zeros_like(obuf)
        for sl in range(2):
            cp = pltpu.make_async_copy(obuf.at[sl], out_hbm.at[pl.ds((n_tok + sl * tm) * SLAB, tm * SLAB)],
                                       ssem.at[sl])
            cp.start()
            cp.wait()

    @pl.when(jnp.logical_and(i == 0, used))
    def _():
        gather_start(0, 0)

    @pl.when(used)
    def _():
        gather_wait(slot)

        @pl.when(next_used)
        def _():
            gather_start(i + 1, 1 - slot)

        @pl.when(i >= 2)
        def _():
            scatter_wait(slot)

        x = _slab_read(xbuf.at[slot])
        cw = _rows_to_lanes(cw_ref[...], sel_ref[...])
        xb = x.astype(BF16)
        acc = jnp.zeros((tm, D_MODEL), F32)
        for wg_ref, wu_ref, wd_ref, lane in ((wgl_ref, wul_ref, wdl_ref, 0), (wgh_ref, wuh_ref, wdh_ref, 1)):
            hg = jnp.dot(xb, wg_ref[0, 0], preferred_element_type=F32)
            hu = jnp.dot(xb, wu_ref[0, 0], preferred_element_type=F32)
            h = _silu(hg) * hu * cw[:, lane:lane + 1]
            acc = acc + jnp.dot(h.astype(BF16), wd_ref[0, 0], preferred_element_type=F32)
        _slab_write(obuf.at[slot], _layer_norm(ALPHA * x + acc, g_ref[...], b_ref[...]))
        scatter_start(i, slot)

        @pl.when(jnp.logical_not(next_used))
        def _():
            @pl.when(i >= 1)
            def _():
                scatter_wait(1 - slot)
            scatter_wait(slot)


_PAIR_LO = tuple(a for a in range(MOE_EPG) for b in range(a + 1, MOE_EPG))
_PAIR_HI = tuple(b for a in range(MOE_EPG) for b in range(a + 1, MOE_EPG))
N_PAIRS = len(_PAIR_LO)
N_CLASSES = MOE_GROUPS * N_PAIRS


def _moe_schedule(rt, tm, nt):
    T = rt.shape[1]
    i32 = jnp.int32
    g, lo, hi = rt[0].astype(i32), rt[1].astype(i32), rt[2].astype(i32)
    cls = g * N_PAIRS + lo * (2 * MOE_EPG - 1 - lo) // 2 + (hi - lo - 1)
    cls_sorted, order = lax.sort((cls, jnp.arange(T, dtype=i32)), num_keys=1, is_stable=True)
    starts = jnp.searchsorted(cls_sorted, jnp.arange(N_CLASSES + 1, dtype=i32), side="left").astype(i32)
    counts = starts[1:] - starts[:-1]
    tiles_per = (counts + tm - 1) // tm
    cum = jnp.cumsum(tiles_per)
    n_used = cum[-1]
    t = jnp.arange(nt, dtype=i32)
    tc = jnp.minimum(t, n_used - 1)
    tcls = jnp.searchsorted(cum, tc, side="right").astype(i32)
    k = tc - (cum[tcls] - tiles_per[tcls])
    nvalid = jnp.where(t < n_used, jnp.clip(counts[tcls] - k * tm, 0, tm), 0).astype(i32)
    r = jnp.arange(tm, dtype=i32)
    pos = starts[tcls][:, None] + k[:, None] * tm + jnp.minimum(r[None, :], jnp.maximum(nvalid, 1)[:, None] - 1)
    idx = order[jnp.clip(pos, 0, T - 1)].reshape(-1)
    pr = tcls % N_PAIRS
    tile_lo = jnp.asarray(_PAIR_LO, i32)[pr]
    tile_hi = jnp.asarray(_PAIR_HI, i32)[pr]
    return (tcls // N_PAIRS).astype(i32), tile_lo, tile_hi, nvalid, idx.astype(i32)


def _moe(x1, rt, wg, wu, wd, ln_g, ln_b):
    T = x1.shape[0] // SLAB
    D = D_MODEL
    tm = MOE_TILE
    nt = T // tm + N_CLASSES
    sched = _moe_schedule(rt, tm, nt)
    idx = sched[-1]
    cw = jnp.stack([rt[3][idx], rt[4][idx]], axis=0)
    cw = jnp.concatenate([cw, jnp.zeros((ROUTE_ROWS - 2, nt * tm), F32)], axis=0)
    sel = jnp.tile(jnp.eye(ROUTE_ROWS, LANES, dtype=BF16), (3, 1))
    lo_map = lambda i, tg, tlo, thi, nv, idx: (tg[i], tlo[i], 0, 0)
    hi_map = lambda i, tg, tlo, thi, nv, idx: (tg[i], thi[i], 0, 0)
    const = lambda i, tg, tlo, thi, nv, idx: (0, 0)
    grid_spec = pltpu.PrefetchScalarGridSpec(
        num_scalar_prefetch=5,
        grid=(nt,),
        in_specs=[
            pl.BlockSpec(memory_space=pl.ANY),
            pl.BlockSpec((ROUTE_ROWS, tm), lambda i, tg, tlo, thi, nv, idx: (0, i)),
            pl.BlockSpec(sel.shape, const),
            pl.BlockSpec((1, 1, D, MOE_FF), lo_map),
            pl.BlockSpec((1, 1, D, MOE_FF), lo_map),
            pl.BlockSpec((1, 1, MOE_FF, D), lo_map),
            pl.BlockSpec((1, 1, D, MOE_FF), hi_map),
            pl.BlockSpec((1, 1, D, MOE_FF), hi_map),
            pl.BlockSpec((1, 1, MOE_FF, D), hi_map),
            pl.BlockSpec((1, D), const),
            pl.BlockSpec((1, D), const),
        ],
        out_specs=pl.BlockSpec(memory_space=pl.ANY),
        scratch_shapes=[pltpu.VMEM((2, tm * SLAB, LANES), F32), pltpu.VMEM((2, tm * SLAB, LANES), F32),
                        pltpu.SemaphoreType.DMA((2,)), pltpu.SemaphoreType.DMA((2,))],
    )
    return pl.pallas_call(
        functools.partial(_moe_kernel, nt=nt, n_tok=T),
        grid_spec=grid_spec,
        out_shape=jax.ShapeDtypeStruct(((T + 2 * tm) * SLAB, LANES), F32),
        compiler_params=_cparams(("arbitrary",)),
    )(*sched, x1, cw, sel, wg, wu, wd, wg, wu, wd, ln_g, ln_b)


def _rglru_stream(z, u_ref, up_ref, un_ref, w_ref, cw_ref, cb_ref, ba_ref, bi_ref, lam_ref,
                  h_out_ref, a_s, b_s, h_s, carry_ref, prev_ok, next_ok, reset):
    tt = u_ref.shape[0]
    u_mid = u_ref[...]
    up = jnp.where(prev_ok, up_ref[...], 0.0)
    un = jnp.where(next_ok, un_ref[...], 0.0)
    ext = jnp.concatenate([up, u_mid, un], axis=0)
    left = RG_CONV // 2
    u = cb_ref[...]
    for kk in range(RG_CONV):
        off = SUBLANES - left + kk
        u = u + cw_ref[kk:kk + 1, :] * ext[off:off + tt]
    ub = u.astype(BF16)
    sp = jnp.maximum(-lam_ref[...], 0.0) + jnp.log(1.0 + jnp.exp(-jnp.abs(lam_ref[...])))
    for n in range(RG_NT):
        s0 = min(max(n - 1, 0), RG_NT - 3) * LANES
        zz = jnp.dot(ub[:, s0:s0 + RG_BAND], w_ref[n], preferred_element_type=F32)
        sl = slice(n * LANES, (n + 1) * LANES)
        r = jax.nn.sigmoid(zz[:, :LANES] + ba_ref[:, sl])
        ig = jax.nn.sigmoid(zz[:, LANES:] + bi_ref[:, sl])
        log_a = -RG_C * r * sp[:, sl]
        a = jnp.exp(log_a)
        a_s[:, sl] = a
        b_s[:, sl] = jnp.sqrt(-jnp.tanh(log_a) * (a * a + 1.0)) * ig * u[:, sl]

    @pl.when(reset)
    def _():
        carry_ref[...] = jnp.zeros_like(carry_ref)

    n_grp = tt // SUBLANES

    def body(gi, h):
        g = gi if z == 0 else n_grp - 1 - gi
        base = pl.multiple_of(g * SUBLANES, SUBLANES)
        a8 = a_s[pl.ds(base, SUBLANES), :]
        b8 = b_s[pl.ds(base, SUBLANES), :]
        rows = [None] * SUBLANES
        order = range(SUBLANES) if z == 0 else range(SUBLANES - 1, -1, -1)
        for r_ in order:
            h = a8[r_:r_ + 1] * h + b8[r_:r_ + 1]
            rows[r_] = h
        h_s[pl.ds(base, SUBLANES), :] = jnp.concatenate(rows, axis=0)
        return h

    h_last = lax.fori_loop(0, n_grp, body, carry_ref[...])
    carry_ref[...] = h_last
    h_out_ref[...] = h_s[...].astype(h_out_ref.dtype)


def _rglru_scan_kernel(uf_ref, ufp_ref, ufn_ref, ubk_ref, ubp_ref, ubn_ref, wf_ref, wb_ref, cw_ref, cb_ref,
                       ba_ref, bi_ref, lam_ref, hf_ref, hb_ref, a_s, b_s, h_s, cf_ref, cbk_ref, *, segs, nt):
    i = pl.program_id(0)
    tt = uf_ref.shape[0]
    _, tps, j = _seq_pos(i, tt, segs)
    _rglru_stream(0, uf_ref, ufp_ref, ufn_ref, wf_ref, cw_ref, cb_ref, ba_ref.at[0:1], bi_ref.at[0:1],
                  lam_ref.at[0:1], hf_ref, a_s, b_s, h_s, cf_ref, j > 0, j < tps - 1, j == 0)
    ib = nt - 1 - i
    _, tps_b, jb = _seq_pos(ib, tt, segs)
    _rglru_stream(1, ubk_ref, ubp_ref, ubn_ref, wb_ref, cw_ref, cb_ref, ba_ref.at[1:2], bi_ref.at[1:2],
                  lam_ref.at[1:2], hb_ref, a_s, b_s, h_s, cbk_ref, jb > 0, jb < tps_b - 1, jb == tps_b - 1)


def _rglru_scan(u_pre, w_band, conv_w, conv_b, ba, bi, lam, segs):
    T, W = u_pre.shape
    tt = SCAN_TILE
    nt = T // tt
    hpt = tt // SUBLANES
    n_h = T // SUBLANES
    fwd = lambda i: (i, 0)
    bwd = lambda i: (nt - 1 - i, 0)

    def halo_specs(idx):
        return [
            pl.BlockSpec((tt, W), lambda i: (idx(i), 0)),
            pl.BlockSpec((SUBLANES, W), lambda i: (jnp.maximum(idx(i) * hpt - 1, 0), 0)),
            pl.BlockSpec((SUBLANES, W), lambda i: (jnp.minimum((idx(i) + 1) * hpt, n_h - 1), 0)),
        ]

    const2 = lambda i: (0, 0)
    const3 = lambda i: (0, 0, 0)
    in_specs = halo_specs(lambda i: i) + halo_specs(lambda i: nt - 1 - i) + [
        pl.BlockSpec(w_band.shape[1:], const3),
        pl.BlockSpec(w_band.shape[1:], const3),
        pl.BlockSpec(conv_w.shape, const2),
        pl.BlockSpec(conv_b.shape, const2),
        pl.BlockSpec(ba.shape, const2),
        pl.BlockSpec(bi.shape, const2),
        pl.BlockSpec(lam.shape, const2),
    ]
    return pl.pallas_call(
        functools.partial(_rglru_scan_kernel, segs=segs, nt=nt),
        grid=(nt,),
        in_specs=in_specs,
        out_specs=[pl.BlockSpec((tt, W), fwd), pl.BlockSpec((tt, W), bwd)],
        out_shape=[jax.ShapeDtypeStruct((T, W), BF16), jax.ShapeDtypeStruct((T, W), BF16)],
        scratch_shapes=[pltpu.VMEM((tt, W), F32), pltpu.VMEM((tt, W), F32), pltpu.VMEM((tt, W), F32),
                        pltpu.VMEM((1, W), F32), pltpu.VMEM((1, W), F32)],
        compiler_params=_cparams(("arbitrary",)),
    )(u_pre, u_pre, u_pre, u_pre, u_pre, u_pre, w_band[0], w_band[1], conv_w, conv_b, ba, bi, lam)


def _rglru_band_weights(wa, wi):
    def dense(w):
        eye = jnp.eye(RG_BLOCKS, dtype=w.dtype)
        return jnp.einsum("ncd,nm->ncmd", w, eye).reshape(RG_WIDTH, RG_WIDTH)

    out = []
    for z in range(2):
        da, di = dense(wa[z]), dense(wi[z])
        tiles = []
        for n in range(RG_NT):
            s0 = min(max(n - 1, 0), RG_NT - 3) * LANES
            sl = slice(n * LANES, (n + 1) * LANES)
            tiles.append(jnp.concatenate([da[s0:s0 + RG_BAND, sl], di[s0:s0 + RG_BAND, sl]], axis=1))
        out.append(jnp.stack(tiles))
    return jnp.stack(out).astype(BF16)


def _gla_stream(z, q_ref, k_ref, v_ref, lr_ref, wa2_ref, ba_ref, o_ref, st_ref, reset):
    tt = q_ref.shape[0]
    C = GLA_CHUNK

    @pl.when(reset)
    def _():
        st_ref[z] = jnp.zeros(st_ref.shape[1:], F32)

    zz = jnp.dot(lr_ref[...].astype(BF16), wa2_ref[:, z * GLA_DK:(z + 1) * GLA_DK],
                 preferred_element_type=F32) + ba_ref[:, z * GLA_DK:(z + 1) * GLA_DK]
    log_a = -(jnp.maximum(-zz, 0.0) + jnp.log(1.0 + jnp.exp(-jnp.abs(zz)))) / GLA_TAU
    ri = lax.broadcasted_iota(jnp.int32, (C, C), 0)
    ci = lax.broadcasted_iota(jnp.int32, (C, C), 1)
    causal = (ri >= ci) if z == 0 else (ri <= ci)
    tri = causal.astype(F32)
    mid = C // 2 if z == 0 else C - 1 - C // 2
    last = C - 1 if z == 0 else 0
    scale = GLA_DKH ** -0.5
    chunks = range(tt // C) if z == 0 else range(tt // C - 1, -1, -1)
    for c in chunks:
        rs = slice(c * C, (c + 1) * C)
        b = jnp.dot(tri, log_a[rs], preferred_element_type=F32, precision=lax.Precision.HIGHEST)
        b_mid = b[mid:mid + 1]
        b_last = b[last:last + 1]
        qc = q_ref[rs, :] * scale
        kc = k_ref[rs, :]
        qd = (qc * jnp.exp(b - b_mid)).astype(BF16)
        kd = (kc * jnp.exp(b_mid - b)).astype(BF16)
        ks = (kc * jnp.exp(b_last - b)).astype(BF16)
        qb = (qc * jnp.exp(b)).astype(BF16)
        dec = jnp.exp(b_last)
        for h in range(GLA_HEADS):
            ksl = slice(h * GLA_DKH, (h + 1) * GLA_DKH)
            vsl = slice(h * GLA_DVH, (h + 1) * GLA_DVH)
            vc = v_ref[rs, vsl]
            att = lax.dot_general(qd[:, ksl], kd[:, ksl], (((1,), (1,)), ((), ())), preferred_element_type=F32)
            att = jnp.where(causal, att, 0.0)
            o = jnp.dot(att.astype(BF16), vc, preferred_element_type=F32)
            st = st_ref[z, h]
            o = o + lax.dot_general(qb[:, ksl], st.astype(BF16), (((1,), (1,)), ((), ())),
                                    preferred_element_type=F32)
            o_ref[rs, vsl] = o
            upd = lax.dot_general(vc, ks[:, ksl], (((0,), (0,)), ((), ())), preferred_element_type=F32)
            st_ref[z, h] = st * dec[:, ksl] + upd


def _gla_kernel(qf, kf, vf, lf, qb, kb, vb, lb, wa2_ref, ba_ref, of_ref, ob_ref, st_ref, *, segs, nt):
    i = pl.program_id(0)
    tt = qf.shape[0]
    _, tps, j = _seq_pos(i, tt, segs)
    _gla_stream(0, qf, kf, vf, lf, wa2_ref, ba_ref, of_ref, st_ref, j == 0)
    ib = nt - 1 - i
    _, tps_b, jb = _seq_pos(ib, tt, segs)
    _gla_stream(1, qb, kb, vb, lb, wa2_ref, ba_ref, ob_ref, st_ref, jb == tps_b - 1)


def _gla(q, k, v, lr, wa2, ba, segs):
    T = q.shape[0]
    tt = SCAN_TILE
    nt = T // tt
    fwd = lambda i: (i, 0)
    bwd = lambda i: (nt - 1 - i, 0)
    in_specs = []
    for idx in (fwd, bwd):
        in_specs += [pl.BlockSpec((tt, GLA_DK), idx), pl.BlockSpec((tt, GLA_DK), idx),
                     pl.BlockSpec((tt, GLA_DV), idx), pl.BlockSpec((tt, LANES), idx)]
    in_specs += [pl.BlockSpec(wa2.shape, lambda i: (0, 0)), pl.BlockSpec(ba.shape, lambda i: (0, 0))]
    return pl.pallas_call(
        functools.partial(_gla_kernel, segs=segs, nt=nt),
        grid=(nt,),
        in_specs=in_specs,
        out_specs=[pl.BlockSpec((tt, GLA_DV), fwd), pl.BlockSpec((tt, GLA_DV), bwd)],
        out_shape=[jax.ShapeDtypeStruct((T, GLA_DV), F32), jax.ShapeDtypeStruct((T, GLA_DV), F32)],
        scratch_shapes=[pltpu.VMEM((2, GLA_HEADS, GLA_DVH, GLA_DKH), F32)],
        compiler_params=_cparams(("arbitrary",)),
    )(q, k, v, lr, q, k, v, lr, wa2, ba)


def _rope_tables(s_max):
    half = HEAD_DIM // 2
    inv = ROPE_THETA ** (-jnp.arange(half, dtype=F32) / half)
    ang = jnp.arange(s_max, dtype=F32)[:, None] * inv[None, :]
    cos, sin = jnp.cos(ang), jnp.sin(ang)
    cos_t = jnp.tile(jnp.concatenate([cos, cos], axis=1), (1, LANES // HEAD_DIM))
    sin_t = jnp.tile(jnp.concatenate([-sin, sin], axis=1), (1, LANES // HEAD_DIM))
    return cos_t, sin_t


def _router_weights(wgr, bgr, wer, ber):
    n_e = MOE_GROUPS * MOE_EPG
    w = jnp.concatenate([jnp.transpose(wer, (0, 2, 1)).reshape(n_e, D_MODEL), wgr.T], axis=0)
    b = jnp.concatenate([ber.reshape(n_e), bgr])
    n_pad = ROUTER_ROWS - w.shape[0]
    w = jnp.pad(w, ((0, n_pad), (0, 0)))
    b = jnp.concatenate([b, jnp.full((n_pad,), NEG_INF, F32)])
    w_hi = w.astype(BF16)
    w_mid = (w - w_hi.astype(F32)).astype(BF16)
    return jnp.concatenate([w_hi, w_mid, w_hi], axis=1), b[:, None]


def _mixer(kind, jl, x, segs, p, cos_t, sin_t, lg1, lb1, wr, br):
    D = D_MODEL
    T = sum(n * s for n, s in segs)
    if kind == 0:
        nq = A_HEADS * HEAD_DIM
        nkv = A_KV_HEADS * HEAD_DIM
        w = p["a_wqkv"][jl]
        wq, wk, wv = w[:, :nq], w[:, nq:nq + nkv], w[:, nq + nkv:]
        dup = lambda t: jnp.repeat(t.reshape(D, A_KV_HEADS, 1, HEAD_DIM), 2, axis=2).reshape(D, 2 * nkv)
        w_ext = jnp.concatenate([wq, dup(wk), dup(wv)], axis=1).astype(BF16)
        q, kd, vd = _proj_rope(x, T, w_ext, cos_t, sin_t, segs, dil=1, n_rope=(nq + 2 * nkv) // LANES,
                               scale_cols=nq, scale=HEAD_DIM ** -0.5, out_cols=(nq, 2 * nkv, 2 * nkv),
                               out_dtypes=(BF16, BF16, BF16))
        o = _band_attn(q, kd, vd, segs, dil=1, radius=A_RADIUS, q_per_k=2, sink=p["a_sink"][jl])
        return _out_ln([o], x, T, p["a_wo"][jl].astype(BF16), lg1, lb1, wr, br, mode="plain")
    if kind == 1:
        gate, u_pre = _proj(x, T, p["b_win"][jl].astype(BF16), (RG_WIDTH, RG_WIDTH), (BF16, F32))
        w_band = _rglru_band_weights(p["b_wa"][jl], p["b_wi"][jl])
        hf, hb = _rglru_scan(u_pre, w_band, p["b_conv_w"][jl], p["b_conv_b"][jl][None, :], p["b_ba"][jl],
                             p["b_bi"][jl], p["b_lam"][jl], segs)
        return _out_ln([gate, hf, hb], x, T, p["b_wo"][jl].astype(BF16), lg1, lb1, wr, br, mode="rglru")
    if kind == 2:
        wa1 = jnp.concatenate([p["c_wa1"][jl][0], p["c_wa1"][jl][1]], axis=1)
        wa1 = jnp.pad(wa1, ((0, 0), (0, LANES - wa1.shape[1])))
        w_all = jnp.concatenate([p["c_wqkvg"][jl], wa1], axis=1).astype(BF16)
        q, k, v, g, lr = _proj(x, T, w_all, (GLA_DK, GLA_DK, GLA_DV, GLA_DV, LANES), (F32, F32, BF16, BF16, F32))
        wa2 = jnp.zeros((LANES, 2 * GLA_DK), F32)
        wa2 = wa2.at[:GLA_RANK, :GLA_DK].set(p["c_wa2"][jl][0])
        wa2 = wa2.at[GLA_RANK:2 * GLA_RANK, GLA_DK:].set(p["c_wa2"][jl][1])
        of, ob = _gla(q, k, v, lr, wa2.astype(BF16), p["c_ba"][jl].reshape(1, 2 * GLA_DK), segs)
        return _out_ln([of, ob, g], x, T, p["c_wo"][jl].astype(BF16), lg1, lb1, wr, br, mode="gla",
                       extra=p["c_norm_g"][jl][None, :])
    ols = []
    for gi, (window, dil) in enumerate(DIL_GROUPS):
        w_g = p["d_wqkv"][jl][:, gi * 3 * DIL_WIDTH:(gi + 1) * 3 * DIL_WIDTH].astype(BF16)
        q, k, v = _proj_rope(x, T, w_g, cos_t, sin_t, segs, dil=dil, n_rope=2 * DIL_WIDTH // LANES,
                             scale_cols=DIL_WIDTH, scale=HEAD_DIM ** -0.5,
                             out_cols=(DIL_WIDTH,) * 3, out_dtypes=(BF16,) * 3)
        ols.append(_band_attn(q, k, v, segs, dil=dil, radius=window // (2 * dil), q_per_k=1, want_lse=True))
    return _out_ln(ols, x, T, p["d_wo"][jl].astype(BF16), lg1, lb1, wr, br, mode="dil")


def kernel(x_prompt, x_sample, ln_g, ln_b, a_wqkv, a_sink, a_wo, b_win, b_conv_w, b_conv_b, b_wa, b_ba, b_wi,
           b_bi, b_lam, b_wo, c_wqkvg, c_wa1, c_wa2, c_ba, c_norm_g, c_wo, d_wqkv, d_wo, m_wgr, m_bgr, m_wer,
           m_ber, m_wg, m_wu, m_wd):
    p = dict(a_wqkv=a_wqkv, a_sink=a_sink, a_wo=a_wo, b_win=b_win, b_conv_w=b_conv_w, b_conv_b=b_conv_b,
             b_wa=b_wa, b_ba=b_ba, b_wi=b_wi, b_bi=b_bi, b_lam=b_lam, b_wo=b_wo, c_wqkvg=c_wqkvg, c_wa1=c_wa1,
             c_wa2=c_wa2, c_ba=c_ba, c_norm_g=c_norm_g, c_wo=c_wo, d_wqkv=d_wqkv, d_wo=d_wo)
    n_p, s_p, D = x_prompt.shape
    n_s, s_s, _ = x_sample.shape
    segs = ((n_p, s_p), (n_s, s_s))
    assert (n_p * s_p) % s_s == 0
    t_p, T = n_p * s_p, n_p * s_p + n_s * s_s
    x = jnp.concatenate([x_prompt.reshape(t_p, D), x_sample.reshape(T - t_p, D)], axis=0).reshape(T * SLAB, LANES)
    cos_t, sin_t = _rope_tables(max(s_p, s_s))
    wg_b, wu_b, wd_b = m_wg.astype(BF16), m_wu.astype(BF16), m_wd.astype(BF16)

    for layer in range(DEPTH):
        lg1, lb1 = ln_g[layer, 0][None, :], ln_b[layer, 0][None, :]
        lg2, lb2 = ln_g[layer, 1][None, :], ln_b[layer, 1][None, :]
        wr, br = _router_weights(m_wgr[layer], m_bgr[layer], m_wer[layer], m_ber[layer])
        x1, rt = _mixer(layer % 4, layer // 4, x, segs, p, cos_t, sin_t, lg1, lb1, wr, br)
        x = _moe(x1, rt, wg_b[layer], wu_b[layer], wd_b[layer], lg2, lb2)

    return x[:t_p * SLAB].reshape(n_p, s_p, D), x[t_p * SLAB:T * SLAB].reshape(n_s, s_s, D)
```

```python
import functools

import jax
import jax.numpy as jnp
from jax import lax
from jax.experimental import pallas as pl
from jax.experimental.pallas import tpu as pltpu

F32 = jnp.float32
BF16 = jnp.bfloat16

D_MODEL = 1024
HEAD_DIM = 64
ROPE_THETA = 10000.0
A_HEADS = 16
A_KV_HEADS = 4
A_RADIUS = 128
RG_WIDTH = 1408
RG_BLOCKS = 16
RG_BW = RG_WIDTH // RG_BLOCKS
RG_CONV = 4
RG_C = 8.0
GLA_HEADS = 4
GLA_DK = 512
GLA_DV = 1024
GLA_DKH = 128
GLA_DVH = 256
GLA_RANK = 16
GLA_TAU = 16.0
GLA_CHUNK = 64
DIL_GROUPS = ((128, 1), (512, 4), (2048, 16))
N_DIL = 3
DIL_HEADS = 8
DIL_WIDTH = 512
MOE_GROUPS = 4
MOE_EPG = 8
MOE_FF = 256
DEPTH = 4
ALPHA = (2 * DEPTH) ** 0.25
LN_EPS = 1e-5
NEG_INF = -1e30

LANES = 128
SUBLANES = 8
VMEM_LIMIT = 52 * 1024 * 1024
ATT_Q = 128
TOK_TILE = 512
SCAN_TILE = 256
RG_NT = RG_WIDTH // LANES
RG_BAND = 3 * LANES
ROUTER_ROWS = 40
ROUTE_ROWS = 16
MOE_TILE = 256
SLAB = D_MODEL // LANES
PROJ_CHUNK = 512


def _cparams(sem):
    return pltpu.CompilerParams(dimension_semantics=sem, vmem_limit_bytes=VMEM_LIMIT)


def _seq_pos(i, rows, segs, dil=1):
    (n_p, s_p), (_, s_s) = segs
    p_tiles = (n_p * s_p) // rows
    in_p = i < p_tiles
    tps = jnp.where(in_p, (s_p // dil) // rows, (s_s // dil) // rows)
    j = lax.rem(i, tps)
    return in_p, tps, j


def _layer_norm(y, g, b):
    mu = jnp.mean(y, axis=-1, keepdims=True)
    yc = y - mu
    var = jnp.mean(yc * yc, axis=-1, keepdims=True)
    return yc * lax.rsqrt(var + LN_EPS) * g + b


def _rope_slab(t, cos, sin_signed):
    lane = lax.broadcasted_iota(jnp.int32, t.shape, 1)
    first_half = (lane % HEAD_DIM) < (HEAD_DIM // 2)
    partner = jnp.where(first_half, pltpu.roll(t, LANES - HEAD_DIM // 2, 1), pltpu.roll(t, HEAD_DIM // 2, 1))
    return t * cos + partner * sin_signed


def _slab_read(ref, chunks=range(SLAB)):
    if len(ref.shape) == 3:
        return jnp.concatenate([ref[:, c, :] for c in chunks], axis=1)
    rows = ref.shape[0] // SLAB
    return jnp.concatenate([ref[pl.ds(c, rows, stride=SLAB), :] for c in chunks], axis=1)


def _slab_write(ref, val, first_chunk=0):
    for c in range(val.shape[1] // LANES):
        v = val[:, c * LANES:(c + 1) * LANES]
        if len(ref.shape) == 3:
            ref[:, first_chunk + c, :] = v
        else:
            ref[pl.ds(first_chunk + c, ref.shape[0] // SLAB, stride=SLAB), :] = v


def _col_chunks(width):
    out, c = [], 0
    while c < width:
        w = min(PROJ_CHUNK, width - c)
        out.append((c, w))
        c += w
    return out


def _x_specs(x, tm):
    if isinstance(x, tuple):
        xp, xs = x
        p_tiles = xp.shape[0] // tm
        specs = [pl.BlockSpec((tm, D_MODEL), lambda i, *_: (jnp.minimum(i, p_tiles - 1), 0)),
                 pl.BlockSpec((tm, D_MODEL), lambda i, *_: (jnp.maximum(i - p_tiles, 0), 0))]
        return specs, [xp, xs], p_tiles
    return [pl.BlockSpec((tm * SLAB, LANES), lambda i, *_: (i, 0))], [x], 0


def _x_read(x_refs, p_tiles):
    if len(x_refs) == 2:
        return jnp.where(pl.program_id(0) < p_tiles, x_refs[0][...], x_refs[1][...])
    return _slab_read(x_refs[0])


def _proj_rope_kernel(*refs, n_x, p_tiles, n_rope, scale_cols, scale, out_cols):
    x_refs, (w_ref, cos_ref, sin_ref), out_refs = refs[:n_x], refs[n_x:n_x + 3], refs[n_x + 3:]
    xb = _x_read(x_refs, p_tiles).astype(BF16)
    cos = cos_ref[...]
    sin = sin_ref[...]
    col = 0
    for o_ref, width in zip(out_refs, out_cols):
        for c0, cw in _col_chunks(width):
            yc = jnp.dot(xb, w_ref[:, col + c0:col + c0 + cw], preferred_element_type=F32)
            for c in range(cw // LANES):
                slab = (col + c0) // LANES + c
                y = yc[:, c * LANES:(c + 1) * LANES]
                if slab < n_rope:
                    y = _rope_slab(y, cos, sin)
                if slab * LANES < scale_cols:
                    y = y * scale
                o_ref[:, c0 + c * LANES:c0 + (c + 1) * LANES] = y.astype(o_ref.dtype)
        col += width


def _proj_rope(x, T, w, cos_tab, sin_tab, segs, *, dil, n_rope, scale_cols, scale, out_cols, out_dtypes):
    D = D_MODEL
    (n_p, s_p), (n_s, s_s) = segs
    l_p, l_s = s_p // dil, s_s // dil
    tj = min(TOK_TILE, l_s, l_p)
    assert l_p % tj == 0 and l_s % tj == 0 and (n_p * s_p) % (dil * tj) == 0
    if dil == 1:
        x_specs, x_args, x_p_tiles = _x_specs(x, tj)
    else:
        assert x.shape[0] % (dil * SLAB) == 0
        x_args = [x.reshape(x.shape[0] // (dil * SLAB), dil * SLAB, LANES)]
        x_specs, x_p_tiles = [pl.BlockSpec((tj, SLAB, LANES), lambda jt, r: (jt, r, 0))], 0
    cosv = cos_tab.reshape(cos_tab.shape[0] // dil, dil * LANES)
    sinv = sin_tab.reshape(sin_tab.shape[0] // dil, dil * LANES)
    p_tiles = (n_p * l_p) // tj

    def decode(jt):
        in_p = jt < p_tiles
        lt = jnp.where(in_p, l_p // tj, l_s // tj)
        jt_loc = jnp.where(in_p, jt, jt - p_tiles)
        b = jt_loc // lt
        j0 = lax.rem(jt_loc, lt)
        return in_p, lt, b, j0

    def out_map(jt, r):
        in_p, lt, b, j0 = decode(jt)
        base = jnp.where(in_p, 0, (n_p * s_p) // tj)
        return (base + b * (lt * dil) + r * lt + j0, 0)

    def tab_map(jt, r):
        _, _, _, j0 = decode(jt)
        return (j0, r)

    kern = functools.partial(_proj_rope_kernel, n_x=len(x_args), p_tiles=x_p_tiles, n_rope=n_rope,
                             scale_cols=scale_cols, scale=scale, out_cols=out_cols)
    n_out = w.shape[1]
    return pl.pallas_call(
        kern,
        grid=(T // dil // tj, dil),
        in_specs=x_specs + [
            pl.BlockSpec((D, n_out), lambda jt, r: (0, 0)),
            pl.BlockSpec((tj, LANES), tab_map),
            pl.BlockSpec((tj, LANES), tab_map),
        ],
        out_specs=[pl.BlockSpec((tj, c), out_map) for c in out_cols],
        out_shape=[jax.ShapeDtypeStruct((T, c), dt) for c, dt in zip(out_cols, out_dtypes)],
        compiler_params=_cparams(("parallel", "parallel")),
    )(*x_args, w, cosv, sinv)


def _from_slab_kernel(x_ref, o_ref):
    o_ref[...] = _slab_read(x_ref)


def _from_slab(x, tok0, n_tok):
    tm = TOK_TILE
    assert tok0 % tm == 0 and n_tok % tm == 0
    return pl.pallas_call(
        _from_slab_kernel,
        grid=(n_tok // tm,),
        in_specs=[pl.BlockSpec((tm * SLAB, LANES), lambda i: (tok0 // tm + i, 0))],
        out_specs=pl.BlockSpec((tm, D_MODEL), lambda i: (i, 0)),
        out_shape=jax.ShapeDtypeStruct((n_tok, D_MODEL), F32),
        compiler_params=_cparams(("parallel",)),
    )(x)


def _proj_kernel(x_ref, w_ref, *out_refs, out_cols):
    xb = _slab_read(x_ref).astype(BF16)
    col = 0
    for o_ref, width in zip(out_refs, out_cols):
        for c0, cw in _col_chunks(width):
            y = jnp.dot(xb, w_ref[:, col + c0:col + c0 + cw], preferred_element_type=F32)
            o_ref[:, c0:c0 + cw] = y.astype(o_ref.dtype)
        col += width


def _proj(x, T, w, out_cols, out_dtypes):
    D = D_MODEL
    tm = TOK_TILE
    n_out = w.shape[1]
    assert sum(out_cols) == n_out and T % tm == 0
    return pl.pallas_call(
        functools.partial(_proj_kernel, out_cols=out_cols),
        grid=(T // tm,),
        in_specs=[pl.BlockSpec((tm * SLAB, LANES), lambda i: (i, 0)), pl.BlockSpec((D, n_out), lambda i: (0, 0))],
        out_specs=[pl.BlockSpec((tm, c), lambda i: (i, 0)) for c in out_cols],
        out_shape=[jax.ShapeDtypeStruct((T, c), dt) for c, dt in zip(out_cols, out_dtypes)],
        compiler_params=_cparams(("parallel",)),
    )(x, w)


def _band_attn_kernel(*refs, segs, dil, radius, q_per_k, has_sink, want_lse):
    it = iter(refs)
    sink_ref = next(it) if has_sink else None
    q_ref = next(it)
    kp_ref, kc_ref, kn_ref = next(it), next(it), next(it)
    vp_ref, vc_ref, vn_ref = next(it), next(it), next(it)
    o_ref = next(it)
    n_q_slabs = q_ref.shape[1] // LANES

    i = pl.program_id(0)
    _, tps, j = _seq_pos(i, ATT_Q, segs, dil)
    prev_ok = j > 0
    next_ok = j < tps - 1
    W = 2 * radius + ATT_Q
    row = lax.broadcasted_iota(jnp.int32, (ATT_Q, W), 0)
    col = lax.broadcasted_iota(jnp.int32, (ATT_Q, W), 1)
    rel = col - radius - row
    ok = (jnp.abs(rel) <= radius) & ((col >= radius) | prev_ok) & ((col < radius + ATT_Q) | next_ok)
    bias = jnp.where(ok, 0.0, NEG_INF).astype(F32)

    k_all = jnp.concatenate([kp_ref[...], kc_ref[...], kn_ref[...]], axis=0)
    v_all = jnp.concatenate([vp_ref[...], vc_ref[...], vn_ref[...]], axis=0)
    nk = k_all.shape[1] // LANES
    lane_lo = lax.broadcasted_iota(jnp.int32, (1, LANES), 1) < HEAD_DIM
    bias_m = jnp.concatenate([bias] * q_per_k, axis=0) if q_per_k > 1 else bias
    M = ATT_Q * q_per_k
    zero = jnp.zeros((), BF16)

    for kc in range(nk):
        ksl = k_all[:, kc * LANES:(kc + 1) * LANES]
        vsl = v_all[:, kc * LANES:(kc + 1) * LANES]
        slabs = [kc * q_per_k + t for t in range(q_per_k)]
        qm = jnp.concatenate([q_ref[:, m * LANES:(m + 1) * LANES] for m in slabs], axis=0) \
            if q_per_k > 1 else q_ref[:, kc * LANES:(kc + 1) * LANES]
        o_acc = jnp.zeros((M, LANES), F32)
        lse_acc = jnp.zeros((M, LANES), F32)
        for half in range(2):
            keep = lane_lo if half == 0 else jnp.logical_not(lane_lo)
            kx = jnp.where(keep, ksl, zero)
            vx = jnp.where(keep, vsl, zero)
            s = lax.dot_general(qm, kx, (((1,), (1,)), ((), ())), preferred_element_type=F32) + bias_m
            mx = jnp.max(s, axis=1, keepdims=True)
            if has_sink:
                sk = jnp.concatenate(
                    [jnp.full((ATT_Q, 1), sink_ref[2 * m + half], F32) for m in slabs], axis=0)
                mx = jnp.maximum(mx, sk)
            p = jnp.exp(s - mx)
            l = jnp.sum(p, axis=1, keepdims=True)
            if has_sink:
                l = l + jnp.exp(sk - mx)
            pv = jnp.dot(p.astype(BF16), vx, preferred_element_type=F32)
            o_acc = o_acc + pv / l
            if want_lse:
                lse_acc = jnp.where(keep, mx + jnp.log(l), lse_acc)
        for t, m in enumerate(slabs):
            if want_lse:
                _slab_write(o_ref, o_acc[t * ATT_Q:(t + 1) * ATT_Q], first_chunk=m)
                _slab_write(o_ref, lse_acc[t * ATT_Q:(t + 1) * ATT_Q], first_chunk=n_q_slabs + m)
            else:
                o_ref[:, m * LANES:(m + 1) * LANES] = o_acc[t * ATT_Q:(t + 1) * ATT_Q].astype(o_ref.dtype)


def _band_attn(q, k, v, segs, *, dil, radius, q_per_k, sink=None, want_lse=False):
    T, wq = q.shape
    wk = k.shape[1]
    (n_p, s_p), (n_s, s_s) = segs
    assert ATT_Q % radius == 0
    hb = ATT_Q // radius
    n_halo = T // radius
    nt = T // ATT_Q
    l_p, l_s = s_p // dil, s_s // dil
    assert l_p % ATT_Q == 0 and l_s % ATT_Q == 0
    p_tiles = (n_p * s_p) // ATT_Q

    def out_map(i):
        in_p = i < p_tiles
        nb = jnp.where(in_p, l_p // ATT_Q, l_s // ATT_Q)
        i_loc = jnp.where(in_p, i, i - p_tiles)
        n = i_loc // nb
        jb = lax.rem(i_loc, nb)
        b = n // dil
        r = lax.rem(n, dil)
        base = jnp.where(in_p, 0, p_tiles // dil)
        return (base + b * nb + jb, r, 0)

    in_specs = []
    args = []
    if sink is not None:
        in_specs.append(pl.BlockSpec(memory_space=pltpu.SMEM))
        args.append(sink)
    in_specs.append(pl.BlockSpec((ATT_Q, wq), lambda i: (i, 0)))
    args.append(q)
    for arr in (k, v):
        in_specs += [
            pl.BlockSpec((radius, wk), lambda i: (jnp.maximum(i * hb - 1, 0), 0)),
            pl.BlockSpec((ATT_Q, wk), lambda i: (i, 0)),
            pl.BlockSpec((radius, wk), lambda i: (jnp.minimum((i + 1) * hb, n_halo - 1), 0)),
        ]
        args += [arr, arr, arr]
    if want_lse:
        assert 2 * wq // LANES == SLAB
        if dil == 1:
            out_shape = jax.ShapeDtypeStruct((T * SLAB, LANES), F32)
            out_specs = pl.BlockSpec((ATT_Q * SLAB, LANES), lambda i: (i, 0))
        else:
            out_shape = jax.ShapeDtypeStruct((T // dil, dil * SLAB, LANES), F32)
            out_specs = pl.BlockSpec((ATT_Q, SLAB, LANES), out_map)
    else:
        assert dil == 1
        out_shape = jax.ShapeDtypeStruct((T, wq), BF16)
        out_specs = pl.BlockSpec((ATT_Q, wq), lambda i: (i, 0))
    kern = functools.partial(_band_attn_kernel, segs=segs, dil=dil, radius=radius, q_per_k=q_per_k,
                             has_sink=sink is not None, want_lse=want_lse)
    out = pl.pallas_call(
        kern, grid=(nt,), in_specs=in_specs, out_specs=out_specs, out_shape=out_shape,
        compiler_params=_cparams(("parallel",)),
    )(*args)
    return out.reshape(T * SLAB, LANES) if want_lse else out


def _split3(v):
    hi = v.astype(BF16)
    r1 = v - hi.astype(F32)
    mid = r1.astype(BF16)
    lo = (r1 - mid.astype(F32)).astype(BF16)
    return hi, mid, lo


def _route(x1, wr_ref, br_ref):
    hi, mid, _ = _split3(x1)
    xs = jnp.concatenate([hi, hi, mid], axis=1)
    lt = lax.dot_general(wr_ref[...], xs, (((1,), (1,)), ((), ())), preferred_element_type=F32) + br_ref[...]
    n_tok = x1.shape[0]
    row = lax.broadcasted_iota(jnp.int32, (MOE_EPG, n_tok), 0)
    big = jnp.int32(MOE_EPG)

    def first_argmax(v):
        m = jnp.max(v, axis=0, keepdims=True)
        return m, jnp.min(jnp.where(v == m, row, big), axis=0, keepdims=True)

    n_e = MOE_GROUPS * MOE_EPG
    gl = lt[n_e:n_e + MOE_EPG]
    gmax, gidx = first_argmax(gl)
    g_w = 1.0 / jnp.sum(jnp.exp(gl - gmax), axis=0, keepdims=True)
    el = lt[(MOE_GROUPS - 1) * MOE_EPG:n_e]
    for g in range(MOE_GROUPS - 2, -1, -1):
        el = jnp.where(gidx == g, lt[g * MOE_EPG:(g + 1) * MOE_EPG], el)
    v1, i1 = first_argmax(el)
    el2 = jnp.where(row == i1, -jnp.inf, el)
    v2, i2 = first_argmax(el2)
    ex = jnp.exp(v2 - v1)
    w1 = 1.0 / (1.0 + ex)
    w2 = ex * w1
    first_lo = i1 < i2
    e_lo = jnp.minimum(i1, i2).astype(F32)
    e_hi = jnp.maximum(i1, i2).astype(F32)
    c_lo = g_w * jnp.where(first_lo, w1, w2)
    c_hi = g_w * jnp.where(first_lo, w2, w1)
    pad = jnp.zeros((ROUTE_ROWS - 5, n_tok), F32)
    return jnp.concatenate([gidx.astype(F32), e_lo, e_hi, c_lo, c_hi, pad], axis=0)


def _rows_to_lanes(rows, sel):
    pieces = jnp.concatenate(_split3(rows), axis=0)
    return lax.dot_general(pieces, sel, (((0,), (0,)), ((), ())), preferred_element_type=F32)


def _gelu_tanh(x):
    return 0.5 * x * (1.0 + jnp.tanh(0.7978845608028654 * (x + 0.044715 * x * x * x)))


def _silu(x):
    return x / (1.0 + jnp.exp(-x))


def _out_ln_kernel(*refs, mode, n_h, n_x, p_tiles):
    h_refs = refs[:n_h]
    extra_ref = refs[n_h] if mode == "gla" else None
    base = n_h + (1 if mode == "gla" else 0)
    x_refs = refs[base:base + n_x]
    wo_ref, g_ref, b_ref, wr_ref, br_ref, x1_ref, rt_ref = refs[base + n_x:base + n_x + 7]

    if mode == "plain":
        hb = h_refs[0][...]
    elif mode == "rglru":
        gate, hf, hbw = h_refs
        hb = (_gelu_tanh(gate[...].astype(F32)) * (hf[...].astype(F32) + hbw[...].astype(F32))).astype(BF16)
    elif mode == "gla":
        of, ob, gg = h_refs
        o = of[...] + ob[...]
        parts = []
        for h in range(GLA_HEADS):
            oh = o[:, h * GLA_DVH:(h + 1) * GLA_DVH]
            ms = jnp.mean(oh * oh, axis=-1, keepdims=True)
            parts.append(oh * lax.rsqrt(ms + LN_EPS) * extra_ref[...])
        o = jnp.concatenate(parts, axis=1)
        hb = (o * _silu(gg[...].astype(F32))).astype(BF16)
    else:
        half = SLAB // 2
        os_ = [_slab_read(r, range(half)) for r in h_refs]
        ls_ = [_slab_read(r, range(half, SLAB)) for r in h_refs]
        mx = functools.reduce(jnp.maximum, ls_)
        es = [jnp.exp(l - mx) for l in ls_]
        den = functools.reduce(jnp.add, es)
        o = functools.reduce(jnp.add, [(e / den) * ov for e, ov in zip(es, os_)])
        hb = o.astype(BF16)

    acc = jnp.dot(hb, wo_ref[...], preferred_element_type=F32)
    x1 = _layer_norm(ALPHA * _x_read(x_refs, p_tiles) + acc, g_ref[...], b_ref[...])
    _slab_write(x1_ref, x1)
    rt_ref[...] = _route(x1, wr_ref, br_ref)


def _out_ln(hs, x, T, wo, ln_g, ln_b, wr, br, *, mode, extra=None):
    D = D_MODEL
    tm = TOK_TILE
    rows = lambda h: tm * SLAB if mode == "dil" else tm
    in_specs = [pl.BlockSpec((rows(h), h.shape[1]), lambda i: (i, 0)) for h in hs]
    args = list(hs)
    if mode == "gla":
        in_specs.append(pl.BlockSpec((1, extra.shape[1]), lambda i: (0, 0)))
        args.append(extra)
    x_specs, x_args, p_tiles = _x_specs(x, tm)
    in_specs += x_specs + [
        pl.BlockSpec(wo.shape, lambda i: (0, 0)),
        pl.BlockSpec((1, D), lambda i: (0, 0)),
        pl.BlockSpec((1, D), lambda i: (0, 0)),
        pl.BlockSpec(wr.shape, lambda i: (0, 0)),
        pl.BlockSpec(br.shape, lambda i: (0, 0)),
    ]
    args += x_args + [wo, ln_g, ln_b, wr, br]
    return pl.pallas_call(
        functools.partial(_out_ln_kernel, mode=mode, n_h=len(hs), n_x=len(x_args), p_tiles=p_tiles),
        grid=(T // tm,),
        in_specs=in_specs,
        out_specs=[pl.BlockSpec((tm * SLAB, LANES), lambda i: (i, 0)),
                   pl.BlockSpec((ROUTE_ROWS, tm), lambda i: (0, i))],
        out_shape=[jax.ShapeDtypeStruct((T * SLAB, LANES), F32),
                   jax.ShapeDtypeStruct((ROUTE_ROWS, T), F32)],
        compiler_params=_cparams(("parallel",)),
    )(*args)


def _moe_kernel(tg_ref, tlo_ref, thi_ref, nv_ref, base_ref, order_ref, x_hbm, c_hbm, sel_ref, wgl_ref, wul_ref,
                wdl_ref, wgh_ref, wuh_ref, wdh_ref, g_ref, b_ref, out_hbm, xbuf, obuf, cbuf, cslab, gsem, ssem,
                csem, *, nt, n_tok):
    i = pl.program_id(0)
    tm = xbuf.shape[1] // SLAB
    win = cbuf.shape[2]
    slot = lax.rem(i, 2)
    used = nv_ref[i] > 0
    nxt = jnp.minimum(i + 1, nt - 1)
    next_used = jnp.logical_and(i + 1 < nt, nv_ref[nxt] > 0)

    def token(ref, tok):
        return ref.at[pl.ds(pl.multiple_of(tok * SLAB, SLAB), SLAB)]

    def weights_copy(tile, sl):
        a = pl.multiple_of(jnp.bitwise_and(base_ref[tile], -LANES), LANES)
        return pltpu.make_async_copy(c_hbm.at[:, pl.ds(a, win)], cbuf.at[sl], csem.at[sl])

    def gather_start(tile, sl):
        base, last = base_ref[tile], nv_ref[tile] - 1
        for r in range(tm):
            tok = order_ref[base + jnp.minimum(r, last)]
            pltpu.make_async_copy(token(x_hbm, tok), xbuf.at[sl, pl.ds(r * SLAB, SLAB)], gsem.at[sl]).start()
        weights_copy(tile, sl).start()

    def gather_wait(tile, sl):
        pltpu.make_async_copy(x_hbm.at[pl.ds(0, tm * SLAB)], xbuf.at[sl], gsem.at[sl]).wait()
        weights_copy(tile, sl).wait()

    def scatter_start(tile, sl):
        base, n = base_ref[tile], nv_ref[tile]
        dump = n_tok + sl * tm
        for r in range(tm):
            tok = jnp.where(r < n, order_ref[base + jnp.minimum(r, n - 1)], dump + r)
            pltpu.make_async_copy(obuf.at[sl, pl.ds(r * SLAB, SLAB)], token(out_hbm, tok), ssem.at[sl]).start()

    def scatter_wait(sl):
        pltpu.make_async_copy(obuf.at[sl], out_hbm.at[pl.ds(0, tm * SLAB)], ssem.at[sl]).wait()

    @pl.when(i == 0)
    def _():
        obuf[...] = jnp.zeros_like(obuf)
        for sl in range(2):
            cp = pltpu.make_async_copy(obuf.at[sl], out_hbm.at[pl.ds((n_tok + sl * tm) * SLAB, tm * SLAB)],
                                       ssem.at[sl])
            cp.start()
            cp.wait()

    @pl.when(jnp.logical_and(i == 0, used))
    def _():
        gather_start(0, 0)

    @pl.when(used)
    def _():
        gather_wait(i, slot)

        @pl.when(next_used)
        def _():
            gather_start(i + 1, 1 - slot)

        @pl.when(i >= 2)
        def _():
            scatter_wait(slot)

        x = _slab_read(xbuf.at[slot])
        cslab[...] = _rows_to_lanes(cbuf[slot], sel_ref[...])
        cw = cslab[pl.ds(jnp.bitwise_and(base_ref[i], LANES - 1), tm), :]
        xb = x.astype(BF16)
        acc = jnp.zeros((tm, D_MODEL), F32)
        for wg_ref, wu_ref, wd_ref, lane in ((wgl_ref, wul_ref, wdl_ref, 0), (wgh_ref, wuh_ref, wdh_ref, 1)):
            hg = jnp.dot(xb, wg_ref[0, 0], preferred_element_type=F32)
            hu = jnp.dot(xb, wu_ref[0, 0], preferred_element_type=F32)
            h = _silu(hg) * hu * cw[:, lane:lane + 1]
            acc = acc + jnp.dot(h.astype(BF16), wd_ref[0, 0], preferred_element_type=F32)
        _slab_write(obuf.at[slot], _layer_norm(ALPHA * x + acc, g_ref[...], b_ref[...]))
        scatter_start(i, slot)

        @pl.when(jnp.logical_not(next_used))
        def _():
            @pl.when(i >= 1)
            def _():
                scatter_wait(1 - slot)
            scatter_wait(slot)


_PAIR_LO = tuple(a for a in range(MOE_EPG) for b in range(a + 1, MOE_EPG))
_PAIR_HI = tuple(b for a in range(MOE_EPG) for b in range(a + 1, MOE_EPG))
N_PAIRS = len(_PAIR_LO)
N_CLASSES = MOE_GROUPS * N_PAIRS


def _moe_schedule(rt, tm, nt):
    T = rt.shape[1]
    i32 = jnp.int32
    g, lo, hi = rt[0].astype(i32), rt[1].astype(i32), rt[2].astype(i32)
    cls = g * N_PAIRS + lo * (2 * MOE_EPG - 1 - lo) // 2 + (hi - lo - 1)
    w_lo, w_hi = rt[3], rt[4]
    cls_sorted, order, w_lo, w_hi = lax.sort((cls, jnp.arange(T, dtype=i32), w_lo, w_hi), num_keys=1,
                                             is_stable=True)
    starts = jnp.searchsorted(cls_sorted, jnp.arange(N_CLASSES + 1, dtype=i32), side="left").astype(i32)
    counts = starts[1:] - starts[:-1]
    tiles_per = (counts + tm - 1) // tm
    cum = jnp.cumsum(tiles_per)
    n_used = cum[-1]
    t = jnp.arange(nt, dtype=i32)
    tc = jnp.minimum(t, n_used - 1)
    tcls = jnp.searchsorted(cum, tc, side="right").astype(i32)
    k = tc - (cum[tcls] - tiles_per[tcls])
    nvalid = jnp.where(t < n_used, jnp.clip(counts[tcls] - k * tm, 0, tm), 0).astype(i32)
    base = (starts[tcls] + k * tm).astype(i32)
    pr = tcls % N_PAIRS
    tile_lo = jnp.asarray(_PAIR_LO, i32)[pr]
    tile_hi = jnp.asarray(_PAIR_HI, i32)[pr]
    c_sorted = jnp.zeros((ROUTE_ROWS, T + tm + LANES), F32).at[0, :T].set(w_lo).at[1, :T].set(w_hi)
    return (tcls // N_PAIRS).astype(i32), tile_lo, tile_hi, nvalid, base, order, c_sorted


def _moe(x1, rt, wg, wu, wd, ln_g, ln_b):
    T = x1.shape[0] // SLAB
    D = D_MODEL
    tm = MOE_TILE
    nt = T // tm + N_CLASSES
    win = tm + LANES
    *sched, c_sorted = _moe_schedule(rt, tm, nt)
    sel = jnp.tile(jnp.eye(ROUTE_ROWS, LANES, dtype=BF16), (3, 1))
    lo_map = lambda i, tg, tlo, thi, nv, base, order: (tg[i], tlo[i], 0, 0)
    hi_map = lambda i, tg, tlo, thi, nv, base, order: (tg[i], thi[i], 0, 0)
    const = lambda i, tg, tlo, thi, nv, base, order: (0, 0)
    grid_spec = pltpu.PrefetchScalarGridSpec(
        num_scalar_prefetch=6,
        grid=(nt,),
        in_specs=[
            pl.BlockSpec(memory_space=pl.ANY),
            pl.BlockSpec(memory_space=pl.ANY),
            pl.BlockSpec(sel.shape, const),
            pl.BlockSpec((1, 1, D, MOE_FF), lo_map),
            pl.BlockSpec((1, 1, D, MOE_FF), lo_map),
            pl.BlockSpec((1, 1, MOE_FF, D), lo_map),
            pl.BlockSpec((1, 1, D, MOE_FF), hi_map),
            pl.BlockSpec((1, 1, D, MOE_FF), hi_map),
            pl.BlockSpec((1, 1, MOE_FF, D), hi_map),
            pl.BlockSpec((1, D), const),
            pl.BlockSpec((1, D), const),
        ],
        out_specs=pl.BlockSpec(memory_space=pl.ANY),
        scratch_shapes=[pltpu.VMEM((2, tm * SLAB, LANES), F32), pltpu.VMEM((2, tm * SLAB, LANES), F32),
                        pltpu.VMEM((2, ROUTE_ROWS, win), F32), pltpu.VMEM((win, LANES), F32),
                        pltpu.SemaphoreType.DMA((2,)), pltpu.SemaphoreType.DMA((2,)),
                        pltpu.SemaphoreType.DMA((2,))],
    )
    return pl.pallas_call(
        functools.partial(_moe_kernel, nt=nt, n_tok=T),
        grid_spec=grid_spec,
        out_shape=jax.ShapeDtypeStruct(((T + 2 * tm) * SLAB, LANES), F32),
        compiler_params=_cparams(("arbitrary",)),
    )(*sched, x1, c_sorted, sel, wg, wu, wd, wg, wu, wd, ln_g, ln_b)


def _rglru_stream(z, u_ref, up_ref, un_ref, w_ref, cw_ref, cb_ref, ba_ref, bi_ref, lam_ref,
                  h_out_ref, a_s, b_s, h_s, carry_ref, prev_ok, next_ok, reset):
    tt = u_ref.shape[0]
    u_mid = u_ref[...]
    up = jnp.where(prev_ok, up_ref[...], 0.0)
    un = jnp.where(next_ok, un_ref[...], 0.0)
    head = jnp.concatenate([up, u_mid[:SUBLANES]], axis=0)
    tail = jnp.concatenate([u_mid[tt - SUBLANES:], un], axis=0)
    left = RG_CONV // 2
    u = cb_ref[...]
    for kk in range(RG_CONV):
        s = kk - left
        if s == 0:
            shifted = u_mid
        else:
            rolled = pltpu.roll(u_mid, (-s) % tt, 0)
            if s < 0:
                shifted = jnp.concatenate([head[SUBLANES + s:2 * SUBLANES + s], rolled[SUBLANES:]], axis=0)
            else:
                shifted = jnp.concatenate([rolled[:tt - SUBLANES], tail[s:SUBLANES + s]], axis=0)
        u = u + cw_ref[kk:kk + 1, :] * shifted
    ub = u.astype(BF16)
    sp = jnp.maximum(-lam_ref[...], 0.0) + jnp.log(1.0 + jnp.exp(-jnp.abs(lam_ref[...])))
    for n in range(RG_NT):
        s0 = min(max(n - 1, 0), RG_NT - 3) * LANES
        zz = jnp.dot(ub[:, s0:s0 + RG_BAND], w_ref[n], preferred_element_type=F32)
        sl = slice(n * LANES, (n + 1) * LANES)
        r = jax.nn.sigmoid(zz[:, :LANES] + ba_ref[:, sl])
        ig = jax.nn.sigmoid(zz[:, LANES:] + bi_ref[:, sl])
        log_a = -RG_C * r * sp[:, sl]
        a = jnp.exp(log_a)
        a_s[:, sl] = a
        b_s[:, sl] = jnp.sqrt(-jnp.tanh(log_a) * (a * a + 1.0)) * ig * u[:, sl]

    @pl.when(reset)
    def _():
        carry_ref[...] = jnp.zeros_like(carry_ref)

    n_grp = tt // SUBLANES

    def body(gi, h):
        g = gi if z == 0 else n_grp - 1 - gi
        base = pl.multiple_of(g * SUBLANES, SUBLANES)
        a8 = a_s[pl.ds(base, SUBLANES), :]
        b8 = b_s[pl.ds(base, SUBLANES), :]
        rows = [None] * SUBLANES
        order = range(SUBLANES) if z == 0 else range(SUBLANES - 1, -1, -1)
        for r_ in order:
            h = a8[r_:r_ + 1] * h + b8[r_:r_ + 1]
            rows[r_] = h
        h_s[pl.ds(base, SUBLANES), :] = jnp.concatenate(rows, axis=0)
        return h

    h_last = lax.fori_loop(0, n_grp, body, carry_ref[...])
    carry_ref[...] = h_last
    h_out_ref[...] = h_s[...].astype(h_out_ref.dtype)


def _rglru_scan_kernel(uf_ref, ufp_ref, ufn_ref, ubk_ref, ubp_ref, ubn_ref, wf_ref, wb_ref, cw_ref, cb_ref,
                       ba_ref, bi_ref, lam_ref, hf_ref, hb_ref, a_s, b_s, h_s, cf_ref, cbk_ref, *, segs, nt):
    i = pl.program_id(0)
    tt = uf_ref.shape[0]
    _, tps, j = _seq_pos(i, tt, segs)
    _rglru_stream(0, uf_ref, ufp_ref, ufn_ref, wf_ref, cw_ref, cb_ref, ba_ref.at[0:1], bi_ref.at[0:1],
                  lam_ref.at[0:1], hf_ref, a_s, b_s, h_s, cf_ref, j > 0, j < tps - 1, j == 0)
    ib = nt - 1 - i
    _, tps_b, jb = _seq_pos(ib, tt, segs)
    _rglru_stream(1, ubk_ref, ubp_ref, ubn_ref, wb_ref, cw_ref, cb_ref, ba_ref.at[1:2], bi_ref.at[1:2],
                  lam_ref.at[1:2], hb_ref, a_s, b_s, h_s, cbk_ref, jb > 0, jb < tps_b - 1, jb == tps_b - 1)


def _rglru_scan(u_pre, w_band, conv_w, conv_b, ba, bi, lam, segs):
    T, W = u_pre.shape
    tt = SCAN_TILE
    nt = T // tt
    hpt = tt // SUBLANES
    n_h = T // SUBLANES
    fwd = lambda i: (i, 0)
    bwd = lambda i: (nt - 1 - i, 0)

    def halo_specs(idx):
        return [
            pl.BlockSpec((tt, W), lambda i: (idx(i), 0)),
            pl.BlockSpec((SUBLANES, W), lambda i: (jnp.maximum(idx(i) * hpt - 1, 0), 0)),
            pl.BlockSpec((SUBLANES, W), lambda i: (jnp.minimum((idx(i) + 1) * hpt, n_h - 1), 0)),
        ]

    const2 = lambda i: (0, 0)
    const3 = lambda i: (0, 0, 0)
    in_specs = halo_specs(lambda i: i) + halo_specs(lambda i: nt - 1 - i) + [
        pl.BlockSpec(w_band.shape[1:], const3),
        pl.BlockSpec(w_band.shape[1:], const3),
        pl.BlockSpec(conv_w.shape, const2),
        pl.BlockSpec(conv_b.shape, const2),
        pl.BlockSpec(ba.shape, const2),
        pl.BlockSpec(bi.shape, const2),
        pl.BlockSpec(lam.shape, const2),
    ]
    return pl.pallas_call(
        functools.partial(_rglru_scan_kernel, segs=segs, nt=nt),
        grid=(nt,),
        in_specs=in_specs,
        out_specs=[pl.BlockSpec((tt, W), fwd), pl.BlockSpec((tt, W), bwd)],
        out_shape=[jax.ShapeDtypeStruct((T, W), BF16), jax.ShapeDtypeStruct((T, W), BF16)],
        scratch_shapes=[pltpu.VMEM((tt, W), F32), pltpu.VMEM((tt, W), F32), pltpu.VMEM((tt, W), F32),
                        pltpu.VMEM((1, W), F32), pltpu.VMEM((1, W), F32)],
        compiler_params=_cparams(("arbitrary",)),
    )(u_pre, u_pre, u_pre, u_pre, u_pre, u_pre, w_band[0], w_band[1], conv_w, conv_b, ba, bi, lam)


def _rglru_band_weights(wa, wi):
    def dense(w):
        eye = jnp.eye(RG_BLOCKS, dtype=w.dtype)
        return jnp.einsum("ncd,nm->ncmd", w, eye).reshape(RG_WIDTH, RG_WIDTH)

    out = []
    for z in range(2):
        da, di = dense(wa[z]), dense(wi[z])
        tiles = []
        for n in range(RG_NT):
            s0 = min(max(n - 1, 0), RG_NT - 3) * LANES
            sl = slice(n * LANES, (n + 1) * LANES)
            tiles.append(jnp.concatenate([da[s0:s0 + RG_BAND, sl], di[s0:s0 + RG_BAND, sl]], axis=1))
        out.append(jnp.stack(tiles))
    return jnp.stack(out).astype(BF16)


def _gla_stream(z, q_ref, k_ref, v_ref, lr_ref, wa2_ref, ba_ref, o_ref, st_ref, reset):
    tt = q_ref.shape[0]
    C = GLA_CHUNK

    @pl.when(reset)
    def _():
        st_ref[z] = jnp.zeros(st_ref.shape[1:], F32)

    zz = jnp.dot(lr_ref[...].astype(BF16), wa2_ref[:, z * GLA_DK:(z + 1) * GLA_DK],
                 preferred_element_type=F32) + ba_ref[:, z * GLA_DK:(z + 1) * GLA_DK]
    log_a = -(jnp.maximum(-zz, 0.0) + jnp.log(1.0 + jnp.exp(-jnp.abs(zz)))) / GLA_TAU
    ri = lax.broadcasted_iota(jnp.int32, (C, C), 0)
    ci = lax.broadcasted_iota(jnp.int32, (C, C), 1)
    causal = (ri >= ci) if z == 0 else (ri <= ci)
    tri = causal.astype(F32)
    mid = C // 2 if z == 0 else C - 1 - C // 2
    last = C - 1 if z == 0 else 0
    scale = GLA_DKH ** -0.5
    chunks = range(tt // C) if z == 0 else range(tt // C - 1, -1, -1)
    for c in chunks:
        rs = slice(c * C, (c + 1) * C)
        b = jnp.dot(tri, log_a[rs], preferred_element_type=F32, precision=lax.Precision.HIGHEST)
        b_mid = b[mid:mid + 1]
        b_last = b[last:last + 1]
        qc = q_ref[rs, :] * scale
        kc = k_ref[rs, :]
        qd = (qc * jnp.exp(b - b_mid)).astype(BF16)
        kd = (kc * jnp.exp(b_mid - b)).astype(BF16)
        ks = (kc * jnp.exp(b_last - b)).astype(BF16)
        qb = (qc * jnp.exp(b)).astype(BF16)
        dec = jnp.exp(b_last)
        for h in range(GLA_HEADS):
            ksl = slice(h * GLA_DKH, (h + 1) * GLA_DKH)
            vsl = slice(h * GLA_DVH, (h + 1) * GLA_DVH)
            vc = v_ref[rs, vsl]
            att = lax.dot_general(qd[:, ksl], kd[:, ksl], (((1,), (1,)), ((), ())), preferred_element_type=F32)
            att = jnp.where(causal, att, 0.0)
            o = jnp.dot(att.astype(BF16), vc, preferred_element_type=F32)
            st = st_ref[z, h]
            o = o + lax.dot_general(qb[:, ksl], st.astype(BF16), (((1,), (1,)), ((), ())),
                                    preferred_element_type=F32)
            o_ref[rs, vsl] = o
            upd = lax.dot_general(vc, ks[:, ksl], (((0,), (0,)), ((), ())), preferred_element_type=F32)
            st_ref[z, h] = st * dec[:, ksl] + upd


def _gla_kernel(qf, kf, vf, lf, qb, kb, vb, lb, wa2_ref, ba_ref, of_ref, ob_ref, st_ref, *, segs, nt):
    i = pl.program_id(0)
    tt = qf.shape[0]
    _, tps, j = _seq_pos(i, tt, segs)
    _gla_stream(0, qf, kf, vf, lf, wa2_ref, ba_ref, of_ref, st_ref, j == 0)
    ib = nt - 1 - i
    _, tps_b, jb = _seq_pos(ib, tt, segs)
    _gla_stream(1, qb, kb, vb, lb, wa2_ref, ba_ref, ob_ref, st_ref, jb == tps_b - 1)


def _gla(q, k, v, lr, wa2, ba, segs):
    T = q.shape[0]
    tt = SCAN_TILE
    nt = T // tt
    fwd = lambda i: (i, 0)
    bwd = lambda i: (nt - 1 - i, 0)
    in_specs = []
    for idx in (fwd, bwd):
        in_specs += [pl.BlockSpec((tt, GLA_DK), idx), pl.BlockSpec((tt, GLA_DK), idx),
                     pl.BlockSpec((tt, GLA_DV), idx), pl.BlockSpec((tt, LANES), idx)]
    in_specs += [pl.BlockSpec(wa2.shape, lambda i: (0, 0)), pl.BlockSpec(ba.shape, lambda i: (0, 0))]
    return pl.pallas_call(
        functools.partial(_gla_kernel, segs=segs, nt=nt),
        grid=(nt,),
        in_specs=in_specs,
        out_specs=[pl.BlockSpec((tt, GLA_DV), fwd), pl.BlockSpec((tt, GLA_DV), bwd)],
        out_shape=[jax.ShapeDtypeStruct((T, GLA_DV), F32), jax.ShapeDtypeStruct((T, GLA_DV), F32)],
        scratch_shapes=[pltpu.VMEM((2, GLA_HEADS, GLA_DVH, GLA_DKH), F32)],
        compiler_params=_cparams(("arbitrary",)),
    )(q, k, v, lr, q, k, v, lr, wa2, ba)


def _rope_tables(s_max):
    half = HEAD_DIM // 2
    inv = ROPE_THETA ** (-jnp.arange(half, dtype=F32) / half)
    ang = jnp.arange(s_max, dtype=F32)[:, None] * inv[None, :]
    cos, sin = jnp.cos(ang), jnp.sin(ang)
    cos_t = jnp.tile(jnp.concatenate([cos, cos], axis=1), (1, LANES // HEAD_DIM))
    sin_t = jnp.tile(jnp.concatenate([-sin, sin], axis=1), (1, LANES // HEAD_DIM))
    return cos_t, sin_t


def _router_weights(wgr, bgr, wer, ber):
    n_e = MOE_GROUPS * MOE_EPG
    w = jnp.concatenate([jnp.transpose(wer, (0, 2, 1)).reshape(n_e, D_MODEL), wgr.T], axis=0)
    b = jnp.concatenate([ber.reshape(n_e), bgr])
    n_pad = ROUTER_ROWS - w.shape[0]
    w = jnp.pad(w, ((0, n_pad), (0, 0)))
    b = jnp.concatenate([b, jnp.full((n_pad,), NEG_INF, F32)])
    w_hi = w.astype(BF16)
    w_mid = (w - w_hi.astype(F32)).astype(BF16)
    return jnp.concatenate([w_hi, w_mid, w_hi], axis=1), b[:, None]


def _mixer(kind, jl, x, segs, p, cos_t, sin_t, lg1, lb1, wr, br):
    D = D_MODEL
    T = sum(n * s for n, s in segs)
    if kind == 0:
        nq = A_HEADS * HEAD_DIM
        nkv = A_KV_HEADS * HEAD_DIM
        w = p["a_wqkv"][jl]
        wq, wk, wv = w[:, :nq], w[:, nq:nq + nkv], w[:, nq + nkv:]
        dup = lambda t: jnp.repeat(t.reshape(D, A_KV_HEADS, 1, HEAD_DIM), 2, axis=2).reshape(D, 2 * nkv)
        w_ext = jnp.concatenate([wq, dup(wk), dup(wv)], axis=1).astype(BF16)
        q, kd, vd = _proj_rope(x, T, w_ext, cos_t, sin_t, segs, dil=1, n_rope=(nq + 2 * nkv) // LANES,
                               scale_cols=nq, scale=HEAD_DIM ** -0.5, out_cols=(nq, 2 * nkv, 2 * nkv),
                               out_dtypes=(BF16, BF16, BF16))
        o = _band_attn(q, kd, vd, segs, dil=1, radius=A_RADIUS, q_per_k=2, sink=p["a_sink"][jl])
        return _out_ln([o], x, T, p["a_wo"][jl].astype(BF16), lg1, lb1, wr, br, mode="plain")
    if kind == 1:
        gate, u_pre = _proj(x, T, p["b_win"][jl].astype(BF16), (RG_WIDTH, RG_WIDTH), (BF16, F32))
        w_band = _rglru_band_weights(p["b_wa"][jl], p["b_wi"][jl])
        hf, hb = _rglru_scan(u_pre, w_band, p["b_conv_w"][jl], p["b_conv_b"][jl][None, :], p["b_ba"][jl],
                             p["b_bi"][jl], p["b_lam"][jl], segs)
        return _out_ln([gate, hf, hb], x, T, p["b_wo"][jl].astype(BF16), lg1, lb1, wr, br, mode="rglru")
    if kind == 2:
        wa1 = jnp.concatenate([p["c_wa1"][jl][0], p["c_wa1"][jl][1]], axis=1)
        wa1 = jnp.pad(wa1, ((0, 0), (0, LANES - wa1.shape[1])))
        w_all = jnp.concatenate([p["c_wqkvg"][jl], wa1], axis=1).astype(BF16)
        q, k, v, g, lr = _proj(x, T, w_all, (GLA_DK, GLA_DK, GLA_DV, GLA_DV, LANES), (F32, F32, BF16, BF16, F32))
        wa2 = jnp.zeros((LANES, 2 * GLA_DK), F32)
        wa2 = wa2.at[:GLA_RANK, :GLA_DK].set(p["c_wa2"][jl][0])
        wa2 = wa2.at[GLA_RANK:2 * GLA_RANK, GLA_DK:].set(p["c_wa2"][jl][1])
        of, ob = _gla(q, k, v, lr, wa2.astype(BF16), p["c_ba"][jl].reshape(1, 2 * GLA_DK), segs)
        return _out_ln([of, ob, g], x, T, p["c_wo"][jl].astype(BF16), lg1, lb1, wr, br, mode="gla",
                       extra=p["c_norm_g"][jl][None, :])
    ols = []
    for gi, (window, dil) in enumerate(DIL_GROUPS):
        w_g = p["d_wqkv"][jl][:, gi * 3 * DIL_WIDTH:(gi + 1) * 3 * DIL_WIDTH].astype(BF16)
        q, k, v = _proj_rope(x, T, w_g, cos_t, sin_t, segs, dil=dil, n_rope=2 * DIL_WIDTH // LANES,
                             scale_cols=DIL_WIDTH, scale=HEAD_DIM ** -0.5,
                             out_cols=(DIL_WIDTH,) * 3, out_dtypes=(BF16,) * 3)
        ols.append(_band_attn(q, k, v, segs, dil=dil, radius=window // (2 * dil), q_per_k=1, want_lse=True))
    return _out_ln(ols, x, T, p["d_wo"][jl].astype(BF16), lg1, lb1, wr, br, mode="dil")


def kernel(x_prompt, x_sample, ln_g, ln_b, a_wqkv, a_sink, a_wo, b_win, b_conv_w, b_conv_b, b_wa, b_ba, b_wi,
           b_bi, b_lam, b_wo, c_wqkvg, c_wa1, c_wa2, c_ba, c_norm_g, c_wo, d_wqkv, d_wo, m_wgr, m_bgr, m_wer,
           m_ber, m_wg, m_wu, m_wd):
    p = dict(a_wqkv=a_wqkv, a_sink=a_sink, a_wo=a_wo, b_win=b_win, b_conv_w=b_conv_w, b_conv_b=b_conv_b,
             b_wa=b_wa, b_ba=b_ba, b_wi=b_wi, b_bi=b_bi, b_lam=b_lam, b_wo=b_wo, c_wqkvg=c_wqkvg, c_wa1=c_wa1,
             c_wa2=c_wa2, c_ba=c_ba, c_norm_g=c_norm_g, c_wo=c_wo, d_wqkv=d_wqkv, d_wo=d_wo)
    n_p, s_p, D = x_prompt.shape
    n_s, s_s, _ = x_sample.shape
    segs = ((n_p, s_p), (n_s, s_s))
    assert (n_p * s_p) % s_s == 0
    t_p, T = n_p * s_p, n_p * s_p + n_s * s_s
    x = (x_prompt.reshape(t_p, D), x_sample.reshape(T - t_p, D))
    cos_t, sin_t = _rope_tables(max(s_p, s_s))
    wg_b, wu_b, wd_b = m_wg.astype(BF16), m_wu.astype(BF16), m_wd.astype(BF16)

    for layer in range(DEPTH):
        lg1, lb1 = ln_g[layer, 0][None, :], ln_b[layer, 0][None, :]
        lg2, lb2 = ln_g[layer, 1][None, :], ln_b[layer, 1][None, :]
        wr, br = _router_weights(m_wgr[layer], m_bgr[layer], m_wer[layer], m_ber[layer])
        x1, rt = _mixer(layer % 4, layer // 4, x, segs, p, cos_t, sin_t, lg1, lb1, wr, br)
        x = _moe(x1, rt, wg_b[layer], wu_b[layer], wd_b[layer], lg2, lb2)

    return _from_slab(x, 0, t_p).reshape(n_p, s_p, D), _from_slab(x, t_p, T - t_p).reshape(n_s, s_s, D)
```

```python
import functools

import jax
import jax.numpy as jnp
from jax import lax
from jax.experimental import pallas as pl
from jax.experimental.pallas import tpu as pltpu

F32 = jnp.float32
BF16 = jnp.bfloat16

D_MODEL = 1024
HEAD_DIM = 64
ROPE_THETA = 10000.0
A_HEADS = 16
A_KV_HEADS = 4
A_RADIUS = 128
RG_WIDTH = 1408
RG_BLOCKS = 16
RG_BW = RG_WIDTH // RG_BLOCKS
RG_CONV = 4
RG_C = 8.0
GLA_HEADS = 4
GLA_DK = 512
GLA_DV = 1024
GLA_DKH = 128
GLA_DVH = 256
GLA_RANK = 16
GLA_TAU = 16.0
GLA_CHUNK = 64
DIL_GROUPS = ((128, 1), (512, 4), (2048, 16))
N_DIL = 3
DIL_HEADS = 8
DIL_WIDTH = 512
MOE_GROUPS = 4
MOE_EPG = 8
MOE_FF = 256
DEPTH = 4
ALPHA = (2 * DEPTH) ** 0.25
LN_EPS = 1e-5
NEG_INF = -1e30

LANES = 128
SUBLANES = 8
VMEM_LIMIT = 52 * 1024 * 1024
ATT_Q = 128
TOK_TILE = 512
SCAN_TILE = 256
RG_NT = RG_WIDTH // LANES
RG_BAND = 3 * LANES
ROUTER_ROWS = 40
ROUTE_ROWS = 16
MOE_TILE = 128
SLAB = D_MODEL // LANES
PROJ_CHUNK = 512


def _cparams(sem):
    return pltpu.CompilerParams(dimension_semantics=sem, vmem_limit_bytes=VMEM_LIMIT)


def _seq_pos(i, rows, segs, dil=1):
    (n_p, s_p), (_, s_s) = segs
    p_tiles = (n_p * s_p) // rows
    in_p = i < p_tiles
    tps = jnp.where(in_p, (s_p // dil) // rows, (s_s // dil) // rows)
    j = lax.rem(i, tps)
    return in_p, tps, j


def _layer_norm(y, g, b):
    mu = jnp.mean(y, axis=-1, keepdims=True)
    yc = y - mu
    var = jnp.mean(yc * yc, axis=-1, keepdims=True)
    return yc * lax.rsqrt(var + LN_EPS) * g + b


def _rope_slab(t, cos, sin_signed):
    lane = lax.broadcasted_iota(jnp.int32, t.shape, 1)
    first_half = (lane % HEAD_DIM) < (HEAD_DIM // 2)
    partner = jnp.where(first_half, pltpu.roll(t, LANES - HEAD_DIM // 2, 1), pltpu.roll(t, HEAD_DIM // 2, 1))
    return t * cos + partner * sin_signed


def _slab_read(ref, chunks=range(SLAB)):
    if len(ref.shape) == 3:
        return jnp.concatenate([ref[:, c, :] for c in chunks], axis=1)
    rows = ref.shape[0] // SLAB
    return jnp.concatenate([ref[pl.ds(c, rows, stride=SLAB), :] for c in chunks], axis=1)


def _slab_write(ref, val, first_chunk=0):
    for c in range(val.shape[1] // LANES):
        v = val[:, c * LANES:(c + 1) * LANES]
        if len(ref.shape) == 3:
            ref[:, first_chunk + c, :] = v
        else:
            ref[pl.ds(first_chunk + c, ref.shape[0] // SLAB, stride=SLAB), :] = v


def _col_chunks(width):
    out, c = [], 0
    while c < width:
        w = min(PROJ_CHUNK, width - c)
        out.append((c, w))
        c += w
    return out


def _x_specs(x, tm):
    if isinstance(x, tuple):
        xp, xs = x
        p_tiles = xp.shape[0] // tm
        specs = [pl.BlockSpec((tm, D_MODEL), lambda i, *_: (jnp.minimum(i, p_tiles - 1), 0)),
                 pl.BlockSpec((tm, D_MODEL), lambda i, *_: (jnp.maximum(i - p_tiles, 0), 0))]
        return specs, [xp, xs], p_tiles
    return [pl.BlockSpec((tm * SLAB, LANES), lambda i, *_: (i, 0))], [x], 0


def _x_read(x_refs, p_tiles):
    if len(x_refs) == 2:
        return jnp.where(pl.program_id(0) < p_tiles, x_refs[0][...], x_refs[1][...])
    return _slab_read(x_refs[0])


def _proj_rope_kernel(*refs, n_x, p_tiles, n_rope, scale_cols, scale, out_cols):
    x_refs, (w_ref, cos_ref, sin_ref), out_refs = refs[:n_x], refs[n_x:n_x + 3], refs[n_x + 3:]
    xb = _x_read(x_refs, p_tiles).astype(BF16)
    cos = cos_ref[...]
    sin = sin_ref[...]
    col = 0
    for o_ref, width in zip(out_refs, out_cols):
        for c0, cw in _col_chunks(width):
            yc = jnp.dot(xb, w_ref[:, col + c0:col + c0 + cw], preferred_element_type=F32)
            for c in range(cw // LANES):
                slab = (col + c0) // LANES + c
                y = yc[:, c * LANES:(c + 1) * LANES]
                if slab < n_rope:
                    y = _rope_slab(y, cos, sin)
                if slab * LANES < scale_cols:
                    y = y * scale
                o_ref[:, c0 + c * LANES:c0 + (c + 1) * LANES] = y.astype(o_ref.dtype)
        col += width


def _proj_rope(x, T, w, cos_tab, sin_tab, segs, *, dil, n_rope, scale_cols, scale, out_cols, out_dtypes):
    D = D_MODEL
    (n_p, s_p), (n_s, s_s) = segs
    l_p, l_s = s_p // dil, s_s // dil
    tj = min(TOK_TILE, l_s, l_p)
    assert l_p % tj == 0 and l_s % tj == 0 and (n_p * s_p) % (dil * tj) == 0
    if dil == 1:
        x_specs, x_args, x_p_tiles = _x_specs(x, tj)
    else:
        assert x.shape[0] % (dil * SLAB) == 0
        x_args = [x.reshape(x.shape[0] // (dil * SLAB), dil * SLAB, LANES)]
        x_specs, x_p_tiles = [pl.BlockSpec((tj, SLAB, LANES), lambda jt, r: (jt, r, 0))], 0
    cosv = cos_tab.reshape(cos_tab.shape[0] // dil, dil * LANES)
    sinv = sin_tab.reshape(sin_tab.shape[0] // dil, dil * LANES)
    p_tiles = (n_p * l_p) // tj

    def decode(jt):
        in_p = jt < p_tiles
        lt = jnp.where(in_p, l_p // tj, l_s // tj)
        jt_loc = jnp.where(in_p, jt, jt - p_tiles)
        b = jt_loc // lt
        j0 = lax.rem(jt_loc, lt)
        return in_p, lt, b, j0

    def out_map(jt, r):
        in_p, lt, b, j0 = decode(jt)
        base = jnp.where(in_p, 0, (n_p * s_p) // tj)
        return (base + b * (lt * dil) + r * lt + j0, 0)

    def tab_map(jt, r):
        _, _, _, j0 = decode(jt)
        return (j0, r)

    kern = functools.partial(_proj_rope_kernel, n_x=len(x_args), p_tiles=x_p_tiles, n_rope=n_rope,
                             scale_cols=scale_cols, scale=scale, out_cols=out_cols)
    n_out = w.shape[1]
    return pl.pallas_call(
        kern,
        grid=(T // dil // tj, dil),
        in_specs=x_specs + [
            pl.BlockSpec((D, n_out), lambda jt, r: (0, 0)),
            pl.BlockSpec((tj, LANES), tab_map),
            pl.BlockSpec((tj, LANES), tab_map),
        ],
        out_specs=[pl.BlockSpec((tj, c), out_map) for c in out_cols],
        out_shape=[jax.ShapeDtypeStruct((T, c), dt) for c, dt in zip(out_cols, out_dtypes)],
        compiler_params=_cparams(("parallel", "parallel")),
    )(*x_args, w, cosv, sinv)


def _from_slab_kernel(x_ref, o_ref):
    o_ref[...] = _slab_read(x_ref)


def _from_slab(x, tok0, n_tok):
    tm = TOK_TILE
    assert tok0 % tm == 0 and n_tok % tm == 0
    return pl.pallas_call(
        _from_slab_kernel,
        grid=(n_tok // tm,),
        in_specs=[pl.BlockSpec((tm * SLAB, LANES), lambda i: (tok0 // tm + i, 0))],
        out_specs=pl.BlockSpec((tm, D_MODEL), lambda i: (i, 0)),
        out_shape=jax.ShapeDtypeStruct((n_tok, D_MODEL), F32),
        compiler_params=_cparams(("parallel",)),
    )(x)


def _proj_kernel(x_ref, w_ref, *out_refs, out_cols):
    xb = _slab_read(x_ref).astype(BF16)
    col = 0
    for o_ref, width in zip(out_refs, out_cols):
        for c0, cw in _col_chunks(width):
            y = jnp.dot(xb, w_ref[:, col + c0:col + c0 + cw], preferred_element_type=F32)
            o_ref[:, c0:c0 + cw] = y.astype(o_ref.dtype)
        col += width


def _proj(x, T, w, out_cols, out_dtypes):
    D = D_MODEL
    tm = TOK_TILE
    n_out = w.shape[1]
    assert sum(out_cols) == n_out and T % tm == 0
    return pl.pallas_call(
        functools.partial(_proj_kernel, out_cols=out_cols),
        grid=(T // tm,),
        in_specs=[pl.BlockSpec((tm * SLAB, LANES), lambda i: (i, 0)), pl.BlockSpec((D, n_out), lambda i: (0, 0))],
        out_specs=[pl.BlockSpec((tm, c), lambda i: (i, 0)) for c in out_cols],
        out_shape=[jax.ShapeDtypeStruct((T, c), dt) for c, dt in zip(out_cols, out_dtypes)],
        compiler_params=_cparams(("parallel",)),
    )(x, w)


def _band_attn_kernel(*refs, segs, dil, radius, q_per_k, has_sink, want_lse):
    it = iter(refs)
    sink_ref = next(it) if has_sink else None
    q_ref = next(it)
    kp_ref, kc_ref, kn_ref = next(it), next(it), next(it)
    vp_ref, vc_ref, vn_ref = next(it), next(it), next(it)
    o_ref = next(it)
    n_q_slabs = q_ref.shape[1] // LANES

    i = pl.program_id(0)
    _, tps, j = _seq_pos(i, ATT_Q, segs, dil)
    prev_ok = j > 0
    next_ok = j < tps - 1
    W = 2 * radius + ATT_Q
    row = lax.broadcasted_iota(jnp.int32, (ATT_Q, W), 0)
    col = lax.broadcasted_iota(jnp.int32, (ATT_Q, W), 1)
    rel = col - radius - row
    ok = (jnp.abs(rel) <= radius) & ((col >= radius) | prev_ok) & ((col < radius + ATT_Q) | next_ok)
    bias = jnp.where(ok, 0.0, NEG_INF).astype(F32)

    k_all = jnp.concatenate([kp_ref[...], kc_ref[...], kn_ref[...]], axis=0)
    v_all = jnp.concatenate([vp_ref[...], vc_ref[...], vn_ref[...]], axis=0)
    nk = k_all.shape[1] // LANES
    lane_lo = lax.broadcasted_iota(jnp.int32, (1, LANES), 1) < HEAD_DIM
    bias_m = jnp.concatenate([bias] * q_per_k, axis=0) if q_per_k > 1 else bias
    M = ATT_Q * q_per_k
    zero = jnp.zeros((), BF16)

    for kc in range(nk):
        ksl = k_all[:, kc * LANES:(kc + 1) * LANES]
        vsl = v_all[:, kc * LANES:(kc + 1) * LANES]
        slabs = [kc * q_per_k + t for t in range(q_per_k)]
        qm = jnp.concatenate([q_ref[:, m * LANES:(m + 1) * LANES] for m in slabs], axis=0) \
            if q_per_k > 1 else q_ref[:, kc * LANES:(kc + 1) * LANES]
        o_acc = jnp.zeros((M, LANES), F32)
        lse_acc = jnp.zeros((M, LANES), F32)
        for half in range(2):
            keep = lane_lo if half == 0 else jnp.logical_not(lane_lo)
            kx = jnp.where(keep, ksl, zero)
            vx = jnp.where(keep, vsl, zero)
            s = lax.dot_general(qm, kx, (((1,), (1,)), ((), ())), preferred_element_type=F32) + bias_m
            mx = jnp.max(s, axis=1, keepdims=True)
            if has_sink:
                sk = jnp.concatenate(
                    [jnp.full((ATT_Q, 1), sink_ref[2 * m + half], F32) for m in slabs], axis=0)
                mx = jnp.maximum(mx, sk)
            p = jnp.exp(s - mx)
            l = jnp.sum(p, axis=1, keepdims=True)
            if has_sink:
                l = l + jnp.exp(sk - mx)
            pv = jnp.dot(p.astype(BF16), vx, preferred_element_type=F32)
            o_acc = o_acc + pv / l
            if want_lse:
                lse_acc = jnp.where(keep, mx + jnp.log(l), lse_acc)
        for t, m in enumerate(slabs):
            if want_lse:
                _slab_write(o_ref, o_acc[t * ATT_Q:(t + 1) * ATT_Q], first_chunk=m)
                _slab_write(o_ref, lse_acc[t * ATT_Q:(t + 1) * ATT_Q], first_chunk=n_q_slabs + m)
            else:
                o_ref[:, m * LANES:(m + 1) * LANES] = o_acc[t * ATT_Q:(t + 1) * ATT_Q].astype(o_ref.dtype)


def _band_attn(q, k, v, segs, *, dil, radius, q_per_k, sink=None, want_lse=False):
    T, wq = q.shape
    wk = k.shape[1]
    (n_p, s_p), (n_s, s_s) = segs
    assert ATT_Q % radius == 0
    hb = ATT_Q // radius
    n_halo = T // radius
    nt = T // ATT_Q
    l_p, l_s = s_p // dil, s_s // dil
    assert l_p % ATT_Q == 0 and l_s % ATT_Q == 0
    p_tiles = (n_p * s_p) // ATT_Q

    def out_map(i):
        in_p = i < p_tiles
        nb = jnp.where(in_p, l_p // ATT_Q, l_s // ATT_Q)
        i_loc = jnp.where(in_p, i, i - p_tiles)
        n = i_loc // nb
        jb = lax.rem(i_loc, nb)
        b = n // dil
        r = lax.rem(n, dil)
        base = jnp.where(in_p, 0, p_tiles // dil)
        return (base + b * nb + jb, r, 0)

    in_specs = []
    args = []
    if sink is not None:
        in_specs.append(pl.BlockSpec(memory_space=pltpu.SMEM))
        args.append(sink)
    in_specs.append(pl.BlockSpec((ATT_Q, wq), lambda i: (i, 0)))
    args.append(q)
    for arr in (k, v):
        in_specs += [
            pl.BlockSpec((radius, wk), lambda i: (jnp.maximum(i * hb - 1, 0), 0)),
            pl.BlockSpec((ATT_Q, wk), lambda i: (i, 0)),
            pl.BlockSpec((radius, wk), lambda i: (jnp.minimum((i + 1) * hb, n_halo - 1), 0)),
        ]
        args += [arr, arr, arr]
    if want_lse:
        assert 2 * wq // LANES == SLAB
        if dil == 1:
            out_shape = jax.ShapeDtypeStruct((T * SLAB, LANES), F32)
            out_specs = pl.BlockSpec((ATT_Q * SLAB, LANES), lambda i: (i, 0))
        else:
            out_shape = jax.ShapeDtypeStruct((T // dil, dil * SLAB, LANES), F32)
            out_specs = pl.BlockSpec((ATT_Q, SLAB, LANES), out_map)
    else:
        assert dil == 1
        out_shape = jax.ShapeDtypeStruct((T, wq), BF16)
        out_specs = pl.BlockSpec((ATT_Q, wq), lambda i: (i, 0))
    kern = functools.partial(_band_attn_kernel, segs=segs, dil=dil, radius=radius, q_per_k=q_per_k,
                             has_sink=sink is not None, want_lse=want_lse)
    out = pl.pallas_call(
        kern, grid=(nt,), in_specs=in_specs, out_specs=out_specs, out_shape=out_shape,
        compiler_params=_cparams(("parallel",)),
    )(*args)
    return out.reshape(T * SLAB, LANES) if want_lse else out


def _split3(v):
    hi = v.astype(BF16)
    r1 = v - hi.astype(F32)
    mid = r1.astype(BF16)
    lo = (r1 - mid.astype(F32)).astype(BF16)
    return hi, mid, lo


def _route(x1, wr_ref, br_ref):
    hi, mid, _ = _split3(x1)
    xs = jnp.concatenate([hi, hi, mid], axis=1)
    lt = lax.dot_general(wr_ref[...], xs, (((1,), (1,)), ((), ())), preferred_element_type=F32) + br_ref[...]
    n_tok = x1.shape[0]
    row = lax.broadcasted_iota(jnp.int32, (MOE_EPG, n_tok), 0)
    big = jnp.int32(MOE_EPG)

    def first_argmax(v):
        m = jnp.max(v, axis=0, keepdims=True)
        return m, jnp.min(jnp.where(v == m, row, big), axis=0, keepdims=True)

    n_e = MOE_GROUPS * MOE_EPG
    gl = lt[n_e:n_e + MOE_EPG]
    gmax, gidx = first_argmax(gl)
    g_w = 1.0 / jnp.sum(jnp.exp(gl - gmax), axis=0, keepdims=True)
    el = lt[(MOE_GROUPS - 1) * MOE_EPG:n_e]
    for g in range(MOE_GROUPS - 2, -1, -1):
        el = jnp.where(gidx == g, lt[g * MOE_EPG:(g + 1) * MOE_EPG], el)
    v1, i1 = first_argmax(el)
    el2 = jnp.where(row == i1, -jnp.inf, el)
    v2, i2 = first_argmax(el2)
    ex = jnp.exp(v2 - v1)
    w1 = 1.0 / (1.0 + ex)
    w2 = ex * w1
    first_lo = i1 < i2
    e_lo = jnp.minimum(i1, i2).astype(F32)
    e_hi = jnp.maximum(i1, i2).astype(F32)
    c_lo = g_w * jnp.where(first_lo, w1, w2)
    c_hi = g_w * jnp.where(first_lo, w2, w1)
    pad = jnp.zeros((ROUTE_ROWS - 5, n_tok), F32)
    return jnp.concatenate([gidx.astype(F32), e_lo, e_hi, c_lo, c_hi, pad], axis=0)


def _rows_to_lanes(rows, sel):
    pieces = jnp.concatenate(_split3(rows), axis=0)
    return lax.dot_general(pieces, sel, (((0,), (0,)), ((), ())), preferred_element_type=F32)


def _gelu_tanh(x):
    return 0.5 * x * (1.0 + jnp.tanh(0.7978845608028654 * (x + 0.044715 * x * x * x)))


def _silu(x):
    return x / (1.0 + jnp.exp(-x))


def _out_ln_kernel(*refs, mode, n_h, n_x, p_tiles):
    h_refs = refs[:n_h]
    extra_ref = refs[n_h] if mode == "gla" else None
    base = n_h + (1 if mode == "gla" else 0)
    x_refs = refs[base:base + n_x]
    wo_ref, g_ref, b_ref, wr_ref, br_ref, x1_ref, rt_ref = refs[base + n_x:base + n_x + 7]

    if mode == "plain":
        hb = h_refs[0][...]
    elif mode == "rglru":
        gate, hf, hbw = h_refs
        hb = (_gelu_tanh(gate[...].astype(F32)) * (hf[...].astype(F32) + hbw[...].astype(F32))).astype(BF16)
    elif mode == "gla":
        of, ob, gg = h_refs
        o = of[...] + ob[...]
        parts = []
        for h in range(GLA_HEADS):
            oh = o[:, h * GLA_DVH:(h + 1) * GLA_DVH]
            ms = jnp.mean(oh * oh, axis=-1, keepdims=True)
            parts.append(oh * lax.rsqrt(ms + LN_EPS) * extra_ref[...])
        o = jnp.concatenate(parts, axis=1)
        hb = (o * _silu(gg[...].astype(F32))).astype(BF16)
    else:
        half = SLAB // 2
        os_ = [_slab_read(r, range(half)) for r in h_refs]
        ls_ = [_slab_read(r, range(half, SLAB)) for r in h_refs]
        mx = functools.reduce(jnp.maximum, ls_)
        es = [jnp.exp(l - mx) for l in ls_]
        den = functools.reduce(jnp.add, es)
        o = functools.reduce(jnp.add, [(e / den) * ov for e, ov in zip(es, os_)])
        hb = o.astype(BF16)

    acc = jnp.dot(hb, wo_ref[...], preferred_element_type=F32)
    x1 = _layer_norm(ALPHA * _x_read(x_refs, p_tiles) + acc, g_ref[...], b_ref[...])
    _slab_write(x1_ref, x1)
    rt_ref[...] = _route(x1, wr_ref, br_ref)


def _out_ln(hs, x, T, wo, ln_g, ln_b, wr, br, *, mode, extra=None):
    D = D_MODEL
    tm = TOK_TILE
    rows = lambda h: tm * SLAB if mode == "dil" else tm
    in_specs = [pl.BlockSpec((rows(h), h.shape[1]), lambda i: (i, 0)) for h in hs]
    args = list(hs)
    if mode == "gla":
        in_specs.append(pl.BlockSpec((1, extra.shape[1]), lambda i: (0, 0)))
        args.append(extra)
    x_specs, x_args, p_tiles = _x_specs(x, tm)
    in_specs += x_specs + [
        pl.BlockSpec(wo.shape, lambda i: (0, 0)),
        pl.BlockSpec((1, D), lambda i: (0, 0)),
        pl.BlockSpec((1, D), lambda i: (0, 0)),
        pl.BlockSpec(wr.shape, lambda i: (0, 0)),
        pl.BlockSpec(br.shape, lambda i: (0, 0)),
    ]
    args += x_args + [wo, ln_g, ln_b, wr, br]
    return pl.pallas_call(
        functools.partial(_out_ln_kernel, mode=mode, n_h=len(hs), n_x=len(x_args), p_tiles=p_tiles),
        grid=(T // tm,),
        in_specs=in_specs,
        out_specs=[pl.BlockSpec((tm * SLAB, LANES), lambda i: (i, 0)),
                   pl.BlockSpec((ROUTE_ROWS, tm), lambda i: (0, i))],
        out_shape=[jax.ShapeDtypeStruct((T * SLAB, LANES), F32),
                   jax.ShapeDtypeStruct((ROUTE_ROWS, T), F32)],
        compiler_params=_cparams(("parallel",)),
    )(*args)


def _moe_kernel(tg_ref, tlo_ref, thi_ref, nv_ref, base_ref, order_ref, x_hbm, c_hbm, sel_ref, wgl_ref, wul_ref,
                wdl_ref, wgh_ref, wuh_ref, wdh_ref, g_ref, b_ref, out_hbm, xbuf, obuf, cbuf, cslab, gsem, ssem,
                csem, *, nt, n_tok):
    i = pl.program_id(0)
    tm = xbuf.shape[1] // SLAB
    win = cbuf.shape[2]
    slot = lax.rem(i, 2)
    used = nv_ref[i] > 0
    nxt = jnp.minimum(i + 1, nt - 1)
    next_used = jnp.logical_and(i + 1 < nt, nv_ref[nxt] > 0)

    def token(ref, tok):
        return ref.at[pl.ds(pl.multiple_of(tok * SLAB, SLAB), SLAB)]

    def weights_copy(tile, sl):
        a = pl.multiple_of(jnp.bitwise_and(base_ref[tile], -LANES), LANES)
        return pltpu.make_async_copy(c_hbm.at[:, pl.ds(a, win)], cbuf.at[sl], csem.at[sl])

    def gather_start(tile, sl):
        base, last = base_ref[tile], nv_ref[tile] - 1
        for r in range(tm):
            tok = order_ref[base + jnp.minimum(r, last)]
            pltpu.make_async_copy(token(x_hbm, tok), xbuf.at[sl, pl.ds(r * SLAB, SLAB)], gsem.at[sl]).start()
        weights_copy(tile, sl).start()

    def gather_wait(tile, sl):
        pltpu.make_async_copy(x_hbm.at[pl.ds(0, tm * SLAB)], xbuf.at[sl], gsem.at[sl]).wait()
        weights_copy(tile, sl).wait()

    def scatter_start(tile, sl):
        base, n = base_ref[tile], nv_ref[tile]
        dump = n_tok + sl * tm
        for r in range(tm):
            tok = jnp.where(r < n, order_ref[base + jnp.minimum(r, n - 1)], dump + r)
            pltpu.make_async_copy(obuf.at[sl, pl.ds(r * SLAB, SLAB)], token(out_hbm, tok), ssem.at[sl]).start()

    def scatter_wait(sl):
        pltpu.make_async_copy(obuf.at[sl], out_hbm.at[pl.ds(0, tm * SLAB)], ssem.at[sl]).wait()

    @pl.when(i == 0)
    def _():
        obuf[...] = jnp.zeros_like(obuf)
        for sl in range(2):
            cp = pltpu.make_async_copy(obuf.at[sl], out_hbm.at[pl.ds((n_tok + sl * tm) * SLAB, tm * SLAB)],
                                       ssem.at[sl])
            cp.start()
            cp.wait()

    @pl.when(jnp.logical_and(i == 0, used))
    def _():
        gather_start(0, 0)

    @pl.when(used)
    def _():
        gather_wait(i, slot)

        @pl.when(next_used)
        def _():
            gather_start(i + 1, 1 - slot)

        @pl.when(i >= 2)
        def _():
            scatter_wait(slot)

        x = _slab_read(xbuf.at[slot])
        cslab[...] = _rows_to_lanes(cbuf[slot], sel_ref[...])
        cw = cslab[pl.ds(jnp.bitwise_and(base_ref[i], LANES - 1), tm), :]
        xb = x.astype(BF16)
        acc = jnp.zeros((tm, D_MODEL), F32)
        for wg_ref, wu_ref, wd_ref, lane in ((wgl_ref, wul_ref, wdl_ref, 0), (wgh_ref, wuh_ref, wdh_ref, 1)):
            hg = jnp.dot(xb, wg_ref[0, 0, 0], preferred_element_type=F32)
            hu = jnp.dot(xb, wu_ref[0, 0, 0], preferred_element_type=F32)
            h = _silu(hg) * hu * cw[:, lane:lane + 1]
            acc = acc + jnp.dot(h.astype(BF16), wd_ref[0, 0, 0], preferred_element_type=F32)
        _slab_write(obuf.at[slot], _layer_norm(ALPHA * x + acc, g_ref[...], b_ref[...]))
        scatter_start(i, slot)

        @pl.when(jnp.logical_not(next_used))
        def _():
            @pl.when(i >= 1)
            def _():
                scatter_wait(1 - slot)
            scatter_wait(slot)


_PAIR_LO = tuple(a for a in range(MOE_EPG) for b in range(a + 1, MOE_EPG))
_PAIR_HI = tuple(b for a in range(MOE_EPG) for b in range(a + 1, MOE_EPG))
N_PAIRS = len(_PAIR_LO)
N_CLASSES = MOE_GROUPS * N_PAIRS


def _moe_schedule(rt, tm, nt):
    T = rt.shape[1]
    i32 = jnp.int32
    g, lo, hi = rt[0].astype(i32), rt[1].astype(i32), rt[2].astype(i32)
    cls = g * N_PAIRS + lo * (2 * MOE_EPG - 1 - lo) // 2 + (hi - lo - 1)
    w_lo, w_hi = rt[3], rt[4]
    cls_sorted, order, w_lo, w_hi = lax.sort((cls, jnp.arange(T, dtype=i32), w_lo, w_hi), num_keys=1,
                                             is_stable=True)
    starts = jnp.searchsorted(cls_sorted, jnp.arange(N_CLASSES + 1, dtype=i32), side="left",
                              method="compare_all").astype(i32)
    counts = starts[1:] - starts[:-1]
    tiles_per = (counts + tm - 1) // tm
    cum = jnp.cumsum(tiles_per)
    n_used = cum[-1]
    t = jnp.arange(nt, dtype=i32)
    tc = jnp.minimum(t, n_used - 1)
    tcls = jnp.searchsorted(cum, tc, side="right", method="compare_all").astype(i32)
    k = tc - (cum[tcls] - tiles_per[tcls])
    nvalid = jnp.where(t < n_used, jnp.clip(counts[tcls] - k * tm, 0, tm), 0).astype(i32)
    base = (starts[tcls] + k * tm).astype(i32)
    pr = tcls % N_PAIRS
    tile_lo = jnp.asarray(_PAIR_LO, i32)[pr]
    tile_hi = jnp.asarray(_PAIR_HI, i32)[pr]
    c_sorted = jnp.zeros((ROUTE_ROWS, T + tm + LANES), F32).at[0, :T].set(w_lo).at[1, :T].set(w_hi)
    return (tcls // N_PAIRS).astype(i32), tile_lo, tile_hi, nvalid, base, order, c_sorted


def _moe(x1, rt, layer, wg, wu, wd, ln_g, ln_b):
    T = x1.shape[0] // SLAB
    D = D_MODEL
    tm = MOE_TILE
    nt = T // tm + N_CLASSES
    win = tm + LANES
    *sched, c_sorted = _moe_schedule(rt, tm, nt)
    sel = jnp.tile(jnp.eye(ROUTE_ROWS, LANES, dtype=BF16), (3, 1))
    lo_map = lambda i, tg, tlo, thi, nv, base, order: (layer, tg[i], tlo[i], 0, 0)
    hi_map = lambda i, tg, tlo, thi, nv, base, order: (layer, tg[i], thi[i], 0, 0)
    const = lambda i, tg, tlo, thi, nv, base, order: (0, 0)
    grid_spec = pltpu.PrefetchScalarGridSpec(
        num_scalar_prefetch=6,
        grid=(nt,),
        in_specs=[
            pl.BlockSpec(memory_space=pl.ANY),
            pl.BlockSpec(memory_space=pl.ANY),
            pl.BlockSpec(sel.shape, const),
            pl.BlockSpec((1, 1, 1, D, MOE_FF), lo_map),
            pl.BlockSpec((1, 1, 1, D, MOE_FF), lo_map),
            pl.BlockSpec((1, 1, 1, MOE_FF, D), lo_map),
            pl.BlockSpec((1, 1, 1, D, MOE_FF), hi_map),
            pl.BlockSpec((1, 1, 1, D, MOE_FF), hi_map),
            pl.BlockSpec((1, 1, 1, MOE_FF, D), hi_map),
            pl.BlockSpec((1, D), const),
            pl.BlockSpec((1, D), const),
        ],
        out_specs=pl.BlockSpec(memory_space=pl.ANY),
        scratch_shapes=[pltpu.VMEM((2, tm * SLAB, LANES), F32), pltpu.VMEM((2, tm * SLAB, LANES), F32),
                        pltpu.VMEM((2, ROUTE_ROWS, win), F32), pltpu.VMEM((win, LANES), F32),
                        pltpu.SemaphoreType.DMA((2,)), pltpu.SemaphoreType.DMA((2,)),
                        pltpu.SemaphoreType.DMA((2,))],
    )
    return pl.pallas_call(
        functools.partial(_moe_kernel, nt=nt, n_tok=T),
        grid_spec=grid_spec,
        out_shape=jax.ShapeDtypeStruct(((T + 2 * tm) * SLAB, LANES), F32),
        compiler_params=_cparams(("arbitrary",)),
    )(*sched, x1, c_sorted, sel, wg, wu, wd, wg, wu, wd, ln_g, ln_b)


def _rglru_stream(z, u_ref, up_ref, un_ref, w_ref, cw_ref, cb_ref, ba_ref, bi_ref, lam_ref,
                  h_out_ref, a_s, b_s, h_s, carry_ref, prev_ok, next_ok, reset):
    tt = u_ref.shape[0]
    u_mid = u_ref[...]
    up = jnp.where(prev_ok, up_ref[...], 0.0)
    un = jnp.where(next_ok, un_ref[...], 0.0)
    head = jnp.concatenate([up, u_mid[:SUBLANES]], axis=0)
    tail = jnp.concatenate([u_mid[tt - SUBLANES:], un], axis=0)
    left = RG_CONV // 2
    u = cb_ref[...]
    for kk in range(RG_CONV):
        s = kk - left
        if s == 0:
            shifted = u_mid
        else:
            rolled = pltpu.roll(u_mid, (-s) % tt, 0)
            if s < 0:
                shifted = jnp.concatenate([head[SUBLANES + s:2 * SUBLANES + s], rolled[SUBLANES:]], axis=0)
            else:
                shifted = jnp.concatenate([rolled[:tt - SUBLANES], tail[s:SUBLANES + s]], axis=0)
        u = u + cw_ref[kk:kk + 1, :] * shifted
    ub = u.astype(BF16)
    sp = jnp.maximum(-lam_ref[...], 0.0) + jnp.log(1.0 + jnp.exp(-jnp.abs(lam_ref[...])))
    for n in range(RG_NT):
        s0 = min(max(n - 1, 0), RG_NT - 3) * LANES
        zz = jnp.dot(ub[:, s0:s0 + RG_BAND], w_ref[n], preferred_element_type=F32)
        sl = slice(n * LANES, (n + 1) * LANES)
        r = jax.nn.sigmoid(zz[:, :LANES] + ba_ref[:, sl])
        ig = jax.nn.sigmoid(zz[:, LANES:] + bi_ref[:, sl])
        log_a = -RG_C * r * sp[:, sl]
        a = jnp.exp(log_a)
        a_s[:, sl] = a
        b_s[:, sl] = jnp.sqrt(-jnp.tanh(log_a) * (a * a + 1.0)) * ig * u[:, sl]

    @pl.when(reset)
    def _():
        carry_ref[...] = jnp.zeros_like(carry_ref)

    n_grp = tt // SUBLANES

    def body(gi, h):
        g = gi if z == 0 else n_grp - 1 - gi
        base = pl.multiple_of(g * SUBLANES, SUBLANES)
        a8 = a_s[pl.ds(base, SUBLANES), :]
        b8 = b_s[pl.ds(base, SUBLANES), :]
        rows = [None] * SUBLANES
        order = range(SUBLANES) if z == 0 else range(SUBLANES - 1, -1, -1)
        for r_ in order:
            h = a8[r_:r_ + 1] * h + b8[r_:r_ + 1]
            rows[r_] = h
        h_s[pl.ds(base, SUBLANES), :] = jnp.concatenate(rows, axis=0)
        return h

    h_last = lax.fori_loop(0, n_grp, body, carry_ref[...])
    carry_ref[...] = h_last
    h_out_ref[...] = h_s[...].astype(h_out_ref.dtype)


def _rglru_scan_kernel(uf_ref, ufp_ref, ufn_ref, ubk_ref, ubp_ref, ubn_ref, wf_ref, wb_ref, cw_ref, cb_ref,
                       ba_ref, bi_ref, lam_ref, hf_ref, hb_ref, a_s, b_s, h_s, cf_ref, cbk_ref, *, segs, nt):
    i = pl.program_id(0)
    tt = uf_ref.shape[0]
    _, tps, j = _seq_pos(i, tt, segs)
    _rglru_stream(0, uf_ref, ufp_ref, ufn_ref, wf_ref, cw_ref, cb_ref, ba_ref.at[0:1], bi_ref.at[0:1],
                  lam_ref.at[0:1], hf_ref, a_s, b_s, h_s, cf_ref, j > 0, j < tps - 1, j == 0)
    ib = nt - 1 - i
    _, tps_b, jb = _seq_pos(ib, tt, segs)
    _rglru_stream(1, ubk_ref, ubp_ref, ubn_ref, wb_ref, cw_ref, cb_ref, ba_ref.at[1:2], bi_ref.at[1:2],
                  lam_ref.at[1:2], hb_ref, a_s, b_s, h_s, cbk_ref, jb > 0, jb < tps_b - 1, jb == tps_b - 1)


def _rglru_scan(u_pre, w_band, conv_w, conv_b, ba, bi, lam, segs):
    T, W = u_pre.shape
    tt = SCAN_TILE
    nt = T // tt
    hpt = tt // SUBLANES
    n_h = T // SUBLANES
    fwd = lambda i: (i, 0)
    bwd = lambda i: (nt - 1 - i, 0)

    def halo_specs(idx):
        return [
            pl.BlockSpec((tt, W), lambda i: (idx(i), 0)),
            pl.BlockSpec((SUBLANES, W), lambda i: (jnp.maximum(idx(i) * hpt - 1, 0), 0)),
            pl.BlockSpec((SUBLANES, W), lambda i: (jnp.minimum((idx(i) + 1) * hpt, n_h - 1), 0)),
        ]

    const2 = lambda i: (0, 0)
    const3 = lambda i: (0, 0, 0)
    in_specs = halo_specs(lambda i: i) + halo_specs(lambda i: nt - 1 - i) + [
        pl.BlockSpec(w_band.shape[1:], const3),
        pl.BlockSpec(w_band.shape[1:], const3),
        pl.BlockSpec(conv_w.shape, const2),
        pl.BlockSpec(conv_b.shape, const2),
        pl.BlockSpec(ba.shape, const2),
        pl.BlockSpec(bi.shape, const2),
        pl.BlockSpec(lam.shape, const2),
    ]
    return pl.pallas_call(
        functools.partial(_rglru_scan_kernel, segs=segs, nt=nt),
        grid=(nt,),
        in_specs=in_specs,
        out_specs=[pl.BlockSpec((tt, W), fwd), pl.BlockSpec((tt, W), bwd)],
        out_shape=[jax.ShapeDtypeStruct((T, W), BF16), jax.ShapeDtypeStruct((T, W), BF16)],
        scratch_shapes=[pltpu.VMEM((tt, W), F32), pltpu.VMEM((tt, W), F32), pltpu.VMEM((tt, W), F32),
                        pltpu.VMEM((1, W), F32), pltpu.VMEM((1, W), F32)],
        compiler_params=_cparams(("arbitrary",)),
    )(u_pre, u_pre, u_pre, u_pre, u_pre, u_pre, w_band[0], w_band[1], conv_w, conv_b, ba, bi, lam)


def _rglru_band_weights(wa, wi):
    def dense(w):
        eye = jnp.eye(RG_BLOCKS, dtype=w.dtype)
        return jnp.einsum("ncd,nm->ncmd", w, eye).reshape(RG_WIDTH, RG_WIDTH)

    out = []
    for z in range(2):
        da, di = dense(wa[z]), dense(wi[z])
        tiles = []
        for n in range(RG_NT):
            s0 = min(max(n - 1, 0), RG_NT - 3) * LANES
            sl = slice(n * LANES, (n + 1) * LANES)
            tiles.append(jnp.concatenate([da[s0:s0 + RG_BAND, sl], di[s0:s0 + RG_BAND, sl]], axis=1))
        out.append(jnp.stack(tiles))
    return jnp.stack(out).astype(BF16)


def _gla_stream(z, q_ref, k_ref, v_ref, lr_ref, wa2_ref, ba_ref, o_ref, st_ref, reset):
    tt = q_ref.shape[0]
    C = GLA_CHUNK

    @pl.when(reset)
    def _():
        st_ref[z] = jnp.zeros(st_ref.shape[1:], F32)

    zz = jnp.dot(lr_ref[...].astype(BF16), wa2_ref[:, z * GLA_DK:(z + 1) * GLA_DK],
                 preferred_element_type=F32) + ba_ref[:, z * GLA_DK:(z + 1) * GLA_DK]
    log_a = -(jnp.maximum(-zz, 0.0) + jnp.log(1.0 + jnp.exp(-jnp.abs(zz)))) / GLA_TAU
    ri = lax.broadcasted_iota(jnp.int32, (C, C), 0)
    ci = lax.broadcasted_iota(jnp.int32, (C, C), 1)
    causal = (ri >= ci) if z == 0 else (ri <= ci)
    tri = causal.astype(F32)
    mid = C // 2 if z == 0 else C - 1 - C // 2
    last = C - 1 if z == 0 else 0
    scale = GLA_DKH ** -0.5
    chunks = range(tt // C) if z == 0 else range(tt // C - 1, -1, -1)
    for c in chunks:
        rs = slice(c * C, (c + 1) * C)
        b = jnp.dot(tri, log_a[rs], preferred_element_type=F32, precision=lax.Precision.HIGHEST)
        b_mid = b[mid:mid + 1]
        b_last = b[last:last + 1]
        qc = q_ref[rs, :] * scale
        kc = k_ref[rs, :]
        qd = (qc * jnp.exp(b - b_mid)).astype(BF16)
        kd = (kc * jnp.exp(b_mid - b)).astype(BF16)
        ks = (kc * jnp.exp(b_last - b)).astype(BF16)
        qb = (qc * jnp.exp(b)).astype(BF16)
        dec = jnp.exp(b_last)
        for h in range(GLA_HEADS):
            ksl = slice(h * GLA_DKH, (h + 1) * GLA_DKH)
            vsl = slice(h * GLA_DVH, (h + 1) * GLA_DVH)
            vc = v_ref[rs, vsl]
            att = lax.dot_general(qd[:, ksl], kd[:, ksl], (((1,), (1,)), ((), ())), preferred_element_type=F32)
            att = jnp.where(causal, att, 0.0)
            o = jnp.dot(att.astype(BF16), vc, preferred_element_type=F32)
            st = st_ref[z, h]
            o = o + lax.dot_general(qb[:, ksl], st.astype(BF16), (((1,), (1,)), ((), ())),
                                    preferred_element_type=F32)
            o_ref[rs, vsl] = o
            upd = lax.dot_general(vc, ks[:, ksl], (((0,), (0,)), ((), ())), preferred_element_type=F32)
            st_ref[z, h] = st * dec[:, ksl] + upd


def _gla_kernel(qf, kf, vf, lf, qb, kb, vb, lb, wa2_ref, ba_ref, of_ref, ob_ref, st_ref, *, segs, nt):
    i = pl.program_id(0)
    tt = qf.shape[0]
    _, tps, j = _seq_pos(i, tt, segs)
    _gla_stream(0, qf, kf, vf, lf, wa2_ref, ba_ref, of_ref, st_ref, j == 0)
    ib = nt - 1 - i
    _, tps_b, jb = _seq_pos(ib, tt, segs)
    _gla_stream(1, qb, kb, vb, lb, wa2_ref, ba_ref, ob_ref, st_ref, jb == tps_b - 1)


def _gla(q, k, v, lr, wa2, ba, segs):
    T = q.shape[0]
    tt = SCAN_TILE
    nt = T // tt
    fwd = lambda i: (i, 0)
    bwd = lambda i: (nt - 1 - i, 0)
    in_specs = []
    for idx in (fwd, bwd):
        in_specs += [pl.BlockSpec((tt, GLA_DK), idx), pl.BlockSpec((tt, GLA_DK), idx),
                     pl.BlockSpec((tt, GLA_DV), idx), pl.BlockSpec((tt, LANES), idx)]
    in_specs += [pl.BlockSpec(wa2.shape, lambda i: (0, 0)), pl.BlockSpec(ba.shape, lambda i: (0, 0))]
    return pl.pallas_call(
        functools.partial(_gla_kernel, segs=segs, nt=nt),
        grid=(nt,),
        in_specs=in_specs,
        out_specs=[pl.BlockSpec((tt, GLA_DV), fwd), pl.BlockSpec((tt, GLA_DV), bwd)],
        out_shape=[jax.ShapeDtypeStruct((T, GLA_DV), F32), jax.ShapeDtypeStruct((T, GLA_DV), F32)],
        scratch_shapes=[pltpu.VMEM((2, GLA_HEADS, GLA_DVH, GLA_DKH), F32)],
        compiler_params=_cparams(("arbitrary",)),
    )(q, k, v, lr, q, k, v, lr, wa2, ba)


def _rope_tables(s_max):
    half = HEAD_DIM // 2
    inv = ROPE_THETA ** (-jnp.arange(half, dtype=F32) / half)
    ang = jnp.arange(s_max, dtype=F32)[:, None] * inv[None, :]
    cos, sin = jnp.cos(ang), jnp.sin(ang)
    cos_t = jnp.tile(jnp.concatenate([cos, cos], axis=1), (1, LANES // HEAD_DIM))
    sin_t = jnp.tile(jnp.concatenate([-sin, sin], axis=1), (1, LANES // HEAD_DIM))
    return cos_t, sin_t


def _router_weights(wgr, bgr, wer, ber):
    n_e = MOE_GROUPS * MOE_EPG
    w = jnp.concatenate([jnp.transpose(wer, (0, 2, 1)).reshape(n_e, D_MODEL), wgr.T], axis=0)
    b = jnp.concatenate([ber.reshape(n_e), bgr])
    n_pad = ROUTER_ROWS - w.shape[0]
    w = jnp.pad(w, ((0, n_pad), (0, 0)))
    b = jnp.concatenate([b, jnp.full((n_pad,), NEG_INF, F32)])
    w_hi = w.astype(BF16)
    w_mid = (w - w_hi.astype(F32)).astype(BF16)
    return jnp.concatenate([w_hi, w_mid, w_hi], axis=1), b[:, None]


def _mixer(kind, jl, x, segs, p, cos_t, sin_t, lg1, lb1, wr, br):
    D = D_MODEL
    T = sum(n * s for n, s in segs)
    if kind == 0:
        nq = A_HEADS * HEAD_DIM
        nkv = A_KV_HEADS * HEAD_DIM
        w = p["a_wqkv"][jl]
        wq, wk, wv = w[:, :nq], w[:, nq:nq + nkv], w[:, nq + nkv:]
        dup = lambda t: jnp.repeat(t.reshape(D, A_KV_HEADS, 1, HEAD_DIM), 2, axis=2).reshape(D, 2 * nkv)
        w_ext = jnp.concatenate([wq, dup(wk), dup(wv)], axis=1).astype(BF16)
        q, kd, vd = _proj_rope(x, T, w_ext, cos_t, sin_t, segs, dil=1, n_rope=(nq + 2 * nkv) // LANES,
                               scale_cols=nq, scale=HEAD_DIM ** -0.5, out_cols=(nq, 2 * nkv, 2 * nkv),
                               out_dtypes=(BF16, BF16, BF16))
        o = _band_attn(q, kd, vd, segs, dil=1, radius=A_RADIUS, q_per_k=2, sink=p["a_sink"][jl])
        return _out_ln([o], x, T, p["a_wo"][jl].astype(BF16), lg1, lb1, wr, br, mode="plain")
    if kind == 1:
        gate, u_pre = _proj(x, T, p["b_win"][jl].astype(BF16), (RG_WIDTH, RG_WIDTH), (BF16, F32))
        w_band = _rglru_band_weights(p["b_wa"][jl], p["b_wi"][jl])
        hf, hb = _rglru_scan(u_pre, w_band, p["b_conv_w"][jl], p["b_conv_b"][jl][None, :], p["b_ba"][jl],
                             p["b_bi"][jl], p["b_lam"][jl], segs)
        return _out_ln([gate, hf, hb], x, T, p["b_wo"][jl].astype(BF16), lg1, lb1, wr, br, mode="rglru")
    if kind == 2:
        wa1 = jnp.concatenate([p["c_wa1"][jl][0], p["c_wa1"][jl][1]], axis=1)
        wa1 = jnp.pad(wa1, ((0, 0), (0, LANES - wa1.shape[1])))
        w_all = jnp.concatenate([p["c_wqkvg"][jl], wa1], axis=1).astype(BF16)
        q, k, v, g, lr = _proj(x, T, w_all, (GLA_DK, GLA_DK, GLA_DV, GLA_DV, LANES), (F32, F32, BF16, BF16, F32))
        wa2 = jnp.zeros((LANES, 2 * GLA_DK), F32)
        wa2 = wa2.at[:GLA_RANK, :GLA_DK].set(p["c_wa2"][jl][0])
        wa2 = wa2.at[GLA_RANK:2 * GLA_RANK, GLA_DK:].set(p["c_wa2"][jl][1])
        of, ob = _gla(q, k, v, lr, wa2.astype(BF16), p["c_ba"][jl].reshape(1, 2 * GLA_DK), segs)
        return _out_ln([of, ob, g], x, T, p["c_wo"][jl].astype(BF16), lg1, lb1, wr, br, mode="gla",
                       extra=p["c_norm_g"][jl][None, :])
    ols = []
    for gi, (window, dil) in enumerate(DIL_GROUPS):
        w_g = p["d_wqkv"][jl][:, gi * 3 * DIL_WIDTH:(gi + 1) * 3 * DIL_WIDTH].astype(BF16)
        q, k, v = _proj_rope(x, T, w_g, cos_t, sin_t, segs, dil=dil, n_rope=2 * DIL_WIDTH // LANES,
                             scale_cols=DIL_WIDTH, scale=HEAD_DIM ** -0.5,
                             out_cols=(DIL_WIDTH,) * 3, out_dtypes=(BF16,) * 3)
        ols.append(_band_attn(q, k, v, segs, dil=dil, radius=window // (2 * dil), q_per_k=1, want_lse=True))
    return _out_ln(ols, x, T, p["d_wo"][jl].astype(BF16), lg1, lb1, wr, br, mode="dil")


def kernel(x_prompt, x_sample, ln_g, ln_b, a_wqkv, a_sink, a_wo, b_win, b_conv_w, b_conv_b, b_wa, b_ba, b_wi,
           b_bi, b_lam, b_wo, c_wqkvg, c_wa1, c_wa2, c_ba, c_norm_g, c_wo, d_wqkv, d_wo, m_wgr, m_bgr, m_wer,
           m_ber, m_wg, m_wu, m_wd):
    p = dict(a_wqkv=a_wqkv, a_sink=a_sink, a_wo=a_wo, b_win=b_win, b_conv_w=b_conv_w, b_conv_b=b_conv_b,
             b_wa=b_wa, b_ba=b_ba, b_wi=b_wi, b_bi=b_bi, b_lam=b_lam, b_wo=b_wo, c_wqkvg=c_wqkvg, c_wa1=c_wa1,
             c_wa2=c_wa2, c_ba=c_ba, c_norm_g=c_norm_g, c_wo=c_wo, d_wqkv=d_wqkv, d_wo=d_wo)
    n_p, s_p, D = x_prompt.shape
    n_s, s_s, _ = x_sample.shape
    segs = ((n_p, s_p), (n_s, s_s))
    assert (n_p * s_p) % s_s == 0
    t_p, T = n_p * s_p, n_p * s_p + n_s * s_s
    x = (x_prompt.reshape(t_p, D), x_sample.reshape(T - t_p, D))
    cos_t, sin_t = _rope_tables(max(s_p, s_s))
    wg_b, wu_b, wd_b = m_wg.astype(BF16), m_wu.astype(BF16), m_wd.astype(BF16)

    for layer in range(DEPTH):
        lg1, lb1 = ln_g[layer, 0][None, :], ln_b[layer, 0][None, :]
        lg2, lb2 = ln_g[layer, 1][None, :], ln_b[layer, 1][None, :]
        wr, br = _router_weights(m_wgr[layer], m_bgr[layer], m_wer[layer], m_ber[layer])
        x1, rt = _mixer(layer % 4, layer // 4, x, segs, p, cos_t, sin_t, lg1, lb1, wr, br)
        x = _moe(x1, rt, layer, wg_b, wu_b, wd_b, lg2, lb2)

    return _from_slab(x, 0, t_p).reshape(n_p, s_p, D), _from_slab(x, t_p, T - t_p).reshape(n_s, s_s, D)
```

```python
import functools

import jax
import jax.numpy as jnp
from jax import lax
from jax.experimental import pallas as pl
from jax.experimental.pallas import tpu as pltpu

F32 = jnp.float32
BF16 = jnp.bfloat16

D_MODEL = 1024
HEAD_DIM = 64
ROPE_THETA = 10000.0
A_HEADS = 16
A_KV_HEADS = 4
A_RADIUS = 128
RG_WIDTH = 1408
RG_BLOCKS = 16
RG_BW = RG_WIDTH // RG_BLOCKS
RG_CONV = 4
RG_C = 8.0
GLA_HEADS = 4
GLA_DK = 512
GLA_DV = 1024
GLA_DKH = 128
GLA_DVH = 256
GLA_RANK = 16
GLA_TAU = 16.0
GLA_CHUNK = 64
DIL_GROUPS = ((128, 1), (512, 4), (2048, 16))
N_DIL = 3
DIL_HEADS = 8
DIL_WIDTH = 512
MOE_GROUPS = 4
MOE_EPG = 8
MOE_FF = 256
DEPTH = 4
ALPHA = (2 * DEPTH) ** 0.25
LN_EPS = 1e-5
NEG_INF = -1e30

LANES = 128
SUBLANES = 8
VMEM_LIMIT = 52 * 1024 * 1024
ATT_Q = 128
ATT_STEP_BLOCKS = 2
TOK_TILE = 512
SCAN_TILE = 256
RG_NT = RG_WIDTH // LANES
RG_BAND = 3 * LANES
ROUTER_ROWS = 40
ROUTE_ROWS = 16
MOE_TILE = 128
SLAB = D_MODEL // LANES
PROJ_CHUNK = 512


def _cparams(sem):
    return pltpu.CompilerParams(dimension_semantics=sem, vmem_limit_bytes=VMEM_LIMIT)


def _seq_pos(i, rows, segs, dil=1):
    (n_p, s_p), (_, s_s) = segs
    p_tiles = (n_p * s_p) // rows
    in_p = i < p_tiles
    tps = jnp.where(in_p, (s_p // dil) // rows, (s_s // dil) // rows)
    j = lax.rem(i, tps)
    return in_p, tps, j


def _layer_norm(y, g, b):
    mu = jnp.mean(y, axis=-1, keepdims=True)
    yc = y - mu
    var = jnp.mean(yc * yc, axis=-1, keepdims=True)
    return yc * lax.rsqrt(var + LN_EPS) * g + b


def _rope_slab(t, cos, sin_signed):
    lane = lax.broadcasted_iota(jnp.int32, t.shape, 1)
    first_half = (lane % HEAD_DIM) < (HEAD_DIM // 2)
    partner = jnp.where(first_half, pltpu.roll(t, LANES - HEAD_DIM // 2, 1), pltpu.roll(t, HEAD_DIM // 2, 1))
    return t * cos + partner * sin_signed


def _slab_read(ref, chunks=range(SLAB)):
    if len(ref.shape) == 3:
        return jnp.concatenate([ref[:, c, :] for c in chunks], axis=1)
    rows = ref.shape[0] // SLAB
    return jnp.concatenate([ref[pl.ds(c, rows, stride=SLAB), :] for c in chunks], axis=1)


def _slab_write(ref, val, first_chunk=0, tok0=0):
    n = val.shape[0]
    for c in range(val.shape[1] // LANES):
        v = val[:, c * LANES:(c + 1) * LANES]
        if len(ref.shape) == 3:
            ref[tok0:tok0 + n, first_chunk + c, :] = v
        else:
            ref[pl.ds(tok0 * SLAB + first_chunk + c, n, stride=SLAB), :] = v


def _col_chunks(width):
    out, c = [], 0
    while c < width:
        w = min(PROJ_CHUNK, width - c)
        out.append((c, w))
        c += w
    return out


def _x_specs(x, tm):
    if isinstance(x, tuple):
        xp, xs = x
        p_tiles = xp.shape[0] // tm
        specs = [pl.BlockSpec((tm, D_MODEL), lambda i, *_: (jnp.minimum(i, p_tiles - 1), 0)),
                 pl.BlockSpec((tm, D_MODEL), lambda i, *_: (jnp.maximum(i - p_tiles, 0), 0))]
        return specs, [xp, xs], p_tiles
    return [pl.BlockSpec((tm * SLAB, LANES), lambda i, *_: (i, 0))], [x], 0


def _x_read(x_refs, p_tiles):
    if len(x_refs) == 2:
        return jnp.where(pl.program_id(0) < p_tiles, x_refs[0][...], x_refs[1][...])
    return _slab_read(x_refs[0])


def _proj_rope_kernel(*refs, n_x, p_tiles, n_rope, scale_cols, scale, out_cols):
    x_refs, (w_ref, cos_ref, sin_ref), out_refs = refs[:n_x], refs[n_x:n_x + 3], refs[n_x + 3:]
    xb = _x_read(x_refs, p_tiles).astype(BF16)
    cos = cos_ref[...]
    sin = sin_ref[...]
    col = 0
    for o_ref, width in zip(out_refs, out_cols):
        for c0, cw in _col_chunks(width):
            yc = jnp.dot(xb, w_ref[:, col + c0:col + c0 + cw], preferred_element_type=F32)
            for c in range(cw // LANES):
                slab = (col + c0) // LANES + c
                y = yc[:, c * LANES:(c + 1) * LANES]
                if slab < n_rope:
                    y = _rope_slab(y, cos, sin)
                if slab * LANES < scale_cols:
                    y = y * scale
                o_ref[:, c0 + c * LANES:c0 + (c + 1) * LANES] = y.astype(o_ref.dtype)
        col += width


def _proj_rope(x, T, w, cos_tab, sin_tab, segs, *, dil, n_rope, scale_cols, scale, out_cols, out_dtypes):
    D = D_MODEL
    (n_p, s_p), (n_s, s_s) = segs
    l_p, l_s = s_p // dil, s_s // dil
    tj = min(TOK_TILE, l_s, l_p)
    assert l_p % tj == 0 and l_s % tj == 0 and (n_p * s_p) % (dil * tj) == 0
    if dil == 1:
        x_specs, x_args, x_p_tiles = _x_specs(x, tj)
    else:
        assert x.shape[0] % (dil * SLAB) == 0
        x_args = [x.reshape(x.shape[0] // (dil * SLAB), dil * SLAB, LANES)]
        x_specs, x_p_tiles = [pl.BlockSpec((tj, SLAB, LANES), lambda jt, r: (jt, r, 0))], 0
    cosv = cos_tab.reshape(cos_tab.shape[0] // dil, dil * LANES)
    sinv = sin_tab.reshape(sin_tab.shape[0] // dil, dil * LANES)
    p_tiles = (n_p * l_p) // tj

    def decode(jt):
        in_p = jt < p_tiles
        lt = jnp.where(in_p, l_p // tj, l_s // tj)
        jt_loc = jnp.where(in_p, jt, jt - p_tiles)
        b = jt_loc // lt
        j0 = lax.rem(jt_loc, lt)
        return in_p, lt, b, j0

    def out_map(jt, r):
        in_p, lt, b, j0 = decode(jt)
        base = jnp.where(in_p, 0, (n_p * s_p) // tj)
        return (base + b * (lt * dil) + r * lt + j0, 0)

    def tab_map(jt, r):
        _, _, _, j0 = decode(jt)
        return (j0, r)

    kern = functools.partial(_proj_rope_kernel, n_x=len(x_args), p_tiles=x_p_tiles, n_rope=n_rope,
                             scale_cols=scale_cols, scale=scale, out_cols=out_cols)
    n_out = w.shape[1]
    return pl.pallas_call(
        kern,
        grid=(T // dil // tj, dil),
        in_specs=x_specs + [
            pl.BlockSpec((D, n_out), lambda jt, r: (0, 0)),
            pl.BlockSpec((tj, LANES), tab_map),
            pl.BlockSpec((tj, LANES), tab_map),
        ],
        out_specs=[pl.BlockSpec((tj, c), out_map) for c in out_cols],
        out_shape=[jax.ShapeDtypeStruct((T, c), dt) for c, dt in zip(out_cols, out_dtypes)],
        compiler_params=_cparams(("parallel", "parallel")),
    )(*x_args, w, cosv, sinv)


def _from_slab_kernel(x_ref, o_ref):
    o_ref[...] = _slab_read(x_ref)


def _from_slab(x, tok0, n_tok):
    tm = TOK_TILE
    assert tok0 % tm == 0 and n_tok % tm == 0
    return pl.pallas_call(
        _from_slab_kernel,
        grid=(n_tok // tm,),
        in_specs=[pl.BlockSpec((tm * SLAB, LANES), lambda i: (tok0 // tm + i, 0))],
        out_specs=pl.BlockSpec((tm, D_MODEL), lambda i: (i, 0)),
        out_shape=jax.ShapeDtypeStruct((n_tok, D_MODEL), F32),
        compiler_params=_cparams(("parallel",)),
    )(x)


def _proj_kernel(x_ref, w_ref, *out_refs, out_cols):
    xb = _slab_read(x_ref).astype(BF16)
    col = 0
    for o_ref, width in zip(out_refs, out_cols):
        for c0, cw in _col_chunks(width):
            y = jnp.dot(xb, w_ref[:, col + c0:col + c0 + cw], preferred_element_type=F32)
            o_ref[:, c0:c0 + cw] = y.astype(o_ref.dtype)
        col += width


def _proj(x, T, w, out_cols, out_dtypes):
    D = D_MODEL
    tm = TOK_TILE
    n_out = w.shape[1]
    assert sum(out_cols) == n_out and T % tm == 0
    return pl.pallas_call(
        functools.partial(_proj_kernel, out_cols=out_cols),
        grid=(T // tm,),
        in_specs=[pl.BlockSpec((tm * SLAB, LANES), lambda i: (i, 0)), pl.BlockSpec((D, n_out), lambda i: (0, 0))],
        out_specs=[pl.BlockSpec((tm, c), lambda i: (i, 0)) for c in out_cols],
        out_shape=[jax.ShapeDtypeStruct((T, c), dt) for c, dt in zip(out_cols, out_dtypes)],
        compiler_params=_cparams(("parallel",)),
    )(x, w)


def _band_attn_kernel(*refs, segs, dil, radius, q_per_k, has_sink, want_lse, n_sub):
    it = iter(refs)
    sink_ref = next(it) if has_sink else None
    q_ref = next(it)
    kp_ref, kc_ref, kn_ref = next(it), next(it), next(it)
    vp_ref, vc_ref, vn_ref = next(it), next(it), next(it)
    o_ref = next(it)
    n_q_slabs = q_ref.shape[1] // LANES
    qs = n_sub * ATT_Q

    i = pl.program_id(0)
    _, tps, j = _seq_pos(i, qs, segs, dil)
    W = 2 * radius + ATT_Q
    row = lax.broadcasted_iota(jnp.int32, (ATT_Q, W), 0)
    col = lax.broadcasted_iota(jnp.int32, (ATT_Q, W), 1)
    in_band = jnp.abs(col - radius - row) <= radius

    def block_bias(prev_ok, next_ok):
        ok = in_band
        if prev_ok is not None:
            ok = ok & ((col >= radius) | prev_ok)
        if next_ok is not None:
            ok = ok & ((col < radius + ATT_Q) | next_ok)
        bias = jnp.where(ok, 0.0, NEG_INF).astype(F32)
        return jnp.concatenate([bias] * q_per_k, axis=0) if q_per_k > 1 else bias

    biases = [block_bias(j > 0 if u == 0 else None, j < tps - 1 if u == n_sub - 1 else None)
              for u in range(n_sub)]

    k_all = jnp.concatenate([kp_ref[...], kc_ref[...], kn_ref[...]], axis=0)
    v_all = jnp.concatenate([vp_ref[...], vc_ref[...], vn_ref[...]], axis=0)
    nk = k_all.shape[1] // LANES
    lane_lo = lax.broadcasted_iota(jnp.int32, (1, LANES), 1) < HEAD_DIM
    M = ATT_Q * q_per_k
    zero = jnp.zeros((), BF16)

    for kc in range(nk):
        ksl = k_all[:, kc * LANES:(kc + 1) * LANES]
        vsl = v_all[:, kc * LANES:(kc + 1) * LANES]
        slabs = [kc * q_per_k + t for t in range(q_per_k)]
        masked = []
        for half in range(2):
            keep = lane_lo if half == 0 else jnp.logical_not(lane_lo)
            masked.append((keep, jnp.where(keep, ksl, zero), jnp.where(keep, vsl, zero)))
        for u in range(n_sub):
            r0 = u * ATT_Q
            qm = jnp.concatenate([q_ref[r0:r0 + ATT_Q, m * LANES:(m + 1) * LANES] for m in slabs], axis=0) \
                if q_per_k > 1 else q_ref[r0:r0 + ATT_Q, kc * LANES:(kc + 1) * LANES]
            o_acc = jnp.zeros((M, LANES), F32)
            lse_acc = jnp.zeros((M, LANES), F32)
            for half, (keep, kx_all, vx_all) in enumerate(masked):
                kx = kx_all[r0:r0 + W]
                vx = vx_all[r0:r0 + W]
                s = lax.dot_general(qm, kx, (((1,), (1,)), ((), ())), preferred_element_type=F32) + biases[u]
                mx = jnp.max(s, axis=1, keepdims=True)
                if has_sink:
                    sk = jnp.concatenate(
                        [jnp.full((ATT_Q, 1), sink_ref[2 * m + half], F32) for m in slabs], axis=0)
                    mx = jnp.maximum(mx, sk)
                p = jnp.exp(s - mx)
                l = jnp.sum(p, axis=1, keepdims=True)
                if has_sink:
                    l = l + jnp.exp(sk - mx)
                pv = jnp.dot(p.astype(BF16), vx, preferred_element_type=F32)
                o_acc = o_acc + pv / l
                if want_lse:
                    lse_acc = jnp.where(keep, mx + jnp.log(l), lse_acc)
            for t, m in enumerate(slabs):
                if want_lse:
                    _slab_write(o_ref, o_acc[t * ATT_Q:(t + 1) * ATT_Q], first_chunk=m, tok0=r0)
                    _slab_write(o_ref, lse_acc[t * ATT_Q:(t + 1) * ATT_Q], first_chunk=n_q_slabs + m, tok0=r0)
                else:
                    o_ref[r0:r0 + ATT_Q, m * LANES:(m + 1) * LANES] = \
                        o_acc[t * ATT_Q:(t + 1) * ATT_Q].astype(o_ref.dtype)


def _band_attn(q, k, v, segs, *, dil, radius, q_per_k, sink=None, want_lse=False):
    T, wq = q.shape
    wk = k.shape[1]
    (n_p, s_p), (n_s, s_s) = segs
    l_p, l_s = s_p // dil, s_s // dil
    n_sub = ATT_STEP_BLOCKS if l_p % (ATT_STEP_BLOCKS * ATT_Q) == 0 and l_s % (ATT_STEP_BLOCKS * ATT_Q) == 0 else 1
    qs = n_sub * ATT_Q
    assert qs % radius == 0 and l_p % qs == 0 and l_s % qs == 0
    hb = qs // radius
    n_halo = T // radius
    nt = T // qs
    p_tiles = (n_p * s_p) // qs

    def out_map(i):
        in_p = i < p_tiles
        nb = jnp.where(in_p, l_p // qs, l_s // qs)
        i_loc = jnp.where(in_p, i, i - p_tiles)
        n = i_loc // nb
        jb = lax.rem(i_loc, nb)
        b = n // dil
        r = lax.rem(n, dil)
        base = jnp.where(in_p, 0, p_tiles // dil)
        return (base + b * nb + jb, r, 0)

    in_specs = []
    args = []
    if sink is not None:
        in_specs.append(pl.BlockSpec(memory_space=pltpu.SMEM))
        args.append(sink)
    in_specs.append(pl.BlockSpec((qs, wq), lambda i: (i, 0)))
    args.append(q)
    for arr in (k, v):
        in_specs += [
            pl.BlockSpec((radius, wk), lambda i: (jnp.maximum(i * hb - 1, 0), 0)),
            pl.BlockSpec((qs, wk), lambda i: (i, 0)),
            pl.BlockSpec((radius, wk), lambda i: (jnp.minimum((i + 1) * hb, n_halo - 1), 0)),
        ]
        args += [arr, arr, arr]
    if want_lse:
        assert 2 * wq // LANES == SLAB
        if dil == 1:
            out_shape = jax.ShapeDtypeStruct((T * SLAB, LANES), F32)
            out_specs = pl.BlockSpec((qs * SLAB, LANES), lambda i: (i, 0))
        else:
            out_shape = jax.ShapeDtypeStruct((T // dil, dil * SLAB, LANES), F32)
            out_specs = pl.BlockSpec((qs, SLAB, LANES), out_map)
    else:
        assert dil == 1
        out_shape = jax.ShapeDtypeStruct((T, wq), BF16)
        out_specs = pl.BlockSpec((qs, wq), lambda i: (i, 0))
    kern = functools.partial(_band_attn_kernel, segs=segs, dil=dil, radius=radius, q_per_k=q_per_k,
                             has_sink=sink is not None, want_lse=want_lse, n_sub=n_sub)
    out = pl.pallas_call(
        kern, grid=(nt,), in_specs=in_specs, out_specs=out_specs, out_shape=out_shape,
        compiler_params=_cparams(("parallel",)),
    )(*args)
    return out.reshape(T * SLAB, LANES) if want_lse else out


def _split3(v):
    hi = v.astype(BF16)
    r1 = v - hi.astype(F32)
    mid = r1.astype(BF16)
    lo = (r1 - mid.astype(F32)).astype(BF16)
    return hi, mid, lo


def _route(x1, wr_ref, br_ref):
    hi, mid, _ = _split3(x1)
    xs = jnp.concatenate([hi, hi, mid], axis=1)
    lt = lax.dot_general(wr_ref[...], xs, (((1,), (1,)), ((), ())), preferred_element_type=F32) + br_ref[...]
    n_tok = x1.shape[0]
    row = lax.broadcasted_iota(jnp.int32, (MOE_EPG, n_tok), 0)
    big = jnp.int32(MOE_EPG)

    def first_argmax(v):
        m = jnp.max(v, axis=0, keepdims=True)
        return m, jnp.min(jnp.where(v == m, row, big), axis=0, keepdims=True)

    n_e = MOE_GROUPS * MOE_EPG
    gl = lt[n_e:n_e + MOE_EPG]
    gmax, gidx = first_argmax(gl)
    g_w = 1.0 / jnp.sum(jnp.exp(gl - gmax), axis=0, keepdims=True)
    el = lt[(MOE_GROUPS - 1) * MOE_EPG:n_e]
    for g in range(MOE_GROUPS - 2, -1, -1):
        el = jnp.where(gidx == g, lt[g * MOE_EPG:(g + 1) * MOE_EPG], el)
    v1, i1 = first_argmax(el)
    el2 = jnp.where(row == i1, -jnp.inf, el)
    v2, i2 = first_argmax(el2)
    ex = jnp.exp(v2 - v1)
    w1 = 1.0 / (1.0 + ex)
    w2 = ex * w1
    first_lo = i1 < i2
    e_lo = jnp.minimum(i1, i2).astype(F32)
    e_hi = jnp.maximum(i1, i2).astype(F32)
    c_lo = g_w * jnp.where(first_lo, w1, w2)
    c_hi = g_w * jnp.where(first_lo, w2, w1)
    pad = jnp.zeros((ROUTE_ROWS - 5, n_tok), F32)
    return jnp.concatenate([gidx.astype(F32), e_lo, e_hi, c_lo, c_hi, pad], axis=0)


def _rows_to_lanes(rows, sel):
    pieces = jnp.concatenate(_split3(rows), axis=0)
    return lax.dot_general(pieces, sel, (((0,), (0,)), ((), ())), preferred_element_type=F32)


def _gelu_tanh(x):
    return 0.5 * x * (1.0 + jnp.tanh(0.7978845608028654 * (x + 0.044715 * x * x * x)))


def _silu(x):
    return x / (1.0 + jnp.exp(-x))


def _out_ln_kernel(*refs, mode, n_h, n_x, p_tiles):
    h_refs = refs[:n_h]
    extra_ref = refs[n_h] if mode == "gla" else None
    base = n_h + (1 if mode == "gla" else 0)
    x_refs = refs[base:base + n_x]
    wo_ref, g_ref, b_ref, wr_ref, br_ref, x1_ref, rt_ref = refs[base + n_x:base + n_x + 7]

    if mode == "plain":
        hb = h_refs[0][...]
    elif mode == "rglru":
        gate, hf, hbw = h_refs
        hb = (_gelu_tanh(gate[...].astype(F32)) * (hf[...].astype(F32) + hbw[...].astype(F32))).astype(BF16)
    elif mode == "gla":
        of, ob, gg = h_refs
        o = of[...] + ob[...]
        parts = []
        for h in range(GLA_HEADS):
            oh = o[:, h * GLA_DVH:(h + 1) * GLA_DVH]
            ms = jnp.mean(oh * oh, axis=-1, keepdims=True)
            parts.append(oh * lax.rsqrt(ms + LN_EPS) * extra_ref[...])
        o = jnp.concatenate(parts, axis=1)
        hb = (o * _silu(gg[...].astype(F32))).astype(BF16)
    else:
        half = SLAB // 2
        os_ = [_slab_read(r, range(half)) for r in h_refs]
        ls_ = [_slab_read(r, range(half, SLAB)) for r in h_refs]
        mx = functools.reduce(jnp.maximum, ls_)
        es = [jnp.exp(l - mx) for l in ls_]
        den = functools.reduce(jnp.add, es)
        o = functools.reduce(jnp.add, [(e / den) * ov for e, ov in zip(es, os_)])
        hb = o.astype(BF16)

    acc = jnp.dot(hb, wo_ref[...], preferred_element_type=F32)
    x1 = _layer_norm(ALPHA * _x_read(x_refs, p_tiles) + acc, g_ref[...], b_ref[...])
    _slab_write(x1_ref, x1)
    rt_ref[...] = _route(x1, wr_ref, br_ref)


def _out_ln(hs, x, T, wo, ln_g, ln_b, wr, br, *, mode, extra=None):
    D = D_MODEL
    tm = TOK_TILE
    rows = lambda h: tm * SLAB if mode == "dil" else tm
    in_specs = [pl.BlockSpec((rows(h), h.shape[1]), lambda i: (i, 0)) for h in hs]
    args = list(hs)
    if mode == "gla":
        in_specs.append(pl.BlockSpec((1, extra.shape[1]), lambda i: (0, 0)))
        args.append(extra)
    x_specs, x_args, p_tiles = _x_specs(x, tm)
    in_specs += x_specs + [
        pl.BlockSpec(wo.shape, lambda i: (0, 0)),
        pl.BlockSpec((1, D), lambda i: (0, 0)),
        pl.BlockSpec((1, D), lambda i: (0, 0)),
        pl.BlockSpec(wr.shape, lambda i: (0, 0)),
        pl.BlockSpec(br.shape, lambda i: (0, 0)),
    ]
    args += x_args + [wo, ln_g, ln_b, wr, br]
    return pl.pallas_call(
        functools.partial(_out_ln_kernel, mode=mode, n_h=len(hs), n_x=len(x_args), p_tiles=p_tiles),
        grid=(T // tm,),
        in_specs=in_specs,
        out_specs=[pl.BlockSpec((tm * SLAB, LANES), lambda i: (i, 0)),
                   pl.BlockSpec((ROUTE_ROWS, tm), lambda i: (0, i))],
        out_shape=[jax.ShapeDtypeStruct((T * SLAB, LANES), F32),
                   jax.ShapeDtypeStruct((ROUTE_ROWS, T), F32)],
        compiler_params=_cparams(("parallel",)),
    )(*args)


def _moe_kernel(tg_ref, tlo_ref, thi_ref, nv_ref, base_ref, order_ref, x_hbm, c_hbm, sel_ref, wgl_ref, wul_ref,
                wdl_ref, wgh_ref, wuh_ref, wdh_ref, g_ref, b_ref, out_hbm, xbuf, obuf, cbuf, cslab, gsem, ssem,
                csem, *, nt, n_tok):
    i = pl.program_id(0)
    tm = xbuf.shape[1] // SLAB
    win = cbuf.shape[2]
    slot = lax.rem(i, 2)
    used = nv_ref[i] > 0
    nxt = jnp.minimum(i + 1, nt - 1)
    next_used = jnp.logical_and(i + 1 < nt, nv_ref[nxt] > 0)

    def token(ref, tok):
        return ref.at[pl.ds(pl.multiple_of(tok * SLAB, SLAB), SLAB)]

    def weights_copy(tile, sl):
        a = pl.multiple_of(jnp.bitwise_and(base_ref[tile], -LANES), LANES)
        return pltpu.make_async_copy(c_hbm.at[:, pl.ds(a, win)], cbuf.at[sl], csem.at[sl])

    def gather_start(tile, sl):
        base, last = base_ref[tile], nv_ref[tile] - 1
        for r in range(tm):
            tok = order_ref[base + jnp.minimum(r, last)]
            pltpu.make_async_copy(token(x_hbm, tok), xbuf.at[sl, pl.ds(r * SLAB, SLAB)], gsem.at[sl]).start()
        weights_copy(tile, sl).start()

    def gather_wait(tile, sl):
        pltpu.make_async_copy(x_hbm.at[pl.ds(0, tm * SLAB)], xbuf.at[sl], gsem.at[sl]).wait()
        weights_copy(tile, sl).wait()

    def scatter_start(tile, sl):
        base, n = base_ref[tile], nv_ref[tile]
        dump = n_tok + sl * tm
        for r in range(tm):
            tok = jnp.where(r < n, order_ref[base + jnp.minimum(r, n - 1)], dump + r)
            pltpu.make_async_copy(obuf.at[sl, pl.ds(r * SLAB, SLAB)], token(out_hbm, tok), ssem.at[sl]).start()

    def scatter_wait(sl):
        pltpu.make_async_copy(obuf.at[sl], out_hbm.at[pl.ds(0, tm * SLAB)], ssem.at[sl]).wait()

    @pl.when(i == 0)
    def _():
        obuf[...] = jnp.zeros_like(obuf)
        for sl in range(2):
            cp = pltpu.make_async_copy(obuf.at[sl], out_hbm.at[pl.ds((n_tok + sl * tm) * SLAB, tm * SLAB)],
                                       ssem.at[sl])
            cp.start()
            cp.wait()

    @pl.when(jnp.logical_and(i == 0, used))
    def _():
        gather_start(0, 0)

    @pl.when(used)
    def _():
        gather_wait(i, slot)

        @pl.when(next_used)
        def _():
            gather_start(i + 1, 1 - slot)

        @pl.when(i >= 2)
        def _():
            scatter_wait(slot)

        x = _slab_read(xbuf.at[slot])
        cslab[...] = _rows_to_lanes(cbuf[slot], sel_ref[...])
        cw = cslab[pl.ds(jnp.bitwise_and(base_ref[i], LANES - 1), tm), :]
        xb = x.astype(BF16)
        acc = jnp.zeros((tm, D_MODEL), F32)
        for wg_ref, wu_ref, wd_ref, lane in ((wgl_ref, wul_ref, wdl_ref, 0), (wgh_ref, wuh_ref, wdh_ref, 1)):
            hg = jnp.dot(xb, wg_ref[0, 0, 0], preferred_element_type=F32)
            hu = jnp.dot(xb, wu_ref[0, 0, 0], preferred_element_type=F32)
            h = _silu(hg) * hu * cw[:, lane:lane + 1]
            acc = acc + jnp.dot(h.astype(BF16), wd_ref[0, 0, 0], preferred_element_type=F32)
        _slab_write(obuf.at[slot], _layer_norm(ALPHA * x + acc, g_ref[...], b_ref[...]))
        scatter_start(i, slot)

        @pl.when(jnp.logical_not(next_used))
        def _():
            @pl.when(i >= 1)
            def _():
                scatter_wait(1 - slot)
            scatter_wait(slot)


_PAIR_LO = tuple(a for a in range(MOE_EPG) for b in range(a + 1, MOE_EPG))
_PAIR_HI = tuple(b for a in range(MOE_EPG) for b in range(a + 1, MOE_EPG))
N_PAIRS = len(_PAIR_LO)
N_CLASSES = MOE_GROUPS * N_PAIRS


def _moe_schedule(rt, tm, nt):
    T = rt.shape[1]
    i32 = jnp.int32
    g, lo, hi = rt[0].astype(i32), rt[1].astype(i32), rt[2].astype(i32)
    cls = g * N_PAIRS + lo * (2 * MOE_EPG - 1 - lo) // 2 + (hi - lo - 1)
    w_lo, w_hi = rt[3], rt[4]
    cls_sorted, order, w_lo, w_hi = lax.sort((cls, jnp.arange(T, dtype=i32), w_lo, w_hi), num_keys=1,
                                             is_stable=True)
    starts = jnp.searchsorted(cls_sorted, jnp.arange(N_CLASSES + 1, dtype=i32), side="left",
                              method="compare_all").astype(i32)
    counts = starts[1:] - starts[:-1]
    tiles_per = (counts + tm - 1) // tm
    cum = jnp.cumsum(tiles_per)
    n_used = cum[-1]
    t = jnp.arange(nt, dtype=i32)
    tc = jnp.minimum(t, n_used - 1)
    tcls = jnp.searchsorted(cum, tc, side="right", method="compare_all").astype(i32)
    k = tc - (cum[tcls] - tiles_per[tcls])
    nvalid = jnp.where(t < n_used, jnp.clip(counts[tcls] - k * tm, 0, tm), 0).astype(i32)
    base = (starts[tcls] + k * tm).astype(i32)
    pr = tcls % N_PAIRS
    tile_lo = jnp.asarray(_PAIR_LO, i32)[pr]
    tile_hi = jnp.asarray(_PAIR_HI, i32)[pr]
    c_sorted = jnp.zeros((ROUTE_ROWS, T + tm + LANES), F32).at[0, :T].set(w_lo).at[1, :T].set(w_hi)
    return (tcls // N_PAIRS).astype(i32), tile_lo, tile_hi, nvalid, base, order, c_sorted


def _moe(x1, rt, layer, wg, wu, wd, ln_g, ln_b):
    T = x1.shape[0] // SLAB
    D = D_MODEL
    tm = MOE_TILE
    nt = T // tm + N_CLASSES
    win = tm + LANES
    *sched, c_sorted = _moe_schedule(rt, tm, nt)
    sel = jnp.tile(jnp.eye(ROUTE_ROWS, LANES, dtype=BF16), (3, 1))
    lo_map = lambda i, tg, tlo, thi, nv, base, order: (layer, tg[i], tlo[i], 0, 0)
    hi_map = lambda i, tg, tlo, thi, nv, base, order: (layer, tg[i], thi[i], 0, 0)
    const = lambda i, tg, tlo, thi, nv, base, order: (0, 0)
    grid_spec = pltpu.PrefetchScalarGridSpec(
        num_scalar_prefetch=6,
        grid=(nt,),
        in_specs=[
            pl.BlockSpec(memory_space=pl.ANY),
            pl.BlockSpec(memory_space=pl.ANY),
            pl.BlockSpec(sel.shape, const),
            pl.BlockSpec((1, 1, 1, D, MOE_FF), lo_map),
            pl.BlockSpec((1, 1, 1, D, MOE_FF), lo_map),
            pl.BlockSpec((1, 1, 1, MOE_FF, D), lo_map),
            pl.BlockSpec((1, 1, 1, D, MOE_FF), hi_map),
            pl.BlockSpec((1, 1, 1, D, MOE_FF), hi_map),
            pl.BlockSpec((1, 1, 1, MOE_FF, D), hi_map),
            pl.BlockSpec((1, D), const),
            pl.BlockSpec((1, D), const),
        ],
        out_specs=pl.BlockSpec(memory_space=pl.ANY),
        scratch_shapes=[pltpu.VMEM((2, tm * SLAB, LANES), F32), pltpu.VMEM((2, tm * SLAB, LANES), F32),
                        pltpu.VMEM((2, ROUTE_ROWS, win), F32), pltpu.VMEM((win, LANES), F32),
                        pltpu.SemaphoreType.DMA((2,)), pltpu.SemaphoreType.DMA((2,)),
                        pltpu.SemaphoreType.DMA((2,))],
    )
    return pl.pallas_call(
        functools.partial(_moe_kernel, nt=nt, n_tok=T),
        grid_spec=grid_spec,
        out_shape=jax.ShapeDtypeStruct(((T + 2 * tm) * SLAB, LANES), F32),
        compiler_params=_cparams(("arbitrary",)),
    )(*sched, x1, c_sorted, sel, wg, wu, wd, wg, wu, wd, ln_g, ln_b)


def _rglru_stream(z, u_ref, up_ref, un_ref, w_ref, cw_ref, cb_ref, ba_ref, bi_ref, lam_ref,
                  h_out_ref, a_s, b_s, h_s, carry_ref, prev_ok, next_ok, reset):
    tt = u_ref.shape[0]
    u_mid = u_ref[...]
    up = jnp.where(prev_ok, up_ref[...], 0.0)
    un = jnp.where(next_ok, un_ref[...], 0.0)
    head = jnp.concatenate([up, u_mid[:SUBLANES]], axis=0)
    tail = jnp.concatenate([u_mid[tt - SUBLANES:], un], axis=0)
    left = RG_CONV // 2
    u = cb_ref[...]
    for kk in range(RG_CONV):
        s = kk - left
        if s == 0:
            shifted = u_mid
        else:
            rolled = pltpu.roll(u_mid, (-s) % tt, 0)
            if s < 0:
                shifted = jnp.concatenate([head[SUBLANES + s:2 * SUBLANES + s], rolled[SUBLANES:]], axis=0)
            else:
                shifted = jnp.concatenate([rolled[:tt - SUBLANES], tail[s:SUBLANES + s]], axis=0)
        u = u + cw_ref[kk:kk + 1, :] * shifted
    ub = u.astype(BF16)
    sp = jnp.maximum(-lam_ref[...], 0.0) + jnp.log(1.0 + jnp.exp(-jnp.abs(lam_ref[...])))
    for n in range(RG_NT):
        s0 = min(max(n - 1, 0), RG_NT - 3) * LANES
        zz = jnp.dot(ub[:, s0:s0 + RG_BAND], w_ref[n], preferred_element_type=F32)
        sl = slice(n * LANES, (n + 1) * LANES)
        r = jax.nn.sigmoid(zz[:, :LANES] + ba_ref[:, sl])
        ig = jax.nn.sigmoid(zz[:, LANES:] + bi_ref[:, sl])
        log_a = -RG_C * r * sp[:, sl]
        a = jnp.exp(log_a)
        a_s[:, sl] = a
        b_s[:, sl] = jnp.sqrt(-jnp.tanh(log_a) * (a * a + 1.0)) * ig * u[:, sl]

    @pl.when(reset)
    def _():
        carry_ref[...] = jnp.zeros_like(carry_ref)

    n_grp = tt // SUBLANES

    def body(gi, h):
        g = gi if z == 0 else n_grp - 1 - gi
        base = pl.multiple_of(g * SUBLANES, SUBLANES)
        a8 = a_s[pl.ds(base, SUBLANES), :]
        b8 = b_s[pl.ds(base, SUBLANES), :]
        rows = [None] * SUBLANES
        order = range(SUBLANES) if z == 0 else range(SUBLANES - 1, -1, -1)
        for r_ in order:
            h = a8[r_:r_ + 1] * h + b8[r_:r_ + 1]
            rows[r_] = h
        h_s[pl.ds(base, SUBLANES), :] = jnp.concatenate(rows, axis=0)
        return h

    h_last = lax.fori_loop(0, n_grp, body, carry_ref[...])
    carry_ref[...] = h_last
    h_out_ref[...] = h_s[...].astype(h_out_ref.dtype)


def _rglru_scan_kernel(uf_ref, ufp_ref, ufn_ref, ubk_ref, ubp_ref, ubn_ref, wf_ref, wb_ref, cw_ref, cb_ref,
                       ba_ref, bi_ref, lam_ref, hf_ref, hb_ref, a_s, b_s, h_s, cf_ref, cbk_ref, *, segs, nt):
    i = pl.program_id(0)
    tt = uf_ref.shape[0]
    _, tps, j = _seq_pos(i, tt, segs)
    _rglru_stream(0, uf_ref, ufp_ref, ufn_ref, wf_ref, cw_ref, cb_ref, ba_ref.at[0:1], bi_ref.at[0:1],
                  lam_ref.at[0:1], hf_ref, a_s, b_s, h_s, cf_ref, j > 0, j < tps - 1, j == 0)
    ib = nt - 1 - i
    _, tps_b, jb = _seq_pos(ib, tt, segs)
    _rglru_stream(1, ubk_ref, ubp_ref, ubn_ref, wb_ref, cw_ref, cb_ref, ba_ref.at[1:2], bi_ref.at[1:2],
                  lam_ref.at[1:2], hb_ref, a_s, b_s, h_s, cbk_ref, jb > 0, jb < tps_b - 1, jb == tps_b - 1)


def _rglru_scan(u_pre, w_band, conv_w, conv_b, ba, bi, lam, segs):
    T, W = u_pre.shape
    tt = SCAN_TILE
    nt = T // tt
    hpt = tt // SUBLANES
    n_h = T // SUBLANES
    fwd = lambda i: (i, 0)
    bwd = lambda i: (nt - 1 - i, 0)

    def halo_specs(idx):
        return [
            pl.BlockSpec((tt, W), lambda i: (idx(i), 0)),
            pl.BlockSpec((SUBLANES, W), lambda i: (jnp.maximum(idx(i) * hpt - 1, 0), 0)),
            pl.BlockSpec((SUBLANES, W), lambda i: (jnp.minimum((idx(i) + 1) * hpt, n_h - 1), 0)),
        ]

    const2 = lambda i: (0, 0)
    const3 = lambda i: (0, 0, 0)
    in_specs = halo_specs(lambda i: i) + halo_specs(lambda i: nt - 1 - i) + [
        pl.BlockSpec(w_band.shape[1:], const3),
        pl.BlockSpec(w_band.shape[1:], const3),
        pl.BlockSpec(conv_w.shape, const2),
        pl.BlockSpec(conv_b.shape, const2),
        pl.BlockSpec(ba.shape, const2),
        pl.BlockSpec(bi.shape, const2),
        pl.BlockSpec(lam.shape, const2),
    ]
    return pl.pallas_call(
        functools.partial(_rglru_scan_kernel, segs=segs, nt=nt),
        grid=(nt,),
        in_specs=in_specs,
        out_specs=[pl.BlockSpec((tt, W), fwd), pl.BlockSpec((tt, W), bwd)],
        out_shape=[jax.ShapeDtypeStruct((T, W), BF16), jax.ShapeDtypeStruct((T, W), BF16)],
        scratch_shapes=[pltpu.VMEM((tt, W), F32), pltpu.VMEM((tt, W), F32), pltpu.VMEM((tt, W), F32),
                        pltpu.VMEM((1, W), F32), pltpu.VMEM((1, W), F32)],
        compiler_params=_cparams(("arbitrary",)),
    )(u_pre, u_pre, u_pre, u_pre, u_pre, u_pre, w_band[0], w_band[1], conv_w, conv_b, ba, bi, lam)


def _rglru_band_weights(wa, wi):
    def dense(w):
        eye = jnp.eye(RG_BLOCKS, dtype=w.dtype)
        return jnp.einsum("ncd,nm->ncmd", w, eye).reshape(RG_WIDTH, RG_WIDTH)

    out = []
    for z in range(2):
        da, di = dense(wa[z]), dense(wi[z])
        tiles = []
        for n in range(RG_NT):
            s0 = min(max(n - 1, 0), RG_NT - 3) * LANES
            sl = slice(n * LANES, (n + 1) * LANES)
            tiles.append(jnp.concatenate([da[s0:s0 + RG_BAND, sl], di[s0:s0 + RG_BAND, sl]], axis=1))
        out.append(jnp.stack(tiles))
    return jnp.stack(out).astype(BF16)


def _gla_stream(z, q_ref, k_ref, v_ref, lr_ref, wa2_ref, ba_ref, o_ref, st_ref, reset):
    tt = q_ref.shape[0]
    C = GLA_CHUNK

    @pl.when(reset)
    def _():
        st_ref[z] = jnp.zeros(st_ref.shape[1:], F32)

    zz = jnp.dot(lr_ref[...].astype(BF16), wa2_ref[:, z * GLA_DK:(z + 1) * GLA_DK],
                 preferred_element_type=F32) + ba_ref[:, z * GLA_DK:(z + 1) * GLA_DK]
    log_a = -(jnp.maximum(-zz, 0.0) + jnp.log(1.0 + jnp.exp(-jnp.abs(zz)))) / GLA_TAU
    ri = lax.broadcasted_iota(jnp.int32, (C, C), 0)
    ci = lax.broadcasted_iota(jnp.int32, (C, C), 1)
    causal = (ri >= ci) if z == 0 else (ri <= ci)
    tri = causal.astype(F32)
    mid = C // 2 if z == 0 else C - 1 - C // 2
    last = C - 1 if z == 0 else 0
    scale = GLA_DKH ** -0.5
    chunks = range(tt // C) if z == 0 else range(tt // C - 1, -1, -1)
    for c in chunks:
        rs = slice(c * C, (c + 1) * C)
        b = jnp.dot(tri, log_a[rs], preferred_element_type=F32, precision=lax.Precision.HIGHEST)
        b_mid = b[mid:mid + 1]
        b_last = b[last:last + 1]
        qc = q_ref[rs, :] * scale
        kc = k_ref[rs, :]
        qd = (qc * jnp.exp(b - b_mid)).astype(BF16)
        kd = (kc * jnp.exp(b_mid - b)).astype(BF16)
        ks = (kc * jnp.exp(b_last - b)).astype(BF16)
        qb = (qc * jnp.exp(b)).astype(BF16)
        dec = jnp.exp(b_last)
        for h in range(GLA_HEADS):
            ksl = slice(h * GLA_DKH, (h + 1) * GLA_DKH)
            vsl = slice(h * GLA_DVH, (h + 1) * GLA_DVH)
            vc = v_ref[rs, vsl]
            att = lax.dot_general(qd[:, ksl], kd[:, ksl], (((1,), (1,)), ((), ())), preferred_element_type=F32)
            att = jnp.where(causal, att, 0.0)
            o = jnp.dot(att.astype(BF16), vc, preferred_element_type=F32)
            st = st_ref[z, h]
            o = o + lax.dot_general(qb[:, ksl], st.astype(BF16), (((1,), (1,)), ((), ())),
                                    preferred_element_type=F32)
            o_ref[rs, vsl] = o
            upd = lax.dot_general(vc, ks[:, ksl], (((0,), (0,)), ((), ())), preferred_element_type=F32)
            st_ref[z, h] = st * dec[:, ksl] + upd


def _gla_kernel(qf, kf, vf, lf, qb, kb, vb, lb, wa2_ref, ba_ref, of_ref, ob_ref, st_ref, *, segs, nt):
    i = pl.program_id(0)
    tt = qf.shape[0]
    _, tps, j = _seq_pos(i, tt, segs)
    _gla_stream(0, qf, kf, vf, lf, wa2_ref, ba_ref, of_ref, st_ref, j == 0)
    ib = nt - 1 - i
    _, tps_b, jb = _seq_pos(ib, tt, segs)
    _gla_stream(1, qb, kb, vb, lb, wa2_ref, ba_ref, ob_ref, st_ref, jb == tps_b - 1)


def _gla(q, k, v, lr, wa2, ba, segs):
    T = q.shape[0]
    tt = SCAN_TILE
    nt = T // tt
    fwd = lambda i: (i, 0)
    bwd = lambda i: (nt - 1 - i, 0)
    in_specs = []
    for idx in (fwd, bwd):
        in_specs += [pl.BlockSpec((tt, GLA_DK), idx), pl.BlockSpec((tt, GLA_DK), idx),
                     pl.BlockSpec((tt, GLA_DV), idx), pl.BlockSpec((tt, LANES), idx)]
    in_specs += [pl.BlockSpec(wa2.shape, lambda i: (0, 0)), pl.BlockSpec(ba.shape, lambda i: (0, 0))]
    return pl.pallas_call(
        functools.partial(_gla_kernel, segs=segs, nt=nt),
        grid=(nt,),
        in_specs=in_specs,
        out_specs=[pl.BlockSpec((tt, GLA_DV), fwd), pl.BlockSpec((tt, GLA_DV), bwd)],
        out_shape=[jax.ShapeDtypeStruct((T, GLA_DV), F32), jax.ShapeDtypeStruct((T, GLA_DV), F32)],
        scratch_shapes=[pltpu.VMEM((2, GLA_HEADS, GLA_DVH, GLA_DKH), F32)],
        compiler_params=_cparams(("arbitrary",)),
    )(q, k, v, lr, q, k, v, lr, wa2, ba)


def _rope_tables(s_max):
    half = HEAD_DIM // 2
    inv = ROPE_THETA ** (-jnp.arange(half, dtype=F32) / half)
    ang = jnp.arange(s_max, dtype=F32)[:, None] * inv[None, :]
    cos, sin = jnp.cos(ang), jnp.sin(ang)
    cos_t = jnp.tile(jnp.concatenate([cos, cos], axis=1), (1, LANES // HEAD_DIM))
    sin_t = jnp.tile(jnp.concatenate([-sin, sin], axis=1), (1, LANES // HEAD_DIM))
    return cos_t, sin_t


def _router_weights(wgr, bgr, wer, ber):
    n_e = MOE_GROUPS * MOE_EPG
    w = jnp.concatenate([jnp.transpose(wer, (0, 2, 1)).reshape(n_e, D_MODEL), wgr.T], axis=0)
    b = jnp.concatenate([ber.reshape(n_e), bgr])
    n_pad = ROUTER_ROWS - w.shape[0]
    w = jnp.pad(w, ((0, n_pad), (0, 0)))
    b = jnp.concatenate([b, jnp.full((n_pad,), NEG_INF, F32)])
    w_hi = w.astype(BF16)
    w_mid = (w - w_hi.astype(F32)).astype(BF16)
    return jnp.concatenate([w_hi, w_mid, w_hi], axis=1), b[:, None]


def _mixer(kind, jl, x, segs, p, cos_t, sin_t, lg1, lb1, wr, br):
    D = D_MODEL
    T = sum(n * s for n, s in segs)
    if kind == 0:
        nq = A_HEADS * HEAD_DIM
        nkv = A_KV_HEADS * HEAD_DIM
        w = p["a_wqkv"][jl]
        wq, wk, wv = w[:, :nq], w[:, nq:nq + nkv], w[:, nq + nkv:]
        dup = lambda t: jnp.repeat(t.reshape(D, A_KV_HEADS, 1, HEAD_DIM), 2, axis=2).reshape(D, 2 * nkv)
        w_ext = jnp.concatenate([wq, dup(wk), dup(wv)], axis=1).astype(BF16)
        q, kd, vd = _proj_rope(x, T, w_ext, cos_t, sin_t, segs, dil=1, n_rope=(nq + 2 * nkv) // LANES,
                               scale_cols=nq, scale=HEAD_DIM ** -0.5, out_cols=(nq, 2 * nkv, 2 * nkv),
                               out_dtypes=(BF16, BF16, BF16))
        o = _band_attn(q, kd, vd, segs, dil=1, radius=A_RADIUS, q_per_k=2, sink=p["a_sink"][jl])
        return _out_ln([o], x, T, p["a_wo"][jl].astype(BF16), lg1, lb1, wr, br, mode="plain")
    if kind == 1:
        gate, u_pre = _proj(x, T, p["b_win"][jl].astype(BF16), (RG_WIDTH, RG_WIDTH), (BF16, F32))
        w_band = _rglru_band_weights(p["b_wa"][jl], p["b_wi"][jl])
        hf, hb = _rglru_scan(u_pre, w_band, p["b_conv_w"][jl], p["b_conv_b"][jl][None, :], p["b_ba"][jl],
                             p["b_bi"][jl], p["b_lam"][jl], segs)
        return _out_ln([gate, hf, hb], x, T, p["b_wo"][jl].astype(BF16), lg1, lb1, wr, br, mode="rglru")
    if kind == 2:
        wa1 = jnp.concatenate([p["c_wa1"][jl][0], p["c_wa1"][jl][1]], axis=1)
        wa1 = jnp.pad(wa1, ((0, 0), (0, LANES - wa1.shape[1])))
        w_all = jnp.concatenate([p["c_wqkvg"][jl], wa1], axis=1).astype(BF16)
        q, k, v, g, lr = _proj(x, T, w_all, (GLA_DK, GLA_DK, GLA_DV, GLA_DV, LANES), (F32, F32, BF16, BF16, F32))
        wa2 = jnp.zeros((LANES, 2 * GLA_DK), F32)
        wa2 = wa2.at[:GLA_RANK, :GLA_DK].set(p["c_wa2"][jl][0])
        wa2 = wa2.at[GLA_RANK:2 * GLA_RANK, GLA_DK:].set(p["c_wa2"][jl][1])
        of, ob = _gla(q, k, v, lr, wa2.astype(BF16), p["c_ba"][jl].reshape(1, 2 * GLA_DK), segs)
        return _out_ln([of, ob, g], x, T, p["c_wo"][jl].astype(BF16), lg1, lb1, wr, br, mode="gla",
                       extra=p["c_norm_g"][jl][None, :])
    ols = []
    for gi, (window, dil) in enumerate(DIL_GROUPS):
        w_g = p["d_wqkv"][jl][:, gi * 3 * DIL_WIDTH:(gi + 1) * 3 * DIL_WIDTH].astype(BF16)
        q, k, v = _proj_rope(x, T, w_g, cos_t, sin_t, segs, dil=dil, n_rope=2 * DIL_WIDTH // LANES,
                             scale_cols=DIL_WIDTH, scale=HEAD_DIM ** -0.5,
                             out_cols=(DIL_WIDTH,) * 3, out_dtypes=(BF16,) * 3)
        ols.append(_band_attn(q, k, v, segs, dil=dil, radius=window // (2 * dil), q_per_k=1, want_lse=True))
    return _out_ln(ols, x, T, p["d_wo"][jl].astype(BF16), lg1, lb1, wr, br, mode="dil")


def kernel(x_prompt, x_sample, ln_g, ln_b, a_wqkv, a_sink, a_wo, b_win, b_conv_w, b_conv_b, b_wa, b_ba, b_wi,
           b_bi, b_lam, b_wo, c_wqkvg, c_wa1, c_wa2, c_ba, c_norm_g, c_wo, d_wqkv, d_wo, m_wgr, m_bgr, m_wer,
           m_ber, m_wg, m_wu, m_wd):
    p = dict(a_wqkv=a_wqkv, a_sink=a_sink, a_wo=a_wo, b_win=b_win, b_conv_w=b_conv_w, b_conv_b=b_conv_b,
             b_wa=b_wa, b_ba=b_ba, b_wi=b_wi, b_bi=b_bi, b_lam=b_lam, b_wo=b_wo, c_wqkvg=c_wqkvg, c_wa1=c_wa1,
             c_wa2=c_wa2, c_ba=c_ba, c_norm_g=c_norm_g, c_wo=c_wo, d_wqkv=d_wqkv, d_wo=d_wo)
    n_p, s_p, D = x_prompt.shape
    n_s, s_s, _ = x_sample.shape
    segs = ((n_p, s_p), (n_s, s_s))
    assert (n_p * s_p) % s_s == 0
    t_p, T = n_p * s_p, n_p * s_p + n_s * s_s
    x = (x_prompt.reshape(t_p, D), x_sample.reshape(T - t_p, D))
    cos_t, sin_t = _rope_tables(max(s_p, s_s))
    wg_b, wu_b, wd_b = m_wg.astype(BF16), m_wu.astype(BF16), m_wd.astype(BF16)

    for layer in range(DEPTH):
        lg1, lb1 = ln_g[layer, 0][None, :], ln_b[layer, 0][None, :]
        lg2, lb2 = ln_g[layer, 1][None, :], ln_b[layer, 1][None, :]
        wr, br = _router_weights(m_wgr[layer], m_bgr[layer], m_wer[layer], m_ber[layer])
        x1, rt = _mixer(layer % 4, layer // 4, x, segs, p, cos_t, sin_t, lg1, lb1, wr, br)
        x = _moe(x1, rt, layer, wg_b, wu_b, wd_b, lg2, lb2)

    return _from_slab(x, 0, t_p).reshape(n_p, s_p, D), _from_slab(x, t_p, T - t_p).reshape(n_s, s_s, D)
```

```python
import functools

import jax
import jax.numpy as jnp
from jax import lax
from jax.experimental import pallas as pl
from jax.experimental.pallas import tpu as pltpu

F32 = jnp.float32
BF16 = jnp.bfloat16

D_MODEL = 1024
HEAD_DIM = 64
ROPE_THETA = 10000.0
A_HEADS = 16
A_KV_HEADS = 4
A_RADIUS = 128
RG_WIDTH = 1408
RG_BLOCKS = 16
RG_BW = RG_WIDTH // RG_BLOCKS
RG_CONV = 4
RG_C = 8.0
GLA_HEADS = 4
GLA_DK = 512
GLA_DV = 1024
GLA_DKH = 128
GLA_DVH = 256
GLA_RANK = 16
GLA_TAU = 16.0
GLA_CHUNK = 64
DIL_GROUPS = ((128, 1), (512, 4), (2048, 16))
N_DIL = 3
DIL_HEADS = 8
DIL_WIDTH = 512
MOE_GROUPS = 4
MOE_EPG = 8
MOE_FF = 256
DEPTH = 4
ALPHA = (2 * DEPTH) ** 0.25
LN_EPS = 1e-5
NEG_INF = -1e30

LANES = 128
SUBLANES = 8
VMEM_LIMIT = 52 * 1024 * 1024
ATT_Q = 128
ATT_STEP_BLOCKS = 4
TOK_TILE = 512
SCAN_TILE = 256
RG_NT = RG_WIDTH // LANES
RG_BAND = 3 * LANES
ROUTER_ROWS = 40
ROUTE_ROWS = 16
MOE_TILE = 128
SLAB = D_MODEL // LANES
PROJ_CHUNK = 512


def _cparams(sem):
    return pltpu.CompilerParams(dimension_semantics=sem, vmem_limit_bytes=VMEM_LIMIT)


def _seq_pos(i, rows, segs, dil=1):
    (n_p, s_p), (_, s_s) = segs
    p_tiles = (n_p * s_p) // rows
    in_p = i < p_tiles
    tps = jnp.where(in_p, (s_p // dil) // rows, (s_s // dil) // rows)
    j = lax.rem(i, tps)
    return in_p, tps, j


def _layer_norm(y, g, b):
    mu = jnp.mean(y, axis=-1, keepdims=True)
    yc = y - mu
    var = jnp.mean(yc * yc, axis=-1, keepdims=True)
    return yc * lax.rsqrt(var + LN_EPS) * g + b


def _rope_slab(t, cos, sin_signed):
    lane = lax.broadcasted_iota(jnp.int32, t.shape, 1)
    first_half = (lane % HEAD_DIM) < (HEAD_DIM // 2)
    partner = jnp.where(first_half, pltpu.roll(t, LANES - HEAD_DIM // 2, 1), pltpu.roll(t, HEAD_DIM // 2, 1))
    return t * cos + partner * sin_signed


def _slab_read(ref, chunks=range(SLAB)):
    if len(ref.shape) == 3:
        return jnp.concatenate([ref[:, c, :] for c in chunks], axis=1)
    rows = ref.shape[0] // SLAB
    return jnp.concatenate([ref[pl.ds(c, rows, stride=SLAB), :] for c in chunks], axis=1)


def _slab_write(ref, val, first_chunk=0, tok0=0):
    n = val.shape[0]
    for c in range(val.shape[1] // LANES):
        v = val[:, c * LANES:(c + 1) * LANES]
        if len(ref.shape) == 3:
            ref[tok0:tok0 + n, first_chunk + c, :] = v
        else:
            ref[pl.ds(tok0 * SLAB + first_chunk + c, n, stride=SLAB), :] = v


def _col_chunks(width):
    out, c = [], 0
    while c < width:
        w = min(PROJ_CHUNK, width - c)
        out.append((c, w))
        c += w
    return out


def _x_specs(x, tm):
    if isinstance(x, tuple):
        xp, xs = x
        p_tiles = xp.shape[0] // tm
        specs = [pl.BlockSpec((tm, D_MODEL), lambda i, *_: (jnp.minimum(i, p_tiles - 1), 0)),
                 pl.BlockSpec((tm, D_MODEL), lambda i, *_: (jnp.maximum(i - p_tiles, 0), 0))]
        return specs, [xp, xs], p_tiles
    return [pl.BlockSpec((tm * SLAB, LANES), lambda i, *_: (i, 0))], [x], 0


def _x_read(x_refs, p_tiles):
    if len(x_refs) == 2:
        return jnp.where(pl.program_id(0) < p_tiles, x_refs[0][...], x_refs[1][...])
    return _slab_read(x_refs[0])


def _proj_rope_kernel(*refs, n_x, p_tiles, n_rope, scale_cols, scale, out_cols):
    x_refs, (w_ref, cos_ref, sin_ref), out_refs = refs[:n_x], refs[n_x:n_x + 3], refs[n_x + 3:]
    xb = _x_read(x_refs, p_tiles).astype(BF16)
    cos = cos_ref[...]
    sin = sin_ref[...]
    col = 0
    for o_ref, width in zip(out_refs, out_cols):
        for c0, cw in _col_chunks(width):
            yc = jnp.dot(xb, w_ref[:, col + c0:col + c0 + cw], preferred_element_type=F32)
            for c in range(cw // LANES):
                slab = (col + c0) // LANES + c
                y = yc[:, c * LANES:(c + 1) * LANES]
                if slab < n_rope:
                    y = _rope_slab(y, cos, sin)
                if slab * LANES < scale_cols:
                    y = y * scale
                o_ref[:, c0 + c * LANES:c0 + (c + 1) * LANES] = y.astype(o_ref.dtype)
        col += width


def _proj_rope(x, T, w, cos_tab, sin_tab, segs, *, dil, n_rope, scale_cols, scale, out_cols, out_dtypes):
    D = D_MODEL
    (n_p, s_p), (n_s, s_s) = segs
    l_p, l_s = s_p // dil, s_s // dil
    tj = min(TOK_TILE, l_s, l_p)
    assert l_p % tj == 0 and l_s % tj == 0 and (n_p * s_p) % (dil * tj) == 0
    if dil == 1:
        x_specs, x_args, x_p_tiles = _x_specs(x, tj)
    else:
        assert x.shape[0] % (dil * SLAB) == 0
        x_args = [x.reshape(x.shape[0] // (dil * SLAB), dil * SLAB, LANES)]
        x_specs, x_p_tiles = [pl.BlockSpec((tj, SLAB, LANES), lambda jt, r: (jt, r, 0))], 0
    cosv = cos_tab.reshape(cos_tab.shape[0] // dil, dil * LANES)
    sinv = sin_tab.reshape(sin_tab.shape[0] // dil, dil * LANES)
    p_tiles = (n_p * l_p) // tj

    def decode(jt):
        in_p = jt < p_tiles
        lt = jnp.where(in_p, l_p // tj, l_s // tj)
        jt_loc = jnp.where(in_p, jt, jt - p_tiles)
        b = jt_loc // lt
        j0 = lax.rem(jt_loc, lt)
        return in_p, lt, b, j0

    def out_map(jt, r):
        in_p, lt, b, j0 = decode(jt)
        base = jnp.where(in_p, 0, (n_p * s_p) // tj)
        return (base + b * (lt * dil) + r * lt + j0, 0)

    def tab_map(jt, r):
        _, _, _, j0 = decode(jt)
        return (j0, r)

    kern = functools.partial(_proj_rope_kernel, n_x=len(x_args), p_tiles=x_p_tiles, n_rope=n_rope,
                             scale_cols=scale_cols, scale=scale, out_cols=out_cols)
    n_out = w.shape[1]
    return pl.pallas_call(
        kern,
        grid=(T // dil // tj, dil),
        in_specs=x_specs + [
            pl.BlockSpec((D, n_out), lambda jt, r: (0, 0)),
            pl.BlockSpec((tj, LANES), tab_map),
            pl.BlockSpec((tj, LANES), tab_map),
        ],
        out_specs=[pl.BlockSpec((tj, c), out_map) for c in out_cols],
        out_shape=[jax.ShapeDtypeStruct((T, c), dt) for c, dt in zip(out_cols, out_dtypes)],
        compiler_params=_cparams(("parallel", "parallel")),
    )(*x_args, w, cosv, sinv)


def _from_slab_kernel(x_ref, o_ref):
    o_ref[...] = _slab_read(x_ref)


def _from_slab(x, tok0, n_tok):
    tm = TOK_TILE
    assert tok0 % tm == 0 and n_tok % tm == 0
    return pl.pallas_call(
        _from_slab_kernel,
        grid=(n_tok // tm,),
        in_specs=[pl.BlockSpec((tm * SLAB, LANES), lambda i: (tok0 // tm + i, 0))],
        out_specs=pl.BlockSpec((tm, D_MODEL), lambda i: (i, 0)),
        out_shape=jax.ShapeDtypeStruct((n_tok, D_MODEL), F32),
        compiler_params=_cparams(("parallel",)),
    )(x)


def _proj_kernel(x_ref, w_ref, *out_refs, out_cols):
    xb = _slab_read(x_ref).astype(BF16)
    col = 0
    for o_ref, width in zip(out_refs, out_cols):
        for c0, cw in _col_chunks(width):
            y = jnp.dot(xb, w_ref[:, col + c0:col + c0 + cw], preferred_element_type=F32)
            o_ref[:, c0:c0 + cw] = y.astype(o_ref.dtype)
        col += width


def _proj(x, T, w, out_cols, out_dtypes):
    D = D_MODEL
    tm = TOK_TILE
    n_out = w.shape[1]
    assert sum(out_cols) == n_out and T % tm == 0
    return pl.pallas_call(
        functools.partial(_proj_kernel, out_cols=out_cols),
        grid=(T // tm,),
        in_specs=[pl.BlockSpec((tm * SLAB, LANES), lambda i: (i, 0)), pl.BlockSpec((D, n_out), lambda i: (0, 0))],
        out_specs=[pl.BlockSpec((tm, c), lambda i: (i, 0)) for c in out_cols],
        out_shape=[jax.ShapeDtypeStruct((T, c), dt) for c, dt in zip(out_cols, out_dtypes)],
        compiler_params=_cparams(("parallel",)),
    )(x, w)


def _band_attn_kernel(*refs, segs, dil, radius, q_per_k, has_sink, want_lse, n_sub):
    it = iter(refs)
    sink_ref = next(it) if has_sink else None
    q_ref = next(it)
    kp_ref, kc_ref, kn_ref = next(it), next(it), next(it)
    vp_ref, vc_ref, vn_ref = next(it), next(it), next(it)
    o_ref = next(it)
    n_q_slabs = q_ref.shape[1] // LANES
    qs = n_sub * ATT_Q

    i = pl.program_id(0)
    _, tps, j = _seq_pos(i, qs, segs, dil)
    W = 2 * radius + ATT_Q
    row = lax.broadcasted_iota(jnp.int32, (ATT_Q, W), 0)
    col = lax.broadcasted_iota(jnp.int32, (ATT_Q, W), 1)
    in_band = jnp.abs(col - radius - row) <= radius

    def block_bias(prev_ok, next_ok):
        ok = in_band
        if prev_ok is not None:
            ok = ok & ((col >= radius) | prev_ok)
        if next_ok is not None:
            ok = ok & ((col < radius + ATT_Q) | next_ok)
        bias = jnp.where(ok, 0.0, NEG_INF).astype(F32)
        return jnp.concatenate([bias] * q_per_k, axis=0) if q_per_k > 1 else bias

    biases = [block_bias(j > 0 if u == 0 else None, j < tps - 1 if u == n_sub - 1 else None)
              for u in range(n_sub)]

    k_all = jnp.concatenate([kp_ref[...], kc_ref[...], kn_ref[...]], axis=0)
    v_all = jnp.concatenate([vp_ref[...], vc_ref[...], vn_ref[...]], axis=0)
    nk = k_all.shape[1] // LANES
    lane_lo = lax.broadcasted_iota(jnp.int32, (1, LANES), 1) < HEAD_DIM
    M = ATT_Q * q_per_k
    zero = jnp.zeros((), BF16)

    for kc in range(nk):
        ksl = k_all[:, kc * LANES:(kc + 1) * LANES]
        vsl = v_all[:, kc * LANES:(kc + 1) * LANES]
        slabs = [kc * q_per_k + t for t in range(q_per_k)]
        masked = []
        for half in range(2):
            keep = lane_lo if half == 0 else jnp.logical_not(lane_lo)
            masked.append((keep, jnp.where(keep, ksl, zero), jnp.where(keep, vsl, zero)))
        for u in range(n_sub):
            r0 = u * ATT_Q
            qm = jnp.concatenate([q_ref[r0:r0 + ATT_Q, m * LANES:(m + 1) * LANES] for m in slabs], axis=0) \
                if q_per_k > 1 else q_ref[r0:r0 + ATT_Q, kc * LANES:(kc + 1) * LANES]
            o_acc = jnp.zeros((M, LANES), F32)
            lse_acc = jnp.zeros((M, LANES), F32)
            for half, (keep, kx_all, vx_all) in enumerate(masked):
                kx = kx_all[r0:r0 + W]
                vx = vx_all[r0:r0 + W]
                s = lax.dot_general(qm, kx, (((1,), (1,)), ((), ())), preferred_element_type=F32) + biases[u]
                mx = jnp.max(s, axis=1, keepdims=True)
                if has_sink:
                    sk = jnp.concatenate(
                        [jnp.full((ATT_Q, 1), sink_ref[2 * m + half], F32) for m in slabs], axis=0)
                    mx = jnp.maximum(mx, sk)
                p = jnp.exp(s - mx)
                l = jnp.sum(p, axis=1, keepdims=True)
                if has_sink:
                    l = l + jnp.exp(sk - mx)
                pv = jnp.dot(p.astype(BF16), vx, preferred_element_type=F32)
                o_acc = o_acc + pv / l
                if want_lse:
                    lse_acc = jnp.where(keep, mx + jnp.log(l), lse_acc)
            for t, m in enumerate(slabs):
                if want_lse:
                    _slab_write(o_ref, o_acc[t * ATT_Q:(t + 1) * ATT_Q], first_chunk=m, tok0=r0)
                    _slab_write(o_ref, lse_acc[t * ATT_Q:(t + 1) * ATT_Q], first_chunk=n_q_slabs + m, tok0=r0)
                else:
                    o_ref[r0:r0 + ATT_Q, m * LANES:(m + 1) * LANES] = \
                        o_acc[t * ATT_Q:(t + 1) * ATT_Q].astype(o_ref.dtype)


def _band_attn(q, k, v, segs, *, dil, radius, q_per_k, sink=None, want_lse=False):
    T, wq = q.shape
    wk = k.shape[1]
    (n_p, s_p), (n_s, s_s) = segs
    l_p, l_s = s_p // dil, s_s // dil
    n_sub = ATT_STEP_BLOCKS if l_p % (ATT_STEP_BLOCKS * ATT_Q) == 0 and l_s % (ATT_STEP_BLOCKS * ATT_Q) == 0 else 1
    qs = n_sub * ATT_Q
    assert qs % radius == 0 and l_p % qs == 0 and l_s % qs == 0
    hb = qs // radius
    n_halo = T // radius
    nt = T // qs
    p_tiles = (n_p * s_p) // qs

    def out_map(i):
        in_p = i < p_tiles
        nb = jnp.where(in_p, l_p // qs, l_s // qs)
        i_loc = jnp.where(in_p, i, i - p_tiles)
        n = i_loc // nb
        jb = lax.rem(i_loc, nb)
        b = n // dil
        r = lax.rem(n, dil)
        base = jnp.where(in_p, 0, p_tiles // dil)
        return (base + b * nb + jb, r, 0)

    in_specs = []
    args = []
    if sink is not None:
        in_specs.append(pl.BlockSpec(memory_space=pltpu.SMEM))
        args.append(sink)
    in_specs.append(pl.BlockSpec((qs, wq), lambda i: (i, 0)))
    args.append(q)
    for arr in (k, v):
        in_specs += [
            pl.BlockSpec((radius, wk), lambda i: (jnp.maximum(i * hb - 1, 0), 0)),
            pl.BlockSpec((qs, wk), lambda i: (i, 0)),
            pl.BlockSpec((radius, wk), lambda i: (jnp.minimum((i + 1) * hb, n_halo - 1), 0)),
        ]
        args += [arr, arr, arr]
    if want_lse:
        assert 2 * wq // LANES == SLAB
        if dil == 1:
            out_shape = jax.ShapeDtypeStruct((T * SLAB, LANES), F32)
            out_specs = pl.BlockSpec((qs * SLAB, LANES), lambda i: (i, 0))
        else:
            out_shape = jax.ShapeDtypeStruct((T // dil, dil * SLAB, LANES), F32)
            out_specs = pl.BlockSpec((qs, SLAB, LANES), out_map)
    else:
        assert dil == 1
        out_shape = jax.ShapeDtypeStruct((T, wq), BF16)
        out_specs = pl.BlockSpec((qs, wq), lambda i: (i, 0))
    kern = functools.partial(_band_attn_kernel, segs=segs, dil=dil, radius=radius, q_per_k=q_per_k,
                             has_sink=sink is not None, want_lse=want_lse, n_sub=n_sub)
    out = pl.pallas_call(
        kern, grid=(nt,), in_specs=in_specs, out_specs=out_specs, out_shape=out_shape,
        compiler_params=_cparams(("parallel",)),
    )(*args)
    return out.reshape(T * SLAB, LANES) if want_lse else out


def _split3(v):
    hi = v.astype(BF16)
    r1 = v - hi.astype(F32)
    mid = r1.astype(BF16)
    lo = (r1 - mid.astype(F32)).astype(BF16)
    return hi, mid, lo


def _route(x1, wr_ref, br_ref):
    hi, mid, _ = _split3(x1)
    xs = jnp.concatenate([hi, hi, mid], axis=1)
    lt = lax.dot_general(wr_ref[...], xs, (((1,), (1,)), ((), ())), preferred_element_type=F32) + br_ref[...]
    n_tok = x1.shape[0]
    row = lax.broadcasted_iota(jnp.int32, (MOE_EPG, n_tok), 0)
    big = jnp.int32(MOE_EPG)

    def first_argmax(v):
        m = jnp.max(v, axis=0, keepdims=True)
        return m, jnp.min(jnp.where(v == m, row, big), axis=0, keepdims=True)

    n_e = MOE_GROUPS * MOE_EPG
    gl = lt[n_e:n_e + MOE_EPG]
    gmax, gidx = first_argmax(gl)
    g_w = 1.0 / jnp.sum(jnp.exp(gl - gmax), axis=0, keepdims=True)
    el = lt[(MOE_GROUPS - 1) * MOE_EPG:n_e]
    for g in range(MOE_GROUPS - 2, -1, -1):
        el = jnp.where(gidx == g, lt[g * MOE_EPG:(g + 1) * MOE_EPG], el)
    v1, i1 = first_argmax(el)
    el2 = jnp.where(row == i1, -jnp.inf, el)
    v2, i2 = first_argmax(el2)
    ex = jnp.exp(v2 - v1)
    w1 = 1.0 / (1.0 + ex)
    w2 = ex * w1
    first_lo = i1 < i2
    e_lo = jnp.minimum(i1, i2).astype(F32)
    e_hi = jnp.maximum(i1, i2).astype(F32)
    c_lo = g_w * jnp.where(first_lo, w1, w2)
    c_hi = g_w * jnp.where(first_lo, w2, w1)
    pad = jnp.zeros((ROUTE_ROWS - 5, n_tok), F32)
    return jnp.concatenate([gidx.astype(F32), e_lo, e_hi, c_lo, c_hi, pad], axis=0)


def _rows_to_lanes(rows, sel):
    pieces = jnp.concatenate(_split3(rows), axis=0)
    return lax.dot_general(pieces, sel, (((0,), (0,)), ((), ())), preferred_element_type=F32)


def _gelu_tanh(x):
    return 0.5 * x * (1.0 + jnp.tanh(0.7978845608028654 * (x + 0.044715 * x * x * x)))


def _silu(x):
    return x / (1.0 + jnp.exp(-x))


def _out_ln_kernel(*refs, mode, n_h, n_x, p_tiles):
    h_refs = refs[:n_h]
    extra_ref = refs[n_h] if mode == "gla" else None
    base = n_h + (1 if mode == "gla" else 0)
    x_refs = refs[base:base + n_x]
    wo_ref, g_ref, b_ref, wr_ref, br_ref, x1_ref, rt_ref = refs[base + n_x:base + n_x + 7]

    if mode == "plain":
        hb = h_refs[0][...]
    elif mode == "rglru":
        gate, hf, hbw = h_refs
        hb = (_gelu_tanh(gate[...].astype(F32)) * (hf[...].astype(F32) + hbw[...].astype(F32))).astype(BF16)
    elif mode == "gla":
        of, ob, gg = h_refs
        o = of[...] + ob[...]
        parts = []
        for h in range(GLA_HEADS):
            oh = o[:, h * GLA_DVH:(h + 1) * GLA_DVH]
            ms = jnp.mean(oh * oh, axis=-1, keepdims=True)
            parts.append(oh * lax.rsqrt(ms + LN_EPS) * extra_ref[...])
        o = jnp.concatenate(parts, axis=1)
        hb = (o * _silu(gg[...].astype(F32))).astype(BF16)
    else:
        half = SLAB // 2
        os_ = [_slab_read(r, range(half)) for r in h_refs]
        ls_ = [_slab_read(r, range(half, SLAB)) for r in h_refs]
        mx = functools.reduce(jnp.maximum, ls_)
        es = [jnp.exp(l - mx) for l in ls_]
        den = functools.reduce(jnp.add, es)
        o = functools.reduce(jnp.add, [(e / den) * ov for e, ov in zip(es, os_)])
        hb = o.astype(BF16)

    acc = jnp.dot(hb, wo_ref[...], preferred_element_type=F32)
    x1 = _layer_norm(ALPHA * _x_read(x_refs, p_tiles) + acc, g_ref[...], b_ref[...])
    _slab_write(x1_ref, x1)
    rt_ref[...] = _route(x1, wr_ref, br_ref)


def _out_ln(hs, x, T, wo, ln_g, ln_b, wr, br, *, mode, extra=None):
    D = D_MODEL
    tm = TOK_TILE
    rows = lambda h: tm * SLAB if mode == "dil" else tm
    in_specs = [pl.BlockSpec((rows(h), h.shape[1]), lambda i: (i, 0)) for h in hs]
    args = list(hs)
    if mode == "gla":
        in_specs.append(pl.BlockSpec((1, extra.shape[1]), lambda i: (0, 0)))
        args.append(extra)
    x_specs, x_args, p_tiles = _x_specs(x, tm)
    in_specs += x_specs + [
        pl.BlockSpec(wo.shape, lambda i: (0, 0)),
        pl.BlockSpec((1, D), lambda i: (0, 0)),
        pl.BlockSpec((1, D), lambda i: (0, 0)),
        pl.BlockSpec(wr.shape, lambda i: (0, 0)),
        pl.BlockSpec(br.shape, lambda i: (0, 0)),
    ]
    args += x_args + [wo, ln_g, ln_b, wr, br]
    return pl.pallas_call(
        functools.partial(_out_ln_kernel, mode=mode, n_h=len(hs), n_x=len(x_args), p_tiles=p_tiles),
        grid=(T // tm,),
        in_specs=in_specs,
        out_specs=[pl.BlockSpec((tm * SLAB, LANES), lambda i: (i, 0)),
                   pl.BlockSpec((ROUTE_ROWS, tm), lambda i: (0, i))],
        out_shape=[jax.ShapeDtypeStruct((T * SLAB, LANES), F32),
                   jax.ShapeDtypeStruct((ROUTE_ROWS, T), F32)],
        compiler_params=_cparams(("parallel",)),
    )(*args)


def _moe_kernel(tg_ref, tlo_ref, thi_ref, nv_ref, base_ref, order_ref, x_hbm, c_hbm, sel_ref, wgl_ref, wul_ref,
                wdl_ref, wgh_ref, wuh_ref, wdh_ref, g_ref, b_ref, out_hbm, xbuf, obuf, cbuf, cslab, gsem, ssem,
                csem, *, nt, n_tok):
    i = pl.program_id(0)
    tm = xbuf.shape[1] // SLAB
    win = cbuf.shape[2]
    slot = lax.rem(i, 2)
    used = nv_ref[i] > 0
    nxt = jnp.minimum(i + 1, nt - 1)
    next_used = jnp.logical_and(i + 1 < nt, nv_ref[nxt] > 0)

    def token(ref, tok):
        return ref.at[pl.ds(pl.multiple_of(tok * SLAB, SLAB), SLAB)]

    def weights_copy(tile, sl):
        a = pl.multiple_of(jnp.bitwise_and(base_ref[tile], -LANES), LANES)
        return pltpu.make_async_copy(c_hbm.at[:, pl.ds(a, win)], cbuf.at[sl], csem.at[sl])

    def gather_start(tile, sl):
        base, last = base_ref[tile], nv_ref[tile] - 1
        for r in range(tm):
            tok = order_ref[base + jnp.minimum(r, last)]
            pltpu.make_async_copy(token(x_hbm, tok), xbuf.at[sl, pl.ds(r * SLAB, SLAB)], gsem.at[sl]).start()
        weights_copy(tile, sl).start()

    def gather_wait(tile, sl):
        pltpu.make_async_copy(x_hbm.at[pl.ds(0, tm * SLAB)], xbuf.at[sl], gsem.at[sl]).wait()
        weights_copy(tile, sl).wait()

    def scatter_start(tile, sl):
        base, n = base_ref[tile], nv_ref[tile]
        dump = n_tok + sl * tm
        for r in range(tm):
            tok = jnp.where(r < n, order_ref[base + jnp.minimum(r, n - 1)], dump + r)
            pltpu.make_async_copy(obuf.at[sl, pl.ds(r * SLAB, SLAB)], token(out_hbm, tok), ssem.at[sl]).start()

    def scatter_wait(sl):
        pltpu.make_async_copy(obuf.at[sl], out_hbm.at[pl.ds(0, tm * SLAB)], ssem.at[sl]).wait()

    @pl.when(i == 0)
    def _():
        obuf[...] = jnp.zeros_like(obuf)
        for sl in range(2):
            cp = pltpu.make_async_copy(obuf.at[sl], out_hbm.at[pl.ds((n_tok + sl * tm) * SLAB, tm * SLAB)],
                                       ssem.at[sl])
            cp.start()
            cp.wait()

    @pl.when(jnp.logical_and(i == 0, used))
    def _():
        gather_start(0, 0)

    @pl.when(used)
    def _():
        gather_wait(i, slot)

        @pl.when(next_used)
        def _():
            gather_start(i + 1, 1 - slot)

        @pl.when(i >= 2)
        def _():
            scatter_wait(slot)

        x = _slab_read(xbuf.at[slot])
        cslab[...] = _rows_to_lanes(cbuf[slot], sel_ref[...])
        cw = cslab[pl.ds(jnp.bitwise_and(base_ref[i], LANES - 1), tm), :]
        xb = x.astype(BF16)
        acc = jnp.zeros((tm, D_MODEL), F32)
        for wg_ref, wu_ref, wd_ref, lane in ((wgl_ref, wul_ref, wdl_ref, 0), (wgh_ref, wuh_ref, wdh_ref, 1)):
            hg = jnp.dot(xb, wg_ref[0, 0, 0], preferred_element_type=F32)
            hu = jnp.dot(xb, wu_ref[0, 0, 0], preferred_element_type=F32)
            h = _silu(hg) * hu * cw[:, lane:lane + 1]
            acc = acc + jnp.dot(h.astype(BF16), wd_ref[0, 0, 0], preferred_element_type=F32)
        _slab_write(obuf.at[slot], _layer_norm(ALPHA * x + acc, g_ref[...], b_ref[...]))
        scatter_start(i, slot)

        @pl.when(jnp.logical_not(next_used))
        def _():
            @pl.when(i >= 1)
            def _():
                scatter_wait(1 - slot)
            scatter_wait(slot)


_PAIR_LO = tuple(a for a in range(MOE_EPG) for b in range(a + 1, MOE_EPG))
_PAIR_HI = tuple(b for a in range(MOE_EPG) for b in range(a + 1, MOE_EPG))
N_PAIRS = len(_PAIR_LO)
N_CLASSES = MOE_GROUPS * N_PAIRS


def _moe_schedule(rt, tm, nt):
    T = rt.shape[1]
    i32 = jnp.int32
    g, lo, hi = rt[0].astype(i32), rt[1].astype(i32), rt[2].astype(i32)
    cls = g * N_PAIRS + lo * (2 * MOE_EPG - 1 - lo) // 2 + (hi - lo - 1)
    w_lo, w_hi = rt[3], rt[4]
    cls_sorted, order, w_lo, w_hi = lax.sort((cls, jnp.arange(T, dtype=i32), w_lo, w_hi), num_keys=1,
                                             is_stable=True)
    starts = jnp.searchsorted(cls_sorted, jnp.arange(N_CLASSES + 1, dtype=i32), side="left",
                              method="compare_all").astype(i32)
    counts = starts[1:] - starts[:-1]
    tiles_per = (counts + tm - 1) // tm
    cum = jnp.cumsum(tiles_per)
    n_used = cum[-1]
    t = jnp.arange(nt, dtype=i32)
    tc = jnp.minimum(t, n_used - 1)
    tcls = jnp.searchsorted(cum, tc, side="right", method="compare_all").astype(i32)
    k = tc - (cum[tcls] - tiles_per[tcls])
    nvalid = jnp.where(t < n_used, jnp.clip(counts[tcls] - k * tm, 0, tm), 0).astype(i32)
    base = (starts[tcls] + k * tm).astype(i32)
    pr = tcls % N_PAIRS
    tile_lo = jnp.asarray(_PAIR_LO, i32)[pr]
    tile_hi = jnp.asarray(_PAIR_HI, i32)[pr]
    c_sorted = jnp.zeros((ROUTE_ROWS, T + tm + LANES), F32).at[0, :T].set(w_lo).at[1, :T].set(w_hi)
    return (tcls // N_PAIRS).astype(i32), tile_lo, tile_hi, nvalid, base, order, c_sorted


def _moe(x1, rt, layer, wg, wu, wd, ln_g, ln_b):
    T = x1.shape[0] // SLAB
    D = D_MODEL
    tm = MOE_TILE
    nt = T // tm + N_CLASSES
    win = tm + LANES
    *sched, c_sorted = _moe_schedule(rt, tm, nt)
    sel = jnp.tile(jnp.eye(ROUTE_ROWS, LANES, dtype=BF16), (3, 1))
    lo_map = lambda i, tg, tlo, thi, nv, base, order: (layer, tg[i], tlo[i], 0, 0)
    hi_map = lambda i, tg, tlo, thi, nv, base, order: (layer, tg[i], thi[i], 0, 0)
    const = lambda i, tg, tlo, thi, nv, base, order: (0, 0)
    grid_spec = pltpu.PrefetchScalarGridSpec(
        num_scalar_prefetch=6,
        grid=(nt,),
        in_specs=[
            pl.BlockSpec(memory_space=pl.ANY),
            pl.BlockSpec(memory_space=pl.ANY),
            pl.BlockSpec(sel.shape, const),
            pl.BlockSpec((1, 1, 1, D, MOE_FF), lo_map),
            pl.BlockSpec((1, 1, 1, D, MOE_FF), lo_map),
            pl.BlockSpec((1, 1, 1, MOE_FF, D), lo_map),
            pl.BlockSpec((1, 1, 1, D, MOE_FF), hi_map),
            pl.BlockSpec((1, 1, 1, D, MOE_FF), hi_map),
            pl.BlockSpec((1, 1, 1, MOE_FF, D), hi_map),
            pl.BlockSpec((1, D), const),
            pl.BlockSpec((1, D), const),
        ],
        out_specs=pl.BlockSpec(memory_space=pl.ANY),
        scratch_shapes=[pltpu.VMEM((2, tm * SLAB, LANES), F32), pltpu.VMEM((2, tm * SLAB, LANES), F32),
                        pltpu.VMEM((2, ROUTE_ROWS, win), F32), pltpu.VMEM((win, LANES), F32),
                        pltpu.SemaphoreType.DMA((2,)), pltpu.SemaphoreType.DMA((2,)),
                        pltpu.SemaphoreType.DMA((2,))],
    )
    return pl.pallas_call(
        functools.partial(_moe_kernel, nt=nt, n_tok=T),
        grid_spec=grid_spec,
        out_shape=jax.ShapeDtypeStruct(((T + 2 * tm) * SLAB, LANES), F32),
        compiler_params=_cparams(("arbitrary",)),
    )(*sched, x1, c_sorted, sel, wg, wu, wd, wg, wu, wd, ln_g, ln_b)


def _rglru_stream(z, u_ref, up_ref, un_ref, w_ref, cw_ref, cb_ref, ba_ref, bi_ref, lam_ref,
                  h_out_ref, a_s, b_s, h_s, carry_ref, prev_ok, next_ok, reset):
    tt = u_ref.shape[0]
    u_mid = u_ref[...]
    up = jnp.where(prev_ok, up_ref[...], 0.0)
    un = jnp.where(next_ok, un_ref[...], 0.0)
    head = jnp.concatenate([up, u_mid[:SUBLANES]], axis=0)
    tail = jnp.concatenate([u_mid[tt - SUBLANES:], un], axis=0)
    left = RG_CONV // 2
    u = cb_ref[...]
    for kk in range(RG_CONV):
        s = kk - left
        if s == 0:
            shifted = u_mid
        else:
            rolled = pltpu.roll(u_mid, (-s) % tt, 0)
            if s < 0:
                shifted = jnp.concatenate([head[SUBLANES + s:2 * SUBLANES + s], rolled[SUBLANES:]], axis=0)
            else:
                shifted = jnp.concatenate([rolled[:tt - SUBLANES], tail[s:SUBLANES + s]], axis=0)
        u = u + cw_ref[kk:kk + 1, :] * shifted
    ub = u.astype(BF16)
    sp = jnp.maximum(-lam_ref[...], 0.0) + jnp.log(1.0 + jnp.exp(-jnp.abs(lam_ref[...])))
    for n in range(RG_NT):
        s0 = min(max(n - 1, 0), RG_NT - 3) * LANES
        zz = jnp.dot(ub[:, s0:s0 + RG_BAND], w_ref[n], preferred_element_type=F32)
        sl = slice(n * LANES, (n + 1) * LANES)
        r = jax.nn.sigmoid(zz[:, :LANES] + ba_ref[:, sl])
        ig = jax.nn.sigmoid(zz[:, LANES:] + bi_ref[:, sl])
        log_a = -RG_C * r * sp[:, sl]
        a = jnp.exp(log_a)
        a_s[:, sl] = a
        b_s[:, sl] = jnp.sqrt(-jnp.tanh(log_a) * (a * a + 1.0)) * ig * u[:, sl]

    @pl.when(reset)
    def _():
        carry_ref[...] = jnp.zeros_like(carry_ref)

    n_grp = tt // SUBLANES

    def body(gi, h):
        g = gi if z == 0 else n_grp - 1 - gi
        base = pl.multiple_of(g * SUBLANES, SUBLANES)
        a8 = a_s[pl.ds(base, SUBLANES), :]
        b8 = b_s[pl.ds(base, SUBLANES), :]
        rows = [None] * SUBLANES
        order = range(SUBLANES) if z == 0 else range(SUBLANES - 1, -1, -1)
        for r_ in order:
            h = a8[r_:r_ + 1] * h + b8[r_:r_ + 1]
            rows[r_] = h
        h_s[pl.ds(base, SUBLANES), :] = jnp.concatenate(rows, axis=0)
        return h

    h_last = lax.fori_loop(0, n_grp, body, carry_ref[...])
    carry_ref[...] = h_last
    h_out_ref[...] = h_s[...].astype(h_out_ref.dtype)


def _rglru_scan_kernel(uf_ref, ufp_ref, ufn_ref, ubk_ref, ubp_ref, ubn_ref, wf_ref, wb_ref, cw_ref, cb_ref,
                       ba_ref, bi_ref, lam_ref, hf_ref, hb_ref, a_s, b_s, h_s, cf_ref, cbk_ref, *, segs, nt):
    i = pl.program_id(0)
    tt = uf_ref.shape[0]
    _, tps, j = _seq_pos(i, tt, segs)
    _rglru_stream(0, uf_ref, ufp_ref, ufn_ref, wf_ref, cw_ref, cb_ref, ba_ref.at[0:1], bi_ref.at[0:1],
                  lam_ref.at[0:1], hf_ref, a_s, b_s, h_s, cf_ref, j > 0, j < tps - 1, j == 0)
    ib = nt - 1 - i
    _, tps_b, jb = _seq_pos(ib, tt, segs)
    _rglru_stream(1, ubk_ref, ubp_ref, ubn_ref, wb_ref, cw_ref, cb_ref, ba_ref.at[1:2], bi_ref.at[1:2],
                  lam_ref.at[1:2], hb_ref, a_s, b_s, h_s, cbk_ref, jb > 0, jb < tps_b - 1, jb == tps_b - 1)


def _rglru_scan(u_pre, w_band, conv_w, conv_b, ba, bi, lam, segs):
    T, W = u_pre.shape
    tt = SCAN_TILE
    nt = T // tt
    hpt = tt // SUBLANES
    n_h = T // SUBLANES
    fwd = lambda i: (i, 0)
    bwd = lambda i: (nt - 1 - i, 0)

    def halo_specs(idx):
        return [
            pl.BlockSpec((tt, W), lambda i: (idx(i), 0)),
            pl.BlockSpec((SUBLANES, W), lambda i: (jnp.maximum(idx(i) * hpt - 1, 0), 0)),
            pl.BlockSpec((SUBLANES, W), lambda i: (jnp.minimum((idx(i) + 1) * hpt, n_h - 1), 0)),
        ]

    const2 = lambda i: (0, 0)
    const3 = lambda i: (0, 0, 0)
    in_specs = halo_specs(lambda i: i) + halo_specs(lambda i: nt - 1 - i) + [
        pl.BlockSpec(w_band.shape[1:], const3),
        pl.BlockSpec(w_band.shape[1:], const3),
        pl.BlockSpec(conv_w.shape, const2),
        pl.BlockSpec(conv_b.shape, const2),
        pl.BlockSpec(ba.shape, const2),
        pl.BlockSpec(bi.shape, const2),
        pl.BlockSpec(lam.shape, const2),
    ]
    return pl.pallas_call(
        functools.partial(_rglru_scan_kernel, segs=segs, nt=nt),
        grid=(nt,),
        in_specs=in_specs,
        out_specs=[pl.BlockSpec((tt, W), fwd), pl.BlockSpec((tt, W), bwd)],
        out_shape=[jax.ShapeDtypeStruct((T, W), BF16), jax.ShapeDtypeStruct((T, W), BF16)],
        scratch_shapes=[pltpu.VMEM((tt, W), F32), pltpu.VMEM((tt, W), F32), pltpu.VMEM((tt, W), F32),
                        pltpu.VMEM((1, W), F32), pltpu.VMEM((1, W), F32)],
        compiler_params=_cparams(("arbitrary",)),
    )(u_pre, u_pre, u_pre, u_pre, u_pre, u_pre, w_band[0], w_band[1], conv_w, conv_b, ba, bi, lam)


def _rglru_band_weights(wa, wi):
    def dense(w):
        eye = jnp.eye(RG_BLOCKS, dtype=w.dtype)
        return jnp.einsum("ncd,nm->ncmd", w, eye).reshape(RG_WIDTH, RG_WIDTH)

    out = []
    for z in range(2):
        da, di = dense(wa[z]), dense(wi[z])
        tiles = []
        for n in range(RG_NT):
            s0 = min(max(n - 1, 0), RG_NT - 3) * LANES
            sl = slice(n * LANES, (n + 1) * LANES)
            tiles.append(jnp.concatenate([da[s0:s0 + RG_BAND, sl], di[s0:s0 + RG_BAND, sl]], axis=1))
        out.append(jnp.stack(tiles))
    return jnp.stack(out).astype(BF16)


def _gla_stream(z, q_ref, k_ref, v_ref, lr_ref, wa2_ref, ba_ref, o_ref, st_ref, reset):
    tt = q_ref.shape[0]
    C = GLA_CHUNK

    @pl.when(reset)
    def _():
        st_ref[z] = jnp.zeros(st_ref.shape[1:], F32)

    zz = jnp.dot(lr_ref[...].astype(BF16), wa2_ref[:, z * GLA_DK:(z + 1) * GLA_DK],
                 preferred_element_type=F32) + ba_ref[:, z * GLA_DK:(z + 1) * GLA_DK]
    log_a = -(jnp.maximum(-zz, 0.0) + jnp.log(1.0 + jnp.exp(-jnp.abs(zz)))) / GLA_TAU
    ri = lax.broadcasted_iota(jnp.int32, (C, C), 0)
    ci = lax.broadcasted_iota(jnp.int32, (C, C), 1)
    causal = (ri >= ci) if z == 0 else (ri <= ci)
    tri = causal.astype(F32)
    mid = C // 2 if z == 0 else C - 1 - C // 2
    last = C - 1 if z == 0 else 0
    scale = GLA_DKH ** -0.5
    chunks = range(tt // C) if z == 0 else range(tt // C - 1, -1, -1)
    for c in chunks:
        rs = slice(c * C, (c + 1) * C)
        b = jnp.dot(tri, log_a[rs], preferred_element_type=F32, precision=lax.Precision.HIGHEST)
        b_mid = b[mid:mid + 1]
        b_last = b[last:last + 1]
        qc = q_ref[rs, :] * scale
        kc = k_ref[rs, :]
        qd = (qc * jnp.exp(b - b_mid)).astype(BF16)
        kd = (kc * jnp.exp(b_mid - b)).astype(BF16)
        ks = (kc * jnp.exp(b_last - b)).astype(BF16)
        qb = (qc * jnp.exp(b)).astype(BF16)
        dec = jnp.exp(b_last)
        for h in range(GLA_HEADS):
            ksl = slice(h * GLA_DKH, (h + 1) * GLA_DKH)
            vsl = slice(h * GLA_DVH, (h + 1) * GLA_DVH)
            vc = v_ref[rs, vsl]
            att = lax.dot_general(qd[:, ksl], kd[:, ksl], (((1,), (1,)), ((), ())), preferred_element_type=F32)
            att = jnp.where(causal, att, 0.0)
            o = jnp.dot(att.astype(BF16), vc, preferred_element_type=F32)
            st = st_ref[z, h]
            o = o + lax.dot_general(qb[:, ksl], st.astype(BF16), (((1,), (1,)), ((), ())),
                                    preferred_element_type=F32)
            o_ref[rs, vsl] = o
            upd = lax.dot_general(vc, ks[:, ksl], (((0,), (0,)), ((), ())), preferred_element_type=F32)
            st_ref[z, h] = st * dec[:, ksl] + upd


def _gla_kernel(qf, kf, vf, lf, qb, kb, vb, lb, wa2_ref, ba_ref, of_ref, ob_ref, st_ref, *, segs, nt):
    i = pl.program_id(0)
    tt = qf.shape[0]
    _, tps, j = _seq_pos(i, tt, segs)
    _gla_stream(0, qf, kf, vf, lf, wa2_ref, ba_ref, of_ref, st_ref, j == 0)
    ib = nt - 1 - i
    _, tps_b, jb = _seq_pos(ib, tt, segs)
    _gla_stream(1, qb, kb, vb, lb, wa2_ref, ba_ref, ob_ref, st_ref, jb == tps_b - 1)


def _gla(q, k, v, lr, wa2, ba, segs):
    T = q.shape[0]
    tt = SCAN_TILE
    nt = T // tt
    fwd = lambda i: (i, 0)
    bwd = lambda i: (nt - 1 - i, 0)
    in_specs = []
    for idx in (fwd, bwd):
        in_specs += [pl.BlockSpec((tt, GLA_DK), idx), pl.BlockSpec((tt, GLA_DK), idx),
                     pl.BlockSpec((tt, GLA_DV), idx), pl.BlockSpec((tt, LANES), idx)]
    in_specs += [pl.BlockSpec(wa2.shape, lambda i: (0, 0)), pl.BlockSpec(ba.shape, lambda i: (0, 0))]
    return pl.pallas_call(
        functools.partial(_gla_kernel, segs=segs, nt=nt),
        grid=(nt,),
        in_specs=in_specs,
        out_specs=[pl.BlockSpec((tt, GLA_DV), fwd), pl.BlockSpec((tt, GLA_DV), bwd)],
        out_shape=[jax.ShapeDtypeStruct((T, GLA_DV), F32), jax.ShapeDtypeStruct((T, GLA_DV), F32)],
        scratch_shapes=[pltpu.VMEM((2, GLA_HEADS, GLA_DVH, GLA_DKH), F32)],
        compiler_params=_cparams(("arbitrary",)),
    )(q, k, v, lr, q, k, v, lr, wa2, ba)


def _rope_tables(s_max):
    half = HEAD_DIM // 2
    inv = ROPE_THETA ** (-jnp.arange(half, dtype=F32) / half)
    ang = jnp.arange(s_max, dtype=F32)[:, None] * inv[None, :]
    cos, sin = jnp.cos(ang), jnp.sin(ang)
    cos_t = jnp.tile(jnp.concatenate([cos, cos], axis=1), (1, LANES // HEAD_DIM))
    sin_t = jnp.tile(jnp.concatenate([-sin, sin], axis=1), (1, LANES // HEAD_DIM))
    return cos_t, sin_t


def _router_weights(wgr, bgr, wer, ber):
    n_e = MOE_GROUPS * MOE_EPG
    w = jnp.concatenate([jnp.transpose(wer, (0, 2, 1)).reshape(n_e, D_MODEL), wgr.T], axis=0)
    b = jnp.concatenate([ber.reshape(n_e), bgr])
    n_pad = ROUTER_ROWS - w.shape[0]
    w = jnp.pad(w, ((0, n_pad), (0, 0)))
    b = jnp.concatenate([b, jnp.full((n_pad,), NEG_INF, F32)])
    w_hi = w.astype(BF16)
    w_mid = (w - w_hi.astype(F32)).astype(BF16)
    return jnp.concatenate([w_hi, w_mid, w_hi], axis=1), b[:, None]


def _mixer(kind, jl, x, segs, p, cos_t, sin_t, lg1, lb1, wr, br):
    D = D_MODEL
    T = sum(n * s for n, s in segs)
    if kind == 0:
        nq = A_HEADS * HEAD_DIM
        nkv = A_KV_HEADS * HEAD_DIM
        w = p["a_wqkv"][jl]
        wq, wk, wv = w[:, :nq], w[:, nq:nq + nkv], w[:, nq + nkv:]
        dup = lambda t: jnp.repeat(t.reshape(D, A_KV_HEADS, 1, HEAD_DIM), 2, axis=2).reshape(D, 2 * nkv)
        w_ext = jnp.concatenate([wq, dup(wk), dup(wv)], axis=1).astype(BF16)
        q, kd, vd = _proj_rope(x, T, w_ext, cos_t, sin_t, segs, dil=1, n_rope=(nq + 2 * nkv) // LANES,
                               scale_cols=nq, scale=HEAD_DIM ** -0.5, out_cols=(nq, 2 * nkv, 2 * nkv),
                               out_dtypes=(BF16, BF16, BF16))
        o = _band_attn(q, kd, vd, segs, dil=1, radius=A_RADIUS, q_per_k=2, sink=p["a_sink"][jl])
        return _out_ln([o], x, T, p["a_wo"][jl].astype(BF16), lg1, lb1, wr, br, mode="plain")
    if kind == 1:
        gate, u_pre = _proj(x, T, p["b_win"][jl].astype(BF16), (RG_WIDTH, RG_WIDTH), (BF16, F32))
        w_band = _rglru_band_weights(p["b_wa"][jl], p["b_wi"][jl])
        hf, hb = _rglru_scan(u_pre, w_band, p["b_conv_w"][jl], p["b_conv_b"][jl][None, :], p["b_ba"][jl],
                             p["b_bi"][jl], p["b_lam"][jl], segs)
        return _out_ln([gate, hf, hb], x, T, p["b_wo"][jl].astype(BF16), lg1, lb1, wr, br, mode="rglru")
    if kind == 2:
        wa1 = jnp.concatenate([p["c_wa1"][jl][0], p["c_wa1"][jl][1]], axis=1)
        wa1 = jnp.pad(wa1, ((0, 0), (0, LANES - wa1.shape[1])))
        w_all = jnp.concatenate([p["c_wqkvg"][jl], wa1], axis=1).astype(BF16)
        q, k, v, g, lr = _proj(x, T, w_all, (GLA_DK, GLA_DK, GLA_DV, GLA_DV, LANES), (F32, F32, BF16, BF16, F32))
        wa2 = jnp.zeros((LANES, 2 * GLA_DK), F32)
        wa2 = wa2.at[:GLA_RANK, :GLA_DK].set(p["c_wa2"][jl][0])
        wa2 = wa2.at[GLA_RANK:2 * GLA_RANK, GLA_DK:].set(p["c_wa2"][jl][1])
        of, ob = _gla(q, k, v, lr, wa2.astype(BF16), p["c_ba"][jl].reshape(1, 2 * GLA_DK), segs)
        return _out_ln([of, ob, g], x, T, p["c_wo"][jl].astype(BF16), lg1, lb1, wr, br, mode="gla",
                       extra=p["c_norm_g"][jl][None, :])
    ols = []
    for gi, (window, dil) in enumerate(DIL_GROUPS):
        w_g = p["d_wqkv"][jl][:, gi * 3 * DIL_WIDTH:(gi + 1) * 3 * DIL_WIDTH].astype(BF16)
        q, k, v = _proj_rope(x, T, w_g, cos_t, sin_t, segs, dil=dil, n_rope=2 * DIL_WIDTH // LANES,
                             scale_cols=DIL_WIDTH, scale=HEAD_DIM ** -0.5,
                             out_cols=(DIL_WIDTH,) * 3, out_dtypes=(BF16,) * 3)
        ols.append(_band_attn(q, k, v, segs, dil=dil, radius=window // (2 * dil), q_per_k=1, want_lse=True))
    return _out_ln(ols, x, T, p["d_wo"][jl].astype(BF16), lg1, lb1, wr, br, mode="dil")


def kernel(x_prompt, x_sample, ln_g, ln_b, a_wqkv, a_sink, a_wo, b_win, b_conv_w, b_conv_b, b_wa, b_ba, b_wi,
           b_bi, b_lam, b_wo, c_wqkvg, c_wa1, c_wa2, c_ba, c_norm_g, c_wo, d_wqkv, d_wo, m_wgr, m_bgr, m_wer,
           m_ber, m_wg, m_wu, m_wd):
    p = dict(a_wqkv=a_wqkv, a_sink=a_sink, a_wo=a_wo, b_win=b_win, b_conv_w=b_conv_w, b_conv_b=b_conv_b,
             b_wa=b_wa, b_ba=b_ba, b_wi=b_wi, b_bi=b_bi, b_lam=b_lam, b_wo=b_wo, c_wqkvg=c_wqkvg, c_wa1=c_wa1,
             c_wa2=c_wa2, c_ba=c_ba, c_norm_g=c_norm_g, c_wo=c_wo, d_wqkv=d_wqkv, d_wo=d_wo)
    n_p, s_p, D = x_prompt.shape
    n_s, s_s, _ = x_sample.shape
    segs = ((n_p, s_p), (n_s, s_s))
    assert (n_p * s_p) % s_s == 0
    t_p, T = n_p * s_p, n_p * s_p + n_s * s_s
    x = (x_prompt.reshape(t_p, D), x_sample.reshape(T - t_p, D))
    cos_t, sin_t = _rope_tables(max(s_p, s_s))
    wg_b, wu_b, wd_b = m_wg.astype(BF16), m_wu.astype(BF16), m_wd.astype(BF16)

    for layer in range(DEPTH):
        lg1, lb1 = ln_g[layer, 0][None, :], ln_b[layer, 0][None, :]
        lg2, lb2 = ln_g[layer, 1][None, :], ln_b[layer, 1][None, :]
        wr, br = _router_weights(m_wgr[layer], m_bgr[layer], m_wer[layer], m_ber[layer])
        x1, rt = _mixer(layer % 4, layer // 4, x, segs, p, cos_t, sin_t, lg1, lb1, wr, br)
        x = _moe(x1, rt, layer, wg_b, wu_b, wd_b, lg2, lb2)

    return _from_slab(x, 0, t_p).reshape(n_p, s_p, D), _from_slab(x, t_p, T - t_p).reshape(n_s, s_s, D)
```

```python
import functools

import jax
import jax.numpy as jnp
from jax import lax
from jax.experimental import pallas as pl
from jax.experimental.pallas import tpu as pltpu

F32 = jnp.float32
BF16 = jnp.bfloat16

D_MODEL = 1024
HEAD_DIM = 64
ROPE_THETA = 10000.0
A_HEADS = 16
A_KV_HEADS = 4
A_RADIUS = 128
RG_WIDTH = 1408
RG_BLOCKS = 16
RG_BW = RG_WIDTH // RG_BLOCKS
RG_CONV = 4
RG_C = 8.0
GLA_HEADS = 4
GLA_DK = 512
GLA_DV = 1024
GLA_DKH = 128
GLA_DVH = 256
GLA_RANK = 16
GLA_TAU = 16.0
GLA_CHUNK = 64
DIL_GROUPS = ((128, 1), (512, 4), (2048, 16))
N_DIL = 3
DIL_HEADS = 8
DIL_WIDTH = 512
MOE_GROUPS = 4
MOE_EPG = 8
MOE_FF = 256
DEPTH = 4
ALPHA = (2 * DEPTH) ** 0.25
LN_EPS = 1e-5
NEG_INF = -1e30

LANES = 128
SUBLANES = 8
VMEM_LIMIT = 52 * 1024 * 1024
ATT_Q = 128
ATT_STEP_BLOCKS = 4
TOK_TILE = 512
SCAN_TILE = 256
RG_NT = RG_WIDTH // LANES
RG_BAND = 3 * LANES
ROUTER_ROWS = 40
ROUTE_ROWS = 16
MOE_TILE = 128
SLAB = D_MODEL // LANES
PROJ_CHUNK = 512


def _cparams(sem):
    return pltpu.CompilerParams(dimension_semantics=sem, vmem_limit_bytes=VMEM_LIMIT)


def _seq_pos(i, rows, segs, dil=1):
    (n_p, s_p), (_, s_s) = segs
    p_tiles = (n_p * s_p) // rows
    in_p = i < p_tiles
    tps = jnp.where(in_p, (s_p // dil) // rows, (s_s // dil) // rows)
    j = lax.rem(i, tps)
    return in_p, tps, j


def _layer_norm(y, g, b):
    mu = jnp.mean(y, axis=-1, keepdims=True)
    yc = y - mu
    var = jnp.mean(yc * yc, axis=-1, keepdims=True)
    return yc * lax.rsqrt(var + LN_EPS) * g + b


def _rope_slab(t, cos, sin_signed):
    lane = lax.broadcasted_iota(jnp.int32, t.shape, 1)
    first_half = (lane % HEAD_DIM) < (HEAD_DIM // 2)
    partner = jnp.where(first_half, pltpu.roll(t, LANES - HEAD_DIM // 2, 1), pltpu.roll(t, HEAD_DIM // 2, 1))
    return t * cos + partner * sin_signed


def _slab_read(ref, chunks=range(SLAB)):
    if len(ref.shape) == 3:
        return jnp.concatenate([ref[:, c, :] for c in chunks], axis=1)
    rows = ref.shape[0] // SLAB
    return jnp.concatenate([ref[pl.ds(c, rows, stride=SLAB), :] for c in chunks], axis=1)


def _slab_write(ref, val, first_chunk=0, tok0=0):
    n = val.shape[0]
    for c in range(val.shape[1] // LANES):
        v = val[:, c * LANES:(c + 1) * LANES]
        if len(ref.shape) == 3:
            ref[tok0:tok0 + n, first_chunk + c, :] = v
        else:
            ref[pl.ds(tok0 * SLAB + first_chunk + c, n, stride=SLAB), :] = v


def _col_chunks(width):
    out, c = [], 0
    while c < width:
        w = min(PROJ_CHUNK, width - c)
        out.append((c, w))
        c += w
    return out


def _x_specs(x, tm):
    if isinstance(x, tuple):
        xp, xs = x
        p_tiles = xp.shape[0] // tm
        specs = [pl.BlockSpec((tm, D_MODEL), lambda i, *_: (jnp.minimum(i, p_tiles - 1), 0)),
                 pl.BlockSpec((tm, D_MODEL), lambda i, *_: (jnp.maximum(i - p_tiles, 0), 0))]
        return specs, [xp, xs], p_tiles
    return [pl.BlockSpec((tm * SLAB, LANES), lambda i, *_: (i, 0))], [x], 0


def _x_read(x_refs, p_tiles):
    if len(x_refs) == 2:
        return jnp.where(pl.program_id(0) < p_tiles, x_refs[0][...], x_refs[1][...])
    return _slab_read(x_refs[0])


def _proj_rope_kernel(*refs, n_x, p_tiles, n_rope, scale_cols, scale, out_cols):
    x_refs, (w_ref, cos_ref, sin_ref), out_refs = refs[:n_x], refs[n_x:n_x + 3], refs[n_x + 3:]
    xb = _x_read(x_refs, p_tiles).astype(BF16)
    cos = cos_ref[...]
    sin = sin_ref[...]
    col = 0
    for o_ref, width in zip(out_refs, out_cols):
        for c0, cw in _col_chunks(width):
            yc = jnp.dot(xb, w_ref[:, col + c0:col + c0 + cw], preferred_element_type=F32)
            for c in range(cw // LANES):
                slab = (col + c0) // LANES + c
                y = yc[:, c * LANES:(c + 1) * LANES]
                if slab < n_rope:
                    y = _rope_slab(y, cos, sin)
                if slab * LANES < scale_cols:
                    y = y * scale
                o_ref[:, c0 + c * LANES:c0 + (c + 1) * LANES] = y.astype(o_ref.dtype)
        col += width


def _proj_rope(x, T, w, cos_tab, sin_tab, segs, *, dil, n_rope, scale_cols, scale, out_cols, out_dtypes):
    D = D_MODEL
    (n_p, s_p), (n_s, s_s) = segs
    l_p, l_s = s_p // dil, s_s // dil
    tj = min(TOK_TILE, l_s, l_p)
    assert l_p % tj == 0 and l_s % tj == 0 and (n_p * s_p) % (dil * tj) == 0
    if dil == 1:
        x_specs, x_args, x_p_tiles = _x_specs(x, tj)
    else:
        assert x.shape[0] % (dil * SLAB) == 0
        x_args = [x.reshape(x.shape[0] // (dil * SLAB), dil * SLAB, LANES)]
        x_specs, x_p_tiles = [pl.BlockSpec((tj, SLAB, LANES), lambda jt, r: (jt, r, 0))], 0
    cosv = cos_tab.reshape(cos_tab.shape[0] // dil, dil * LANES)
    sinv = sin_tab.reshape(sin_tab.shape[0] // dil, dil * LANES)
    p_tiles = (n_p * l_p) // tj

    def decode(jt):
        in_p = jt < p_tiles
        lt = jnp.where(in_p, l_p // tj, l_s // tj)
        jt_loc = jnp.where(in_p, jt, jt - p_tiles)
        b = jt_loc // lt
        j0 = lax.rem(jt_loc, lt)
        return in_p, lt, b, j0

    def out_map(jt, r):
        in_p, lt, b, j0 = decode(jt)
        base = jnp.where(in_p, 0, (n_p * s_p) // tj)
        return (base + b * (lt * dil) + r * lt + j0, 0)

    def tab_map(jt, r):
        _, _, _, j0 = decode(jt)
        return (j0, r)

    kern = functools.partial(_proj_rope_kernel, n_x=len(x_args), p_tiles=x_p_tiles, n_rope=n_rope,
                             scale_cols=scale_cols, scale=scale, out_cols=out_cols)
    n_out = w.shape[1]
    return pl.pallas_call(
        kern,
        grid=(T // dil // tj, dil),
        in_specs=x_specs + [
            pl.BlockSpec((D, n_out), lambda jt, r: (0, 0)),
            pl.BlockSpec((tj, LANES), tab_map),
            pl.BlockSpec((tj, LANES), tab_map),
        ],
        out_specs=[pl.BlockSpec((tj, c), out_map) for c in out_cols],
        out_shape=[jax.ShapeDtypeStruct((T, c), dt) for c, dt in zip(out_cols, out_dtypes)],
        compiler_params=_cparams(("parallel", "parallel")),
    )(*x_args, w, cosv, sinv)


def _from_slab_kernel(x_ref, o_ref):
    o_ref[...] = _slab_read(x_ref)


def _from_slab(x, tok0, n_tok):
    tm = TOK_TILE
    assert tok0 % tm == 0 and n_tok % tm == 0
    return pl.pallas_call(
        _from_slab_kernel,
        grid=(n_tok // tm,),
        in_specs=[pl.BlockSpec((tm * SLAB, LANES), lambda i: (tok0 // tm + i, 0))],
        out_specs=pl.BlockSpec((tm, D_MODEL), lambda i: (i, 0)),
        out_shape=jax.ShapeDtypeStruct((n_tok, D_MODEL), F32),
        compiler_params=_cparams(("parallel",)),
    )(x)


def _proj_kernel(x_ref, w_ref, *out_refs, out_cols):
    xb = _slab_read(x_ref).astype(BF16)
    col = 0
    for o_ref, width in zip(out_refs, out_cols):
        for c0, cw in _col_chunks(width):
            y = jnp.dot(xb, w_ref[:, col + c0:col + c0 + cw], preferred_element_type=F32)
            o_ref[:, c0:c0 + cw] = y.astype(o_ref.dtype)
        col += width


def _proj(x, T, w, out_cols, out_dtypes):
    D = D_MODEL
    tm = TOK_TILE
    n_out = w.shape[1]
    assert sum(out_cols) == n_out and T % tm == 0
    return pl.pallas_call(
        functools.partial(_proj_kernel, out_cols=out_cols),
        grid=(T // tm,),
        in_specs=[pl.BlockSpec((tm * SLAB, LANES), lambda i: (i, 0)), pl.BlockSpec((D, n_out), lambda i: (0, 0))],
        out_specs=[pl.BlockSpec((tm, c), lambda i: (i, 0)) for c in out_cols],
        out_shape=[jax.ShapeDtypeStruct((T, c), dt) for c, dt in zip(out_cols, out_dtypes)],
        compiler_params=_cparams(("parallel",)),
    )(x, w)


def _band_attn_kernel(*refs, segs, dil, radius, q_per_k, has_sink, want_lse, n_sub):
    it = iter(refs)
    sink_ref = next(it) if has_sink else None
    q_ref = next(it)
    kp_ref, kc_ref, kn_ref = next(it), next(it), next(it)
    vp_ref, vc_ref, vn_ref = next(it), next(it), next(it)
    o_ref = next(it)
    n_q_slabs = q_ref.shape[1] // LANES
    qs = n_sub * ATT_Q

    i = pl.program_id(0)
    _, tps, j = _seq_pos(i, qs, segs, dil)
    W = 2 * radius + ATT_Q
    row = lax.broadcasted_iota(jnp.int32, (ATT_Q, W), 0)
    col = lax.broadcasted_iota(jnp.int32, (ATT_Q, W), 1)
    in_band = jnp.abs(col - radius - row) <= radius

    def block_bias(prev_ok, next_ok):
        ok = in_band
        if prev_ok is not None:
            ok = ok & ((col >= radius) | prev_ok)
        if next_ok is not None:
            ok = ok & ((col < radius + ATT_Q) | next_ok)
        bias = jnp.where(ok, 0.0, NEG_INF).astype(F32)
        return jnp.concatenate([bias] * q_per_k, axis=0) if q_per_k > 1 else bias

    biases = [block_bias(j > 0 if u == 0 else None, j < tps - 1 if u == n_sub - 1 else None)
              for u in range(n_sub)]

    k_all = jnp.concatenate([kp_ref[...], kc_ref[...], kn_ref[...]], axis=0)
    v_all = jnp.concatenate([vp_ref[...], vc_ref[...], vn_ref[...]], axis=0)
    nk = k_all.shape[1] // LANES
    lane_lo = lax.broadcasted_iota(jnp.int32, (1, LANES), 1) < HEAD_DIM
    M = ATT_Q * q_per_k
    zero = jnp.zeros((), BF16)

    for kc in range(nk):
        ksl = k_all[:, kc * LANES:(kc + 1) * LANES]
        vsl = v_all[:, kc * LANES:(kc + 1) * LANES]
        slabs = [kc * q_per_k + t for t in range(q_per_k)]
        masked = []
        for half in range(2):
            keep = lane_lo if half == 0 else jnp.logical_not(lane_lo)
            masked.append((keep, jnp.where(keep, ksl, zero), jnp.where(keep, vsl, zero)))
        for u in range(n_sub):
            r0 = u * ATT_Q
            qm = jnp.concatenate([q_ref[r0:r0 + ATT_Q, m * LANES:(m + 1) * LANES] for m in slabs], axis=0) \
                if q_per_k > 1 else q_ref[r0:r0 + ATT_Q, kc * LANES:(kc + 1) * LANES]
            o_acc = jnp.zeros((M, LANES), F32)
            lse_acc = jnp.zeros((M, LANES), F32)
            for half, (keep, kx_all, vx_all) in enumerate(masked):
                kx = kx_all[r0:r0 + W]
                vx = vx_all[r0:r0 + W]
                s = lax.dot_general(qm, kx, (((1,), (1,)), ((), ())), preferred_element_type=F32) + biases[u]
                mx = jnp.max(s, axis=1, keepdims=True)
                if has_sink:
                    sk = jnp.concatenate(
                        [jnp.full((ATT_Q, 1), sink_ref[2 * m + half], F32) for m in slabs], axis=0)
                    mx = jnp.maximum(mx, sk)
                p = jnp.exp(s - mx)
                l = jnp.sum(p, axis=1, keepdims=True)
                if has_sink:
                    l = l + jnp.exp(sk - mx)
                pv = jnp.dot(p.astype(BF16), vx, preferred_element_type=F32)
                o_acc = o_acc + pv / l
                if want_lse:
                    lse_acc = jnp.where(keep, mx + jnp.log(l), lse_acc)
            for t, m in enumerate(slabs):
                if want_lse:
                    _slab_write(o_ref, o_acc[t * ATT_Q:(t + 1) * ATT_Q], first_chunk=m, tok0=r0)
                    _slab_write(o_ref, lse_acc[t * ATT_Q:(t + 1) * ATT_Q], first_chunk=n_q_slabs + m, tok0=r0)
                else:
                    o_ref[r0:r0 + ATT_Q, m * LANES:(m + 1) * LANES] = \
                        o_acc[t * ATT_Q:(t + 1) * ATT_Q].astype(o_ref.dtype)


def _band_attn(q, k, v, segs, *, dil, radius, q_per_k, sink=None, want_lse=False):
    T, wq = q.shape
    wk = k.shape[1]
    (n_p, s_p), (n_s, s_s) = segs
    l_p, l_s = s_p // dil, s_s // dil
    n_sub = ATT_STEP_BLOCKS if l_p % (ATT_STEP_BLOCKS * ATT_Q) == 0 and l_s % (ATT_STEP_BLOCKS * ATT_Q) == 0 else 1
    qs = n_sub * ATT_Q
    assert qs % radius == 0 and l_p % qs == 0 and l_s % qs == 0
    hb = qs // radius
    n_halo = T // radius
    nt = T // qs
    p_tiles = (n_p * s_p) // qs

    def out_map(i):
        in_p = i < p_tiles
        nb = jnp.where(in_p, l_p // qs, l_s // qs)
        i_loc = jnp.where(in_p, i, i - p_tiles)
        n = i_loc // nb
        jb = lax.rem(i_loc, nb)
        b = n // dil
        r = lax.rem(n, dil)
        base = jnp.where(in_p, 0, p_tiles // dil)
        return (base + b * nb + jb, r, 0)

    in_specs = []
    args = []
    if sink is not None:
        in_specs.append(pl.BlockSpec(memory_space=pltpu.SMEM))
        args.append(sink)
    in_specs.append(pl.BlockSpec((qs, wq), lambda i: (i, 0)))
    args.append(q)
    for arr in (k, v):
        in_specs += [
            pl.BlockSpec((radius, wk), lambda i: (jnp.maximum(i * hb - 1, 0), 0)),
            pl.BlockSpec((qs, wk), lambda i: (i, 0)),
            pl.BlockSpec((radius, wk), lambda i: (jnp.minimum((i + 1) * hb, n_halo - 1), 0)),
        ]
        args += [arr, arr, arr]
    if want_lse:
        assert 2 * wq // LANES == SLAB
        if dil == 1:
            out_shape = jax.ShapeDtypeStruct((T * SLAB, LANES), F32)
            out_specs = pl.BlockSpec((qs * SLAB, LANES), lambda i: (i, 0))
        else:
            out_shape = jax.ShapeDtypeStruct((T // dil, dil * SLAB, LANES), F32)
            out_specs = pl.BlockSpec((qs, SLAB, LANES), out_map)
    else:
        assert dil == 1
        out_shape = jax.ShapeDtypeStruct((T, wq), BF16)
        out_specs = pl.BlockSpec((qs, wq), lambda i: (i, 0))
    kern = functools.partial(_band_attn_kernel, segs=segs, dil=dil, radius=radius, q_per_k=q_per_k,
                             has_sink=sink is not None, want_lse=want_lse, n_sub=n_sub)
    out = pl.pallas_call(
        kern, grid=(nt,), in_specs=in_specs, out_specs=out_specs, out_shape=out_shape,
        compiler_params=_cparams(("parallel",)),
    )(*args)
    return out.reshape(T * SLAB, LANES) if want_lse else out


def _split3(v):
    hi = v.astype(BF16)
    r1 = v - hi.astype(F32)
    mid = r1.astype(BF16)
    lo = (r1 - mid.astype(F32)).astype(BF16)
    return hi, mid, lo


def _route(x1, wr_ref, br_ref):
    hi, mid, _ = _split3(x1)
    xs = jnp.concatenate([hi, hi, mid], axis=1)
    lt = lax.dot_general(wr_ref[...], xs, (((1,), (1,)), ((), ())), preferred_element_type=F32) + br_ref[...]
    n_tok = x1.shape[0]
    row = lax.broadcasted_iota(jnp.int32, (MOE_EPG, n_tok), 0)
    big = jnp.int32(MOE_EPG)

    def first_argmax(v):
        m = jnp.max(v, axis=0, keepdims=True)
        return m, jnp.min(jnp.where(v == m, row, big), axis=0, keepdims=True)

    n_e = MOE_GROUPS * MOE_EPG
    gl = lt[n_e:n_e + MOE_EPG]
    gmax, gidx = first_argmax(gl)
    g_w = 1.0 / jnp.sum(jnp.exp(gl - gmax), axis=0, keepdims=True)
    el = lt[(MOE_GROUPS - 1) * MOE_EPG:n_e]
    for g in range(MOE_GROUPS - 2, -1, -1):
        el = jnp.where(gidx == g, lt[g * MOE_EPG:(g + 1) * MOE_EPG], el)
    v1, i1 = first_argmax(el)
    el2 = jnp.where(row == i1, -jnp.inf, el)
    v2, i2 = first_argmax(el2)
    ex = jnp.exp(v2 - v1)
    w1 = 1.0 / (1.0 + ex)
    w2 = ex * w1
    first_lo = i1 < i2
    e_lo = jnp.minimum(i1, i2).astype(F32)
    e_hi = jnp.maximum(i1, i2).astype(F32)
    c_lo = g_w * jnp.where(first_lo, w1, w2)
    c_hi = g_w * jnp.where(first_lo, w2, w1)
    pad = jnp.zeros((ROUTE_ROWS - 5, n_tok), F32)
    return jnp.concatenate([gidx.astype(F32), e_lo, e_hi, c_lo, c_hi, pad], axis=0)


def _rows_to_lanes(rows, sel):
    pieces = jnp.concatenate(_split3(rows), axis=0)
    return lax.dot_general(pieces, sel, (((0,), (0,)), ((), ())), preferred_element_type=F32)


def _gelu_tanh(x):
    return 0.5 * x * (1.0 + jnp.tanh(0.7978845608028654 * (x + 0.044715 * x * x * x)))


def _silu(x):
    return x / (1.0 + jnp.exp(-x))


def _out_ln_kernel(*refs, mode, n_h, n_x, p_tiles):
    h_refs = refs[:n_h]
    extra_ref = refs[n_h] if mode == "gla" else None
    base = n_h + (1 if mode == "gla" else 0)
    x_refs = refs[base:base + n_x]
    wo_ref, g_ref, b_ref, wr_ref, br_ref, x1_ref, rt_ref = refs[base + n_x:base + n_x + 7]

    if mode == "plain":
        hb = h_refs[0][...]
    elif mode == "rglru":
        gate, hf, hbw = h_refs
        hb = (_gelu_tanh(gate[...].astype(F32)) * (hf[...].astype(F32) + hbw[...].astype(F32))).astype(BF16)
    elif mode == "gla":
        of, ob, gg = h_refs
        o = of[...] + ob[...]
        parts = []
        for h in range(GLA_HEADS):
            oh = o[:, h * GLA_DVH:(h + 1) * GLA_DVH]
            ms = jnp.mean(oh * oh, axis=-1, keepdims=True)
            parts.append(oh * lax.rsqrt(ms + LN_EPS) * extra_ref[...])
        o = jnp.concatenate(parts, axis=1)
        hb = (o * _silu(gg[...].astype(F32))).astype(BF16)
    else:
        half = SLAB // 2
        os_ = [_slab_read(r, range(half)) for r in h_refs]
        ls_ = [_slab_read(r, range(half, SLAB)) for r in h_refs]
        mx = functools.reduce(jnp.maximum, ls_)
        es = [jnp.exp(l - mx) for l in ls_]
        den = functools.reduce(jnp.add, es)
        o = functools.reduce(jnp.add, [(e / den) * ov for e, ov in zip(es, os_)])
        hb = o.astype(BF16)

    acc = jnp.dot(hb, wo_ref[...], preferred_element_type=F32)
    x1 = _layer_norm(ALPHA * _x_read(x_refs, p_tiles) + acc, g_ref[...], b_ref[...])
    _slab_write(x1_ref, x1)
    rt_ref[...] = _route(x1, wr_ref, br_ref)


def _out_ln(hs, x, T, wo, ln_g, ln_b, wr, br, *, mode, extra=None):
    D = D_MODEL
    tm = TOK_TILE
    rows = lambda h: tm * SLAB if mode == "dil" else tm
    in_specs = [pl.BlockSpec((rows(h), h.shape[1]), lambda i: (i, 0)) for h in hs]
    args = list(hs)
    if mode == "gla":
        in_specs.append(pl.BlockSpec((1, extra.shape[1]), lambda i: (0, 0)))
        args.append(extra)
    x_specs, x_args, p_tiles = _x_specs(x, tm)
    in_specs += x_specs + [
        pl.BlockSpec(wo.shape, lambda i: (0, 0)),
        pl.BlockSpec((1, D), lambda i: (0, 0)),
        pl.BlockSpec((1, D), lambda i: (0, 0)),
        pl.BlockSpec(wr.shape, lambda i: (0, 0)),
        pl.BlockSpec(br.shape, lambda i: (0, 0)),
    ]
    args += x_args + [wo, ln_g, ln_b, wr, br]
    return pl.pallas_call(
        functools.partial(_out_ln_kernel, mode=mode, n_h=len(hs), n_x=len(x_args), p_tiles=p_tiles),
        grid=(T // tm,),
        in_specs=in_specs,
        out_specs=[pl.BlockSpec((tm * SLAB, LANES), lambda i: (i, 0)),
                   pl.BlockSpec((ROUTE_ROWS, tm), lambda i: (0, i))],
        out_shape=[jax.ShapeDtypeStruct((T * SLAB, LANES), F32),
                   jax.ShapeDtypeStruct((ROUTE_ROWS, T), F32)],
        compiler_params=_cparams(("parallel",)),
    )(*args)


def _moe_kernel(tg_ref, tlo_ref, thi_ref, nv_ref, base_ref, order_ref, x_hbm, c_hbm, sel_ref, wgl_ref, wul_ref,
                wdl_ref, wgh_ref, wuh_ref, wdh_ref, g_ref, b_ref, out_hbm, xbuf, obuf, cbuf, cslab, gsem, ssem,
                csem, *, nt, n_tok):
    i = pl.program_id(0)
    tm = xbuf.shape[1] // SLAB
    win = cbuf.shape[2]
    slot = lax.rem(i, 2)
    used = nv_ref[i] > 0
    nxt = jnp.minimum(i + 1, nt - 1)
    next_used = jnp.logical_and(i + 1 < nt, nv_ref[nxt] > 0)

    def token(ref, tok):
        return ref.at[pl.ds(pl.multiple_of(tok * SLAB, SLAB), SLAB)]

    def weights_copy(tile, sl):
        a = pl.multiple_of(jnp.bitwise_and(base_ref[tile], -LANES), LANES)
        return pltpu.make_async_copy(c_hbm.at[:, pl.ds(a, win)], cbuf.at[sl], csem.at[sl])

    def gather_start(tile, sl):
        base, last = base_ref[tile], nv_ref[tile] - 1
        for r in range(tm):
            tok = order_ref[base + jnp.minimum(r, last)]
            pltpu.make_async_copy(token(x_hbm, tok), xbuf.at[sl, pl.ds(r * SLAB, SLAB)], gsem.at[sl]).start()
        weights_copy(tile, sl).start()

    def gather_wait(tile, sl):
        pltpu.make_async_copy(x_hbm.at[pl.ds(0, tm * SLAB)], xbuf.at[sl], gsem.at[sl]).wait()
        weights_copy(tile, sl).wait()

    def scatter_start(tile, sl):
        base, n = base_ref[tile], nv_ref[tile]
        dump = n_tok + sl * tm
        for r in range(tm):
            tok = jnp.where(r < n, order_ref[base + jnp.minimum(r, n - 1)], dump + r)
            pltpu.make_async_copy(obuf.at[sl, pl.ds(r * SLAB, SLAB)], token(out_hbm, tok), ssem.at[sl]).start()

    def scatter_wait(sl):
        pltpu.make_async_copy(obuf.at[sl], out_hbm.at[pl.ds(0, tm * SLAB)], ssem.at[sl]).wait()

    @pl.when(i == 0)
    def _():
        obuf[...] = jnp.zeros_like(obuf)
        for sl in range(2):
            cp = pltpu.make_async_copy(obuf.at[sl], out_hbm.at[pl.ds((n_tok + sl * tm) * SLAB, tm * SLAB)],
                                       ssem.at[sl])
            cp.start()
            cp.wait()

    @pl.when(jnp.logical_and(i == 0, used))
    def _():
        gather_start(0, 0)

    @pl.when(used)
    def _():
        gather_wait(i, slot)

        @pl.when(next_used)
        def _():
            gather_start(i + 1, 1 - slot)

        @pl.when(i >= 2)
        def _():
            scatter_wait(slot)

        x = _slab_read(xbuf.at[slot])
        cslab[...] = _rows_to_lanes(cbuf[slot], sel_ref[...])
        cw = cslab[pl.ds(jnp.bitwise_and(base_ref[i], LANES - 1), tm), :]
        xb = x.astype(BF16)
        acc = jnp.zeros((tm, D_MODEL), F32)
        for wg_ref, wu_ref, wd_ref, lane in ((wgl_ref, wul_ref, wdl_ref, 0), (wgh_ref, wuh_ref, wdh_ref, 1)):
            hg = jnp.dot(xb, wg_ref[0, 0, 0], preferred_element_type=F32)
            hu = jnp.dot(xb, wu_ref[0, 0, 0], preferred_element_type=F32)
            h = _silu(hg) * hu * cw[:, lane:lane + 1]
            acc = acc + jnp.dot(h.astype(BF16), wd_ref[0, 0, 0], preferred_element_type=F32)
        _slab_write(obuf.at[slot], _layer_norm(ALPHA * x + acc, g_ref[...], b_ref[...]))
        scatter_start(i, slot)

        @pl.when(jnp.logical_not(next_used))
        def _():
            @pl.when(i >= 1)
            def _():
                scatter_wait(1 - slot)
            scatter_wait(slot)


_PAIR_LO = tuple(a for a in range(MOE_EPG) for b in range(a + 1, MOE_EPG))
_PAIR_HI = tuple(b for a in range(MOE_EPG) for b in range(a + 1, MOE_EPG))
N_PAIRS = len(_PAIR_LO)
N_CLASSES = MOE_GROUPS * N_PAIRS


def _moe_schedule(rt, tm, nt):
    T = rt.shape[1]
    i32 = jnp.int32
    g, lo, hi = rt[0].astype(i32), rt[1].astype(i32), rt[2].astype(i32)
    cls = g * N_PAIRS + lo * (2 * MOE_EPG - 1 - lo) // 2 + (hi - lo - 1)
    w_lo, w_hi = rt[3], rt[4]
    cls_sorted, order, w_lo, w_hi = lax.sort((cls, jnp.arange(T, dtype=i32), w_lo, w_hi), num_keys=1,
                                             is_stable=True)
    starts = jnp.searchsorted(cls_sorted, jnp.arange(N_CLASSES + 1, dtype=i32), side="left",
                              method="compare_all").astype(i32)
    counts = starts[1:] - starts[:-1]
    tiles_per = (counts + tm - 1) // tm
    cum = jnp.cumsum(tiles_per)
    n_used = cum[-1]
    t = jnp.arange(nt, dtype=i32)
    tc = jnp.minimum(t, n_used - 1)
    tcls = jnp.searchsorted(cum, tc, side="right", method="compare_all").astype(i32)
    k = tc - (cum[tcls] - tiles_per[tcls])
    nvalid = jnp.where(t < n_used, jnp.clip(counts[tcls] - k * tm, 0, tm), 0).astype(i32)
    base = (starts[tcls] + k * tm).astype(i32)
    pr = tcls % N_PAIRS
    tile_lo = jnp.asarray(_PAIR_LO, i32)[pr]
    tile_hi = jnp.asarray(_PAIR_HI, i32)[pr]
    c_sorted = jnp.zeros((ROUTE_ROWS, T + tm + LANES), F32).at[0, :T].set(w_lo).at[1, :T].set(w_hi)
    return (tcls // N_PAIRS).astype(i32), tile_lo, tile_hi, nvalid, base, order, c_sorted


def _moe(x1, rt, layer, wg, wu, wd, ln_g, ln_b):
    T = x1.shape[0] // SLAB
    D = D_MODEL
    tm = MOE_TILE
    nt = T // tm + N_CLASSES
    win = tm + LANES
    *sched, c_sorted = _moe_schedule(rt, tm, nt)
    sel = jnp.tile(jnp.eye(ROUTE_ROWS, LANES, dtype=BF16), (3, 1))
    lo_map = lambda i, tg, tlo, thi, nv, base, order: (layer, tg[i], tlo[i], 0, 0)
    hi_map = lambda i, tg, tlo, thi, nv, base, order: (layer, tg[i], thi[i], 0, 0)
    const = lambda i, tg, tlo, thi, nv, base, order: (0, 0)
    grid_spec = pltpu.PrefetchScalarGridSpec(
        num_scalar_prefetch=6,
        grid=(nt,),
        in_specs=[
            pl.BlockSpec(memory_space=pl.ANY),
            pl.BlockSpec(memory_space=pl.ANY),
            pl.BlockSpec(sel.shape, const),
            pl.BlockSpec((1, 1, 1, D, MOE_FF), lo_map),
            pl.BlockSpec((1, 1, 1, D, MOE_FF), lo_map),
            pl.BlockSpec((1, 1, 1, MOE_FF, D), lo_map),
            pl.BlockSpec((1, 1, 1, D, MOE_FF), hi_map),
            pl.BlockSpec((1, 1, 1, D, MOE_FF), hi_map),
            pl.BlockSpec((1, 1, 1, MOE_FF, D), hi_map),
            pl.BlockSpec((1, D), const),
            pl.BlockSpec((1, D), const),
        ],
        out_specs=pl.BlockSpec(memory_space=pl.ANY),
        scratch_shapes=[pltpu.VMEM((2, tm * SLAB, LANES), F32), pltpu.VMEM((2, tm * SLAB, LANES), F32),
                        pltpu.VMEM((2, ROUTE_ROWS, win), F32), pltpu.VMEM((win, LANES), F32),
                        pltpu.SemaphoreType.DMA((2,)), pltpu.SemaphoreType.DMA((2,)),
                        pltpu.SemaphoreType.DMA((2,))],
    )
    return pl.pallas_call(
        functools.partial(_moe_kernel, nt=nt, n_tok=T),
        grid_spec=grid_spec,
        out_shape=jax.ShapeDtypeStruct(((T + 2 * tm) * SLAB, LANES), F32),
        compiler_params=_cparams(("arbitrary",)),
    )(*sched, x1, c_sorted, sel, wg, wu, wd, wg, wu, wd, ln_g, ln_b)


def _rglru_stream(z, u_ref, up_ref, un_ref, w_ref, cw_ref, cb_ref, ba_ref, bi_ref, lam_ref,
                  h_out_ref, a_s, b_s, h_s, carry_ref, prev_ok, next_ok, reset):
    tt = u_ref.shape[0]
    u_mid = u_ref[...]
    up = jnp.where(prev_ok, up_ref[...], 0.0)
    un = jnp.where(next_ok, un_ref[...], 0.0)
    head = jnp.concatenate([up, u_mid[:SUBLANES]], axis=0)
    tail = jnp.concatenate([u_mid[tt - SUBLANES:], un], axis=0)
    left = RG_CONV // 2
    u = cb_ref[...]
    for kk in range(RG_CONV):
        s = kk - left
        if s == 0:
            shifted = u_mid
        else:
            rolled = pltpu.roll(u_mid, (-s) % tt, 0)
            if s < 0:
                shifted = jnp.concatenate([head[SUBLANES + s:2 * SUBLANES + s], rolled[SUBLANES:]], axis=0)
            else:
                shifted = jnp.concatenate([rolled[:tt - SUBLANES], tail[s:SUBLANES + s]], axis=0)
        u = u + cw_ref[kk:kk + 1, :] * shifted
    ub = u.astype(BF16)
    sp = jnp.maximum(-lam_ref[...], 0.0) + jnp.log(1.0 + jnp.exp(-jnp.abs(lam_ref[...])))
    for n in range(RG_NT):
        s0 = min(max(n - 1, 0), RG_NT - 3) * LANES
        zz = jnp.dot(ub[:, s0:s0 + RG_BAND], w_ref[n], preferred_element_type=F32)
        sl = slice(n * LANES, (n + 1) * LANES)
        r = jax.nn.sigmoid(zz[:, :LANES] + ba_ref[:, sl])
        ig = jax.nn.sigmoid(zz[:, LANES:] + bi_ref[:, sl])
        log_a = -RG_C * r * sp[:, sl]
        a = jnp.exp(log_a)
        a_s[:, sl] = a
        b_s[:, sl] = jnp.sqrt(-jnp.tanh(log_a) * (a * a + 1.0)) * ig * u[:, sl]

    @pl.when(reset)
    def _():
        carry_ref[...] = jnp.zeros_like(carry_ref)

    n_grp = tt // SUBLANES

    def body(gi, h):
        g = gi if z == 0 else n_grp - 1 - gi
        base = pl.multiple_of(g * SUBLANES, SUBLANES)
        a8 = a_s[pl.ds(base, SUBLANES), :]
        b8 = b_s[pl.ds(base, SUBLANES), :]
        rows = [None] * SUBLANES
        order = range(SUBLANES) if z == 0 else range(SUBLANES - 1, -1, -1)
        for r_ in order:
            h = a8[r_:r_ + 1] * h + b8[r_:r_ + 1]
            rows[r_] = h
        h_s[pl.ds(base, SUBLANES), :] = jnp.concatenate(rows, axis=0)
        return h

    h_last = lax.fori_loop(0, n_grp, body, carry_ref[...])
    carry_ref[...] = h_last
    h_out_ref[...] = h_s[...].astype(h_out_ref.dtype)


def _rglru_scan_kernel(uf_ref, ufp_ref, ufn_ref, ubk_ref, ubp_ref, ubn_ref, wf_ref, wb_ref, cw_ref, cb_ref,
                       ba_ref, bi_ref, lam_ref, hf_ref, hb_ref, a_s, b_s, h_s, cf_ref, cbk_ref, *, segs, nt):
    i = pl.program_id(0)
    tt = uf_ref.shape[0]
    _, tps, j = _seq_pos(i, tt, segs)
    _rglru_stream(0, uf_ref, ufp_ref, ufn_ref, wf_ref, cw_ref, cb_ref, ba_ref.at[0:1], bi_ref.at[0:1],
                  lam_ref.at[0:1], hf_ref, a_s, b_s, h_s, cf_ref, j > 0, j < tps - 1, j == 0)
    ib = nt - 1 - i
    _, tps_b, jb = _seq_pos(ib, tt, segs)
    _rglru_stream(1, ubk_ref, ubp_ref, ubn_ref, wb_ref, cw_ref, cb_ref, ba_ref.at[1:2], bi_ref.at[1:2],
                  lam_ref.at[1:2], hb_ref, a_s, b_s, h_s, cbk_ref, jb > 0, jb < tps_b - 1, jb == tps_b - 1)


def _rglru_scan(u_pre, w_band, conv_w, conv_b, ba, bi, lam, segs):
    T, W = u_pre.shape
    tt = SCAN_TILE
    nt = T // tt
    hpt = tt // SUBLANES
    n_h = T // SUBLANES
    fwd = lambda i: (i, 0)
    bwd = lambda i: (nt - 1 - i, 0)

    def halo_specs(idx):
        return [
            pl.BlockSpec((tt, W), lambda i: (idx(i), 0)),
            pl.BlockSpec((SUBLANES, W), lambda i: (jnp.maximum(idx(i) * hpt - 1, 0), 0)),
            pl.BlockSpec((SUBLANES, W), lambda i: (jnp.minimum((idx(i) + 1) * hpt, n_h - 1), 0)),
        ]

    const2 = lambda i: (0, 0)
    const3 = lambda i: (0, 0, 0)
    in_specs = halo_specs(lambda i: i) + halo_specs(lambda i: nt - 1 - i) + [
        pl.BlockSpec(w_band.shape[1:], const3),
        pl.BlockSpec(w_band.shape[1:], const3),
        pl.BlockSpec(conv_w.shape, const2),
        pl.BlockSpec(conv_b.shape, const2),
        pl.BlockSpec(ba.shape, const2),
        pl.BlockSpec(bi.shape, const2),
        pl.BlockSpec(lam.shape, const2),
    ]
    return pl.pallas_call(
        functools.partial(_rglru_scan_kernel, segs=segs, nt=nt),
        grid=(nt,),
        in_specs=in_specs,
        out_specs=[pl.BlockSpec((tt, W), fwd), pl.BlockSpec((tt, W), bwd)],
        out_shape=[jax.ShapeDtypeStruct((T, W), BF16), jax.ShapeDtypeStruct((T, W), BF16)],
        scratch_shapes=[pltpu.VMEM((tt, W), F32), pltpu.VMEM((tt, W), F32), pltpu.VMEM((tt, W), F32),
                        pltpu.VMEM((1, W), F32), pltpu.VMEM((1, W), F32)],
        compiler_params=_cparams(("arbitrary",)),
    )(u_pre, u_pre, u_pre, u_pre, u_pre, u_pre, w_band[0], w_band[1], conv_w, conv_b, ba, bi, lam)


def _rglru_band_weights(wa, wi):
    def dense(w):
        eye = jnp.eye(RG_BLOCKS, dtype=w.dtype)
        return jnp.einsum("ncd,nm->ncmd", w, eye).reshape(RG_WIDTH, RG_WIDTH)

    out = []
    for z in range(2):
        da, di = dense(wa[z]), dense(wi[z])
        tiles = []
        for n in range(RG_NT):
            s0 = min(max(n - 1, 0), RG_NT - 3) * LANES
            sl = slice(n * LANES, (n + 1) * LANES)
            tiles.append(jnp.concatenate([da[s0:s0 + RG_BAND, sl], di[s0:s0 + RG_BAND, sl]], axis=1))
        out.append(jnp.stack(tiles))
    return jnp.stack(out).astype(BF16)


def _gla_stream(z, q_ref, k_ref, v_ref, lr_ref, wa2_ref, ba_ref, o_ref, st_ref, reset):
    tt = q_ref.shape[0]
    C = GLA_CHUNK

    @pl.when(reset)
    def _():
        st_ref[z] = jnp.zeros(st_ref.shape[1:], F32)

    zz = jnp.dot(lr_ref[...].astype(BF16), wa2_ref[:, z * GLA_DK:(z + 1) * GLA_DK],
                 preferred_element_type=F32) + ba_ref[:, z * GLA_DK:(z + 1) * GLA_DK]
    log_a = -(jnp.maximum(-zz, 0.0) + jnp.log(1.0 + jnp.exp(-jnp.abs(zz)))) / GLA_TAU
    ri = lax.broadcasted_iota(jnp.int32, (C, C), 0)
    ci = lax.broadcasted_iota(jnp.int32, (C, C), 1)
    causal = (ri >= ci) if z == 0 else (ri <= ci)
    tri3 = jnp.concatenate([causal.astype(BF16)] * 3, axis=1)
    la_p = _split3(log_a)
    mid = C // 2 if z == 0 else C - 1 - C // 2
    last = C - 1 if z == 0 else 0
    scale = GLA_DKH ** -0.5
    chunks = range(tt // C) if z == 0 else range(tt // C - 1, -1, -1)
    for c in chunks:
        rs = slice(c * C, (c + 1) * C)
        b = jnp.dot(tri3, jnp.concatenate([p_[rs] for p_ in la_p], axis=0), preferred_element_type=F32)
        b_mid = b[mid:mid + 1]
        b_last = b[last:last + 1]
        qc = q_ref[rs, :] * scale
        kc = k_ref[rs, :]
        e = jnp.exp(b - b_mid)
        qd32 = qc * e
        kd32 = kc * (1.0 / e)
        qd = qd32.astype(BF16)
        kd = kd32.astype(BF16)
        ks = (kd32 * jnp.exp(b_last - b_mid)).astype(BF16)
        qb = (qd32 * jnp.exp(b_mid)).astype(BF16)
        dec = jnp.exp(b_last)
        for h in range(GLA_HEADS):
            ksl = slice(h * GLA_DKH, (h + 1) * GLA_DKH)
            vsl = slice(h * GLA_DVH, (h + 1) * GLA_DVH)
            vc = v_ref[rs, vsl]
            att = lax.dot_general(qd[:, ksl], kd[:, ksl], (((1,), (1,)), ((), ())), preferred_element_type=F32)
            att = jnp.where(causal, att, 0.0)
            o = jnp.dot(att.astype(BF16), vc, preferred_element_type=F32)
            st = st_ref[z, h]
            o = o + lax.dot_general(qb[:, ksl], st.astype(BF16), (((1,), (1,)), ((), ())),
                                    preferred_element_type=F32)
            o_ref[rs, vsl] = o
            upd = lax.dot_general(vc, ks[:, ksl], (((0,), (0,)), ((), ())), preferred_element_type=F32)
            st_ref[z, h] = st * dec[:, ksl] + upd


def _gla_kernel(qf, kf, vf, lf, qb, kb, vb, lb, wa2_ref, ba_ref, of_ref, ob_ref, st_ref, *, segs, nt):
    i = pl.program_id(0)
    tt = qf.shape[0]
    _, tps, j = _seq_pos(i, tt, segs)
    _gla_stream(0, qf, kf, vf, lf, wa2_ref, ba_ref, of_ref, st_ref, j == 0)
    ib = nt - 1 - i
    _, tps_b, jb = _seq_pos(ib, tt, segs)
    _gla_stream(1, qb, kb, vb, lb, wa2_ref, ba_ref, ob_ref, st_ref, jb == tps_b - 1)


def _gla(q, k, v, lr, wa2, ba, segs):
    T = q.shape[0]
    tt = SCAN_TILE
    nt = T // tt
    fwd = lambda i: (i, 0)
    bwd = lambda i: (nt - 1 - i, 0)
    in_specs = []
    for idx in (fwd, bwd):
        in_specs += [pl.BlockSpec((tt, GLA_DK), idx), pl.BlockSpec((tt, GLA_DK), idx),
                     pl.BlockSpec((tt, GLA_DV), idx), pl.BlockSpec((tt, LANES), idx)]
    in_specs += [pl.BlockSpec(wa2.shape, lambda i: (0, 0)), pl.BlockSpec(ba.shape, lambda i: (0, 0))]
    return pl.pallas_call(
        functools.partial(_gla_kernel, segs=segs, nt=nt),
        grid=(nt,),
        in_specs=in_specs,
        out_specs=[pl.BlockSpec((tt, GLA_DV), fwd), pl.BlockSpec((tt, GLA_DV), bwd)],
        out_shape=[jax.ShapeDtypeStruct((T, GLA_DV), F32), jax.ShapeDtypeStruct((T, GLA_DV), F32)],
        scratch_shapes=[pltpu.VMEM((2, GLA_HEADS, GLA_DVH, GLA_DKH), F32)],
        compiler_params=_cparams(("arbitrary",)),
    )(q, k, v, lr, q, k, v, lr, wa2, ba)


def _rope_tables(s_max):
    half = HEAD_DIM // 2
    inv = ROPE_THETA ** (-jnp.arange(half, dtype=F32) / half)
    ang = jnp.arange(s_max, dtype=F32)[:, None] * inv[None, :]
    cos, sin = jnp.cos(ang), jnp.sin(ang)
    cos_t = jnp.tile(jnp.concatenate([cos, cos], axis=1), (1, LANES // HEAD_DIM))
    sin_t = jnp.tile(jnp.concatenate([-sin, sin], axis=1), (1, LANES // HEAD_DIM))
    return cos_t, sin_t


def _router_weights(wgr, bgr, wer, ber):
    n_e = MOE_GROUPS * MOE_EPG
    w = jnp.concatenate([jnp.transpose(wer, (0, 2, 1)).reshape(n_e, D_MODEL), wgr.T], axis=0)
    b = jnp.concatenate([ber.reshape(n_e), bgr])
    n_pad = ROUTER_ROWS - w.shape[0]
    w = jnp.pad(w, ((0, n_pad), (0, 0)))
    b = jnp.concatenate([b, jnp.full((n_pad,), NEG_INF, F32)])
    w_hi = w.astype(BF16)
    w_mid = (w - w_hi.astype(F32)).astype(BF16)
    return jnp.concatenate([w_hi, w_mid, w_hi], axis=1), b[:, None]


def _mixer(kind, jl, x, segs, p, cos_t, sin_t, lg1, lb1, wr, br):
    D = D_MODEL
    T = sum(n * s for n, s in segs)
    if kind == 0:
        nq = A_HEADS * HEAD_DIM
        nkv = A_KV_HEADS * HEAD_DIM
        w = p["a_wqkv"][jl]
        wq, wk, wv = w[:, :nq], w[:, nq:nq + nkv], w[:, nq + nkv:]
        dup = lambda t: jnp.repeat(t.reshape(D, A_KV_HEADS, 1, HEAD_DIM), 2, axis=2).reshape(D, 2 * nkv)
        w_ext = jnp.concatenate([wq, dup(wk), dup(wv)], axis=1).astype(BF16)
        q, kd, vd = _proj_rope(x, T, w_ext, cos_t, sin_t, segs, dil=1, n_rope=(nq + 2 * nkv) // LANES,
                               scale_cols=nq, scale=HEAD_DIM ** -0.5, out_cols=(nq, 2 * nkv, 2 * nkv),
                               out_dtypes=(BF16, BF16, BF16))
        o = _band_attn(q, kd, vd, segs, dil=1, radius=A_RADIUS, q_per_k=2, sink=p["a_sink"][jl])
        return _out_ln([o], x, T, p["a_wo"][jl].astype(BF16), lg1, lb1, wr, br, mode="plain")
    if kind == 1:
        gate, u_pre = _proj(x, T, p["b_win"][jl].astype(BF16), (RG_WIDTH, RG_WIDTH), (BF16, F32))
        w_band = _rglru_band_weights(p["b_wa"][jl], p["b_wi"][jl])
        hf, hb = _rglru_scan(u_pre, w_band, p["b_conv_w"][jl], p["b_conv_b"][jl][None, :], p["b_ba"][jl],
                             p["b_bi"][jl], p["b_lam"][jl], segs)
        return _out_ln([gate, hf, hb], x, T, p["b_wo"][jl].astype(BF16), lg1, lb1, wr, br, mode="rglru")
    if kind == 2:
        wa1 = jnp.concatenate([p["c_wa1"][jl][0], p["c_wa1"][jl][1]], axis=1)
        wa1 = jnp.pad(wa1, ((0, 0), (0, LANES - wa1.shape[1])))
        w_all = jnp.concatenate([p["c_wqkvg"][jl], wa1], axis=1).astype(BF16)
        q, k, v, g, lr = _proj(x, T, w_all, (GLA_DK, GLA_DK, GLA_DV, GLA_DV, LANES), (F32, F32, BF16, BF16, F32))
        wa2 = jnp.zeros((LANES, 2 * GLA_DK), F32)
        wa2 = wa2.at[:GLA_RANK, :GLA_DK].set(p["c_wa2"][jl][0])
        wa2 = wa2.at[GLA_RANK:2 * GLA_RANK, GLA_DK:].set(p["c_wa2"][jl][1])
        of, ob = _gla(q, k, v, lr, wa2.astype(BF16), p["c_ba"][jl].reshape(1, 2 * GLA_DK), segs)
        return _out_ln([of, ob, g], x, T, p["c_wo"][jl].astype(BF16), lg1, lb1, wr, br, mode="gla",
                       extra=p["c_norm_g"][jl][None, :])
    ols = []
    for gi, (window, dil) in enumerate(DIL_GROUPS):
        w_g = p["d_wqkv"][jl][:, gi * 3 * DIL_WIDTH:(gi + 1) * 3 * DIL_WIDTH].astype(BF16)
        q, k, v = _proj_rope(x, T, w_g, cos_t, sin_t, segs, dil=dil, n_rope=2 * DIL_WIDTH // LANES,
                             scale_cols=DIL_WIDTH, scale=HEAD_DIM ** -0.5,
                             out_cols=(DIL_WIDTH,) * 3, out_dtypes=(BF16,) * 3)
        ols.append(_band_attn(q, k, v, segs, dil=dil, radius=window // (2 * dil), q_per_k=1, want_lse=True))
    return _out_ln(ols, x, T, p["d_wo"][jl].astype(BF16), lg1, lb1, wr, br, mode="dil")


def kernel(x_prompt, x_sample, ln_g, ln_b, a_wqkv, a_sink, a_wo, b_win, b_conv_w, b_conv_b, b_wa, b_ba, b_wi,
           b_bi, b_lam, b_wo, c_wqkvg, c_wa1, c_wa2, c_ba, c_norm_g, c_wo, d_wqkv, d_wo, m_wgr, m_bgr, m_wer,
           m_ber, m_wg, m_wu, m_wd):
    p = dict(a_wqkv=a_wqkv, a_sink=a_sink, a_wo=a_wo, b_win=b_win, b_conv_w=b_conv_w, b_conv_b=b_conv_b,
             b_wa=b_wa, b_ba=b_ba, b_wi=b_wi, b_bi=b_bi, b_lam=b_lam, b_wo=b_wo, c_wqkvg=c_wqkvg, c_wa1=c_wa1,
             c_wa2=c_wa2, c_ba=c_ba, c_norm_g=c_norm_g, c_wo=c_wo, d_wqkv=d_wqkv, d_wo=d_wo)
    n_p, s_p, D = x_prompt.shape
    n_s, s_s, _ = x_sample.shape
    segs = ((n_p, s_p), (n_s, s_s))
    assert (n_p * s_p) % s_s == 0
    t_p, T = n_p * s_p, n_p * s_p + n_s * s_s
    x = (x_prompt.reshape(t_p, D), x_sample.reshape(T - t_p, D))
    cos_t, sin_t = _rope_tables(max(s_p, s_s))
    wg_b, wu_b, wd_b = m_wg.astype(BF16), m_wu.astype(BF16), m_wd.astype(BF16)

    for layer in range(DEPTH):
        lg1, lb1 = ln_g[layer, 0][None, :], ln_b[layer, 0][None, :]
        lg2, lb2 = ln_g[layer, 1][None, :], ln_b[layer, 1][None, :]
        wr, br = _router_weights(m_wgr[layer], m_bgr[layer], m_wer[layer], m_ber[layer])
        x1, rt = _mixer(layer % 4, layer // 4, x, segs, p, cos_t, sin_t, lg1, lb1, wr, br)
        x = _moe(x1, rt, layer, wg_b, wu_b, wd_b, lg2, lb2)

    return _from_slab(x, 0, t_p).reshape(n_p, s_p, D), _from_slab(x, t_p, T - t_p).reshape(n_s, s_s, D)
```
